```python
import jax, jax.numpy as jnp
from jax import lax
import numpy as np

D_MODEL = 1024
BATCH = 16
SEQ = 256
DEPTH = 4
DEC_BATCH = 4
DEC_SEQ = 2048
PAST_LEN = 512

GRID_W = 64
N_Q_HEADS = 8
N_KV_HEADS = 2
GQA_GROUP = N_Q_HEADS // N_KV_HEADS
HEAD_DIM = 64
AXIS_DIM = HEAD_DIM // 2
ATTN_WIDTH = N_Q_HEADS * HEAD_DIM
KV_WIDTH = N_KV_HEADS * HEAD_DIM
ROPE_THETA = 10000.0
Q_BLOCK = 128
HGRN_HEADS = 4
HGRN_K = 128
HGRN_V = 128
HGRN_WIDTH = HGRN_HEADS * HGRN_K
HGRN_CHUNK = 64
N_EXPERTS = 16
N_GROUPS = 4
EXPERTS_PER_GROUP = N_EXPERTS // N_GROUPS
TOP_K = 2
D_FF = 512
EPS = 1e-6
IN_WIDTHS = (ATTN_WIDTH, KV_WIDTH, KV_WIDTH, HGRN_WIDTH, HGRN_WIDTH, HGRN_WIDTH, HGRN_WIDTH, HGRN_WIDTH, D_MODEL, D_MODEL)
IN_COLS = ATTN_WIDTH + 2 * KV_WIDTH + 5 * HGRN_WIDTH + 2 * D_MODEL

kernel_name = 'hybrid_gqa_hgrn2_moe_diffusion_step'

F32 = jnp.float32


def split_points():
    pts, acc = [], 0
    for w in IN_WIDTHS[:-1]:
        acc += w
        pts.append(acc)
    return pts


def rms_norm(x, w):
    xf = x.astype(F32)
    y = xf * lax.rsqrt(jnp.mean(xf * xf, axis=-1, keepdims=True) + EPS)
    return (y * w.astype(F32)).astype(x.dtype)


def modulation(silu_cond, w_mod_l, b_mod_l):
    m = silu_cond @ w_mod_l + b_mod_l
    return jnp.split(m[:, None, :], 6, axis=-1)


def modulate(x, w, shift, scale):
    return rms_norm(x, w) * (1 + scale) + shift


def axial_rope_tables(n_tokens):
    rows = n_tokens // GRID_W
    row_ids = jnp.repeat(jnp.arange(rows), GRID_W).astype(F32)
    col_ids = jnp.tile(jnp.arange(GRID_W), rows).astype(F32)
    inv_freq = ROPE_THETA ** (-jnp.arange(0, AXIS_DIM, 2, dtype=F32) / AXIS_DIM)
    ang_r = row_ids[:, None] * inv_freq[None, :]
    ang_c = col_ids[:, None] * inv_freq[None, :]
    ang = jnp.concatenate([ang_r, ang_r, ang_c, ang_c], axis=-1)
    return jnp.cos(ang), jnp.sin(ang)


def apply_axial_rope(x, cos, sin):
    quarter = AXIS_DIM // 2

    def rot_half(u):
        return jnp.concatenate([-u[..., quarter:], u[..., :quarter]], axis=-1)

    rotated = jnp.concatenate([rot_half(x[..., :AXIS_DIM]), rot_half(x[..., AXIS_DIM:])], axis=-1)
    return (x.astype(F32) * cos + rotated.astype(F32) * sin).astype(x.dtype)


def block_attention(q, k, v):
    b, hkv, g, t, hd = q.shape
    nb = t // Q_BLOCK
    qb = q.reshape(b, hkv, g, nb, Q_BLOCK, hd).transpose(3, 0, 1, 2, 4, 5)
    kf, vf = k.astype(F32), v.astype(F32)
    scale = hd ** -0.5

    def one_block(qblk):
        s = jnp.einsum('bhgqd,bhkd->bhgqk', qblk.astype(F32), kf) * scale
        p = jax.nn.softmax(s, axis=-1)
        return jnp.einsum('bhgqk,bhkd->bhgqd', p, vf).astype(q.dtype)

    out = lax.map(one_block, qb)
    return out.transpose(1, 2, 3, 0, 4, 5).reshape(b, hkv * g, t, hd)


def hgrn_chunk_scan(q, k, v, log_f, s0):
    b, h, t, _ = q.shape
    dv = v.shape[-1]
    nc = t // HGRN_CHUNK

    def to_chunks(a):
        return a.reshape(b, h, nc, HGRN_CHUNK, a.shape[-1]).transpose(2, 0, 1, 3, 4).astype(F32)

    incl = jnp.tril(jnp.ones((HGRN_CHUNK, HGRN_CHUNK), dtype=bool))[:, :, None]

    def step(state, blk):
        qc, kc, vc, gc = blk
        bcum = jnp.cumsum(gc, axis=2)
        o_inter = jnp.einsum('bhck,bhkv->bhcv', qc * jnp.exp(bcum), state)
        diff = bcum[:, :, :, None, :] - bcum[:, :, None, :, :]
        decay = jnp.exp(jnp.where(incl, diff, -jnp.inf))
        scores = jnp.einsum('bhtk,bhsk,bhtsk->bhts', qc, kc, decay)
        o_intra = jnp.einsum('bhts,bhsv->bhtv', scores, vc)
        b_last = bcum[:, :, -1:, :]
        k_dec = kc * jnp.exp(b_last - bcum)
        new_state = jnp.exp(b_last[:, :, 0, :])[..., None] * state + jnp.einsum('bhsk,bhsv->bhkv', k_dec, vc)
        return new_state, o_inter + o_intra

    s_final, o = lax.scan(step, s0.astype(F32), (to_chunks(q), to_chunks(k), to_chunks(v), to_chunks(log_f)))
    return o.transpose(1, 2, 0, 3, 4).reshape(b, h, t, dv), s_final


def layer_lower_bounds(lb_logits):
    p = jax.nn.softmax(lb_logits.astype(F32), axis=0)
    return jnp.cumsum(p, axis=0) - p[0:1]


def hgrn_branch(h_q, h_ff, h_fb, h_i, h_g, lb_f, lb_b, hnorm_w, s0_f, s0_b):
    b, t, _ = h_q.shape

    def heads(a, d):
        return a.reshape(b, t, HGRN_HEADS, d).transpose(0, 2, 1, 3)

    q = heads(jax.nn.silu(h_q.astype(F32)), HGRN_K) * (HGRN_K ** -0.5)
    v = heads(h_i.astype(F32), HGRN_V)

    def gates(hf, lb):
        f = lb + (1.0 - lb) * jax.nn.sigmoid(hf.astype(F32))
        return heads(1.0 - f, HGRN_K), heads(jnp.log(f), HGRN_K)

    k_f, lf_f = gates(h_ff, lb_f)
    k_b, lf_b = gates(h_fb, lb_b)
    o_f, s_f = hgrn_chunk_scan(q, k_f, v, lf_f, s0_f)

    def flip(a):
        return jnp.flip(a, axis=2)

    o_b, s_b = hgrn_chunk_scan(flip(q), flip(k_b), flip(v), flip(lf_b), s0_b)
    o = o_f + flip(o_b)
    o = rms_norm(o, hnorm_w) * jax.nn.silu(heads(h_g.astype(F32), HGRN_V))
    o = o.transpose(0, 2, 1, 3).reshape(b, t, HGRN_WIDTH).astype(h_q.dtype)
    return o, s_f, s_b


def token_mixer(h, lw, rope, ctx):
    b, t, _ = h.shape
    a_q, a_k, a_v, h_q, h_ff, h_fb, h_i, h_g, g_attn, g_hgrn = jnp.split(h @ lw['w_in'], split_points(), axis=-1)
    q = rms_norm(a_q.reshape(b, t, N_Q_HEADS, HEAD_DIM), lw['q_norm']).transpose(0, 2, 1, 3)
    k = rms_norm(a_k.reshape(b, t, N_KV_HEADS, HEAD_DIM), lw['k_norm']).transpose(0, 2, 1, 3)
    v = a_v.reshape(b, t, N_KV_HEADS, HEAD_DIM).transpose(0, 2, 1, 3)
    if ctx is None:
        keys, vals = k, v
        s0_f = jnp.zeros((b, HGRN_HEADS, HGRN_K, HGRN_V), F32)
        s0_b = jnp.zeros((b, HGRN_HEADS, HGRN_K, HGRN_V), F32)
    else:
        ctx_k, ctx_v, s0_f, s0_b = ctx
        cos, sin = rope
        q = apply_axial_rope(q, cos, sin)
        k = apply_axial_rope(k, cos, sin)
        keys = jnp.concatenate([k, ctx_k.astype(k.dtype)], axis=2)
        vals = jnp.concatenate([v, ctx_v.astype(v.dtype)], axis=2)
    attn = block_attention(q.reshape(b, N_KV_HEADS, GQA_GROUP, t, HEAD_DIM), keys, vals)
    attn = attn.transpose(0, 2, 1, 3).reshape(b, t, ATTN_WIDTH)
    rec, s_f, s_b = hgrn_branch(h_q, h_ff, h_fb, h_i, h_g, lw['lb_f'], lw['lb_b'], lw['hnorm'], s0_f, s0_b)
    merged = (jax.nn.sigmoid(g_attn) * (attn @ lw['w_br_attn'])
              + jax.nn.sigmoid(g_hgrn) * (rec @ lw['w_br_hgrn']))
    return merged @ lw['w_out'], (k, v, s_f, s_b)


def moe_ffn(h, lw):
    b, t, d = h.shape
    xt = h.reshape(b * t, d)
    scores = jax.nn.sigmoid((xt @ lw['w_router']).astype(F32))
    sel = scores + lw['b_router'].astype(F32)
    grp_score = lax.top_k(sel.reshape(-1, N_GROUPS, EXPERTS_PER_GROUP), TOP_K)[0].sum(-1)
    best = jnp.argmax(grp_score, axis=-1)
    in_group = (jnp.arange(N_EXPERTS) // EXPERTS_PER_GROUP)[None, :] == best[:, None]
    _, idx = lax.top_k(jnp.where(in_group, sel, -jnp.inf), TOP_K)
    w = jnp.take_along_axis(scores, idx, axis=-1)
    w = w / jnp.sum(w, axis=-1, keepdims=True)
    gates = jnp.sum(jax.nn.one_hot(idx, N_EXPERTS, dtype=F32) * w[..., None], axis=1)
    act = jax.nn.silu(jnp.einsum('nd,edf->nef', xt, lw['w_gate'])) * jnp.einsum('nd,edf->nef', xt, lw['w_up'])
    y = jnp.einsum('nef,efd->nd', act * gates[..., None].astype(act.dtype), lw['w_down'])
    return y.reshape(b, t, d)


def run_layer(x, silu_cond, lw, rope, ctx):
    sh1, sc1, g1, sh2, sc2, g2 = modulation(silu_cond, lw['w_mod'], lw['b_mod'])
    out, ctx_tensors = token_mixer(modulate(x, lw['norm1'], sh1, sc1), lw, rope, ctx)
    x = x + g1 * out
    x = x + g2 * moe_ffn(modulate(x, lw['norm2'], sh2, sc2), lw)
    return x, ctx_tensors


def setup_inputs(seed: int = 0) -> dict:
    key = jax.random.key(seed)
    ks = jax.random.split(key, 32)

    def nrm(k, shape, scale=1.0):
        return jax.random.normal(k, shape, F32) * scale

    return {
        'x_prompt': nrm(ks[0], (BATCH, SEQ, D_MODEL)),
        'x_sample': nrm(ks[1], (DEC_BATCH, DEC_SEQ, D_MODEL)),
        'cache_k': nrm(ks[2], (DEC_BATCH, DEPTH, N_KV_HEADS, PAST_LEN, HEAD_DIM)),
        'cache_v': nrm(ks[3], (DEC_BATCH, DEPTH, N_KV_HEADS, PAST_LEN, HEAD_DIM)),
        'state_fwd': nrm(ks[4], (DEC_BATCH, DEPTH, HGRN_HEADS, HGRN_K, HGRN_V), 0.5),
        'state_bwd': nrm(ks[5], (DEC_BATCH, DEPTH, HGRN_HEADS, HGRN_K, HGRN_V), 0.5),
        'c': nrm(ks[6], (DEC_BATCH, D_MODEL)),
        'c_ctx': nrm(ks[7], (D_MODEL,)),
        'norm1_w': 1.0 + nrm(ks[8], (DEPTH, D_MODEL), 0.02),
        'norm2_w': 1.0 + nrm(ks[9], (DEPTH, D_MODEL), 0.02),
        'w_mod': nrm(ks[10], (DEPTH, D_MODEL, 6 * D_MODEL), 0.5 * D_MODEL ** -0.5),
        'b_mod': nrm(ks[11], (DEPTH, 6 * D_MODEL), 0.02),
        'w_in': nrm(ks[12], (DEPTH, D_MODEL, IN_COLS), D_MODEL ** -0.5),
        'q_norm_w': 1.0 + nrm(ks[13], (DEPTH, HEAD_DIM), 0.02),
        'k_norm_w': 1.0 + nrm(ks[14], (DEPTH, HEAD_DIM), 0.02),
        'hgrn_lb_fwd': 1.0 + nrm(ks[15], (DEPTH, HGRN_WIDTH), 0.1),
        'hgrn_lb_bwd': 1.0 + nrm(ks[16], (DEPTH, HGRN_WIDTH), 0.1),
        'hgrn_norm_w': 1.0 + nrm(ks[17], (DEPTH, HGRN_V), 0.02),
        'w_br_attn': nrm(ks[18], (DEPTH, ATTN_WIDTH, D_MODEL), ATTN_WIDTH ** -0.5),
        'w_br_hgrn': nrm(ks[19], (DEPTH, HGRN_WIDTH, D_MODEL), HGRN_WIDTH ** -0.5),
        'w_out': nrm(ks[20], (DEPTH, D_MODEL, D_MODEL), D_MODEL ** -0.5),
        'w_router': nrm(ks[21], (D_MODEL, N_EXPERTS), D_MODEL ** -0.5),
        'b_router': nrm(ks[22], (N_EXPERTS,), 0.01),
        'w_exp_gate': nrm(ks[23], (DEPTH, N_EXPERTS, D_MODEL, D_FF), D_MODEL ** -0.5),
        'w_exp_up': nrm(ks[24], (DEPTH, N_EXPERTS, D_MODEL, D_FF), D_MODEL ** -0.5),
        'w_exp_down': nrm(ks[25], (DEPTH, N_EXPERTS, D_FF, D_MODEL), D_FF ** -0.5),
    }


def reference(x_prompt, x_sample, cache_k, cache_v, state_fwd, state_bwd, c, c_ctx,
              norm1_w, norm2_w, w_mod, b_mod, w_in, q_norm_w, k_norm_w, hgrn_lb_fwd, hgrn_lb_bwd,
              hgrn_norm_w, w_br_attn, w_br_hgrn, w_out, w_router, b_router,
              w_exp_gate, w_exp_up, w_exp_down):
    lb_f_all = layer_lower_bounds(hgrn_lb_fwd)
    lb_b_all = layer_lower_bounds(hgrn_lb_bwd)

    def layer_weights(l):
        return {'norm1': norm1_w[l], 'norm2': norm2_w[l], 'w_mod': w_mod[l], 'b_mod': b_mod[l],
                'w_in': w_in[l], 'q_norm': q_norm_w[l], 'k_norm': k_norm_w[l],
                'lb_f': lb_f_all[l], 'lb_b': lb_b_all[l], 'hnorm': hgrn_norm_w[l],
                'w_br_attn': w_br_attn[l], 'w_br_hgrn': w_br_hgrn[l], 'w_out': w_out[l],
                'w_router': w_router, 'b_router': b_router,
                'w_gate': w_exp_gate[l], 'w_up': w_exp_up[l], 'w_down': w_exp_down[l]}

    silu_ctx = jax.nn.silu(c_ctx)[None, :]
    xp = x_prompt
    ks_out, vs_out, sf_out, sb_out = [], [], [], []
    for l in range(DEPTH):
        xp, (kc, vc, sf, sb) = run_layer(xp, silu_ctx, layer_weights(l), None, None)
        ks_out.append(kc)
        vs_out.append(vc)
        sf_out.append(sf)
        sb_out.append(sb)
    new_cache_k = jnp.stack(ks_out, axis=1)
    new_cache_v = jnp.stack(vs_out, axis=1)
    new_state_fwd = jnp.stack(sf_out, axis=1)
    new_state_bwd = jnp.stack(sb_out, axis=1)

    rope = axial_rope_tables(x_sample.shape[1])
    silu_c = jax.nn.silu(c)
    xs = x_sample
    for l in range(DEPTH):
        ctx = (cache_k[:, l], cache_v[:, l], state_fwd[:, l], state_bwd[:, l])
        xs, _ = run_layer(xs, silu_c, layer_weights(l), rope, ctx)

    return (xp, xs, new_cache_k, new_cache_v, new_state_fwd, new_state_bwd)
```

```python
import functools

import numpy as np
import jax
import jax.numpy as jnp
from jax import lax
from jax.experimental import pallas as pl
from jax.experimental.pallas import tpu as pltpu

F32 = jnp.float32
BF16 = jnp.bfloat16

D_MODEL = 1024
BATCH = 16
SEQ = 256
DEPTH = 4
DEC_BATCH = 4
DEC_SEQ = 2048
PAST_LEN = 512
GRID_W = 64
N_Q_HEADS = 8
N_KV_HEADS = 2
GQA_GROUP = N_Q_HEADS // N_KV_HEADS
HEAD_DIM = 64
AXIS_DIM = HEAD_DIM // 2
ATTN_WIDTH = N_Q_HEADS * HEAD_DIM
KV_WIDTH = N_KV_HEADS * HEAD_DIM
ROPE_THETA = 10000.0
HGRN_HEADS = 4
HGRN_K = 128
HGRN_V = 128
HGRN_WIDTH = HGRN_HEADS * HGRN_K
N_EXPERTS = 16
N_GROUPS = 4
EXPERTS_PER_GROUP = N_EXPERTS // N_GROUPS
D_FF = 512
EPS = 1e-6
IN_COLS = ATTN_WIDTH + 2 * KV_WIDTH + 5 * HGRN_WIDTH + 2 * D_MODEL

N_CTX = BATCH * SEQ
N_LAT = DEC_BATCH * DEC_SEQ
N_TOK = N_CTX + N_LAT
N_COND = 8
N_MOD = 6

C_Q = 0
C_K = C_Q + ATTN_WIDTH
C_V = C_K + KV_WIDTH
C_HQ = C_V + KV_WIDTH
C_FF = C_HQ + HGRN_WIDTH
C_FB = C_FF + HGRN_WIDTH
C_HI = C_FB + HGRN_WIDTH
C_HG = C_HI + HGRN_WIDTH
C_GA = C_HG + HGRN_WIDTH
C_GH = C_GA + D_MODEL

TOK_TILE = 512
MOE_TILE = 1024
Q_TILE = 512
CHUNK = 64
SUB = 16
VMEM_LIMIT = 56 * 1024 * 1024

_HI = lax.Precision.HIGHEST


def _sigmoid(x):
    return 1.0 / (1.0 + jnp.exp(-x))


def _dot(a, b):
    return jnp.dot(a, b, preferred_element_type=F32)


def _dot_nt(a, b, precision=None):
    return lax.dot_general(a, b, (((1,), (1,)), ((), ())), precision=precision,
                           preferred_element_type=F32)


def _dot_tn(a, b, precision=None):
    return lax.dot_general(a, b, (((0,), (0,)), ((), ())), precision=precision,
                           preferred_element_type=F32)


def _dot_exact(a, b):
    return jnp.dot(a, b, precision=_HI, preferred_element_type=F32)


def _params(*sem):
    return pltpu.CompilerParams(dimension_semantics=sem, vmem_limit_bytes=VMEM_LIMIT)


def _mod_kernel(cond_ref, w_ref, b_ref, out_ref):
    c = cond_ref[...]
    sc = c * _sigmoid(c)
    out_ref[...] = _dot(sc.astype(BF16), w_ref[...].astype(BF16)) + b_ref[...]


def _modulation(cond, w_mod, b_mod):
    out = pl.pallas_call(
        _mod_kernel,
        grid=(DEPTH, N_MOD),
        in_specs=[
            pl.BlockSpec((N_COND, D_MODEL), lambda l, j: (0, 0)),
            pl.BlockSpec((None, D_MODEL, D_MODEL), lambda l, j: (l, 0, j)),
            pl.BlockSpec((None, 1, D_MODEL), lambda l, j: (l, 0, j)),
        ],
        out_specs=pl.BlockSpec((None, None, N_COND, D_MODEL), lambda l, j: (l, j, 0, 0)),
        out_shape=jax.ShapeDtypeStruct((DEPTH, N_MOD, N_COND, D_MODEL), F32),
        compiler_params=_params("arbitrary", "arbitrary"),
        name="modulation",
    )(cond, w_mod, b_mod.reshape(DEPTH, 1, N_MOD * D_MODEL))
    return out.transpose(0, 2, 1, 3)


def _head_mean_sq(a, bd):
    return _dot_exact(a * a, bd) * (1.0 / HEAD_DIM)


def _rope(x, cos, sin_signed):
    width = x.shape[-1]
    lane = lax.broadcasted_iota(jnp.int32, x.shape, 1)
    first = (lane & (AXIS_DIM - 1)) < (AXIS_DIM // 2)
    rot = jnp.where(first, pltpu.roll(x, width - AXIS_DIM // 2, 1), pltpu.roll(x, AXIS_DIM // 2, 1))
    return x * cos + rot * sin_signed


def _inproj_kernel(x_ref, mod_ref, n1_ref, w_ref, cos_ref, sin_ref, qn_ref, kn_ref, bd_ref,
                   q_ref, k_ref, v_ref, h_ref, g_ref):
    x = x_ref[...]
    xn = x * lax.rsqrt(jnp.mean(x * x, axis=-1, keepdims=True) + EPS) * n1_ref[...]
    xb = (xn * (1.0 + mod_ref[1:2, :]) + mod_ref[0:1, :]).astype(BF16)

    def proj(c0, width):
        return _dot(xb, w_ref[:, c0:c0 + width])

    a = proj(C_Q, ATTN_WIDTH)
    qn = a * lax.rsqrt(_head_mean_sq(a, bd_ref[...]) + EPS) * qn_ref[...]
    q_ref[...] = (_rope(qn, cos_ref[...], sin_ref[...]) * (HEAD_DIM ** -0.5)).astype(BF16)

    a = proj(C_K, KV_WIDTH)
    kn = a * lax.rsqrt(_head_mean_sq(a, bd_ref[:KV_WIDTH, :KV_WIDTH]) + EPS) * kn_ref[...]
    k_ref[...] = _rope(kn, cos_ref[:, :KV_WIDTH], sin_ref[:, :KV_WIDTH])
    v_ref[...] = proj(C_V, KV_WIDTH)

    a = proj(C_HQ, HGRN_WIDTH)
    h_ref[:, 0:HGRN_WIDTH] = a * _sigmoid(a) * (HGRN_K ** -0.5)
    h_ref[:, HGRN_WIDTH:2 * HGRN_WIDTH] = proj(C_FF, HGRN_WIDTH)
    h_ref[:, 2 * HGRN_WIDTH:3 * HGRN_WIDTH] = proj(C_FB, HGRN_WIDTH)
    h_ref[:, 3 * HGRN_WIDTH:4 * HGRN_WIDTH] = proj(C_HI, HGRN_WIDTH)
    a = proj(C_HG, HGRN_WIDTH)
    h_ref[:, 4 * HGRN_WIDTH:5 * HGRN_WIDTH] = a * _sigmoid(a)

    g_ref[:, 0:D_MODEL] = _sigmoid(proj(C_GA, D_MODEL))
    g_ref[:, D_MODEL:2 * D_MODEL] = _sigmoid(proj(C_GH, D_MODEL))


_CTX_TILES = N_CTX // TOK_TILE
_LAT_TILES_PER_SEQ = DEC_SEQ // TOK_TILE


def _tile_cond(i):
    return jnp.where(i < _CTX_TILES, 0, 1 + (i - _CTX_TILES) // _LAT_TILES_PER_SEQ)


def _tile_rope_block(i):
    return jnp.where(i < _CTX_TILES, _LAT_TILES_PER_SEQ, (i - _CTX_TILES) % _LAT_TILES_PER_SEQ)


def _input_projection(l, x, mod, norm1_w, w_in, cos_t, sin_t, qn_w, kn_w, bd):
    row = lambda w: pl.BlockSpec((TOK_TILE, w), lambda i: (i, 0))
    return pl.pallas_call(
        _inproj_kernel,
        grid=(N_TOK // TOK_TILE,),
        in_specs=[
            row(D_MODEL),
            pl.BlockSpec((None, None, N_MOD, D_MODEL), lambda i: (l, _tile_cond(i), 0, 0)),
            pl.BlockSpec((None, 1, D_MODEL), lambda i: (l, 0, 0)),
            pl.BlockSpec((None, D_MODEL, IN_COLS), lambda i: (l, 0, 0),
                         pipeline_mode=pl.Buffered(1)),
            pl.BlockSpec((TOK_TILE, ATTN_WIDTH), lambda i: (_tile_rope_block(i), 0)),
            pl.BlockSpec((TOK_TILE, ATTN_WIDTH), lambda i: (_tile_rope_block(i), 0)),
            pl.BlockSpec((None, 1, ATTN_WIDTH), lambda i: (l, 0, 0)),
            pl.BlockSpec((None, 1, KV_WIDTH), lambda i: (l, 0, 0)),
            pl.BlockSpec((ATTN_WIDTH, ATTN_WIDTH), lambda i: (0, 0)),
        ],
        out_specs=[row(ATTN_WIDTH), row(KV_WIDTH), row(KV_WIDTH), row(5 * HGRN_WIDTH),
                   row(2 * D_MODEL)],
        out_shape=[
            jax.ShapeDtypeStruct((N_TOK, ATTN_WIDTH), BF16),
            jax.ShapeDtypeStruct((N_TOK, KV_WIDTH), F32),
            jax.ShapeDtypeStruct((N_TOK, KV_WIDTH), F32),
            jax.ShapeDtypeStruct((N_TOK, 5 * HGRN_WIDTH), F32),
            jax.ShapeDtypeStruct((N_TOK, 2 * D_MODEL), F32),
        ],
        compiler_params=_params("arbitrary"),
        name="input_projection",
    )(x, mod, norm1_w, w_in, cos_t, sin_t, qn_w, kn_w, bd)


def _softmax_pv(scores, values):
    m = scores[0].max(axis=-1, keepdims=True)
    for s in scores[1:]:
        m = jnp.maximum(m, s.max(axis=-1, keepdims=True))
    num, den = None, None
    for s, v in zip(scores, values):
        p = jnp.exp(s - m)
        d = p.sum(axis=-1, keepdims=True)
        o = _dot(p.astype(BF16), v)
        num = o if num is None else num + o
        den = d if den is None else den + d
    return num / den


def _attn_ctx_kernel(q_ref, k_ref, v_ref, o_ref):
    for g in range(N_KV_HEADS):
        cols = slice(g * HEAD_DIM, (g + 1) * HEAD_DIM)
        kg = k_ref[:, cols].astype(BF16)
        vg = v_ref[:, cols].astype(BF16)
        for hh in range(GQA_GROUP):
            h = g * GQA_GROUP + hh
            hc = slice(h * HEAD_DIM, (h + 1) * HEAD_DIM)
            o = _softmax_pv([_dot_nt(q_ref[:, hc], kg)], [vg])
            o_ref[:, hc] = o.astype(BF16)


def _attention_ctx(q, k, v):
    return pl.pallas_call(
        _attn_ctx_kernel,
        grid=(BATCH,),
        in_specs=[
            pl.BlockSpec((SEQ, ATTN_WIDTH), lambda b: (b, 0)),
            pl.BlockSpec((SEQ, KV_WIDTH), lambda b: (b, 0)),
            pl.BlockSpec((SEQ, KV_WIDTH), lambda b: (b, 0)),
        ],
        out_specs=pl.BlockSpec((SEQ, ATTN_WIDTH), lambda b: (b, 0)),
        out_shape=jax.ShapeDtypeStruct((N_CTX, ATTN_WIDTH), BF16),
        compiler_params=_params("arbitrary"),
        name="attention_ctx",
    )(q, k, v)


def _attn_lat_kernel(q_ref, k_ref, v_ref, ck_ref, cv_ref, o_ref):
    for g in range(N_KV_HEADS):
        cols = slice(g * HEAD_DIM, (g + 1) * HEAD_DIM)
        kg = k_ref[:, cols].astype(BF16)
        vg = v_ref[:, cols].astype(BF16)
        ckg = ck_ref[g].astype(BF16)
        cvg = cv_ref[g].astype(BF16)
        for hh in range(GQA_GROUP):
            h = g * GQA_GROUP + hh
            hc = slice(h * HEAD_DIM, (h + 1) * HEAD_DIM)
            qh = q_ref[:, hc]
            o = _softmax_pv([_dot_nt(qh, kg), _dot_nt(qh, ckg)], [vg, cvg])
            o_ref[:, hc] = o.astype(BF16)


def _attention_lat(l, q, k, v, cache_k, cache_v):
    q_blocks = DEC_SEQ // Q_TILE
    q_off = N_CTX // Q_TILE
    kv_off = N_CTX // DEC_SEQ
    cache_spec = pl.BlockSpec((None, None, N_KV_HEADS, PAST_LEN, HEAD_DIM),
                              lambda b, i: (b, l, 0, 0, 0))
    return pl.pallas_call(
        _attn_lat_kernel,
        grid=(DEC_BATCH, q_blocks),
        in_specs=[
            pl.BlockSpec((Q_TILE, ATTN_WIDTH), lambda b, i: (q_off + b * q_blocks + i, 0)),
            pl.BlockSpec((DEC_SEQ, KV_WIDTH), lambda b, i: (kv_off + b, 0)),
            pl.BlockSpec((DEC_SEQ, KV_WIDTH), lambda b, i: (kv_off + b, 0)),
            cache_spec,
            cache_spec,
        ],
        out_specs=pl.BlockSpec((Q_TILE, ATTN_WIDTH), lambda b, i: (b * q_blocks + i, 0)),
        out_shape=jax.ShapeDtypeStruct((N_LAT, ATTN_WIDTH), BF16),
        compiler_params=_params("arbitrary", "arbitrary"),
        name="attention_lat",
    )(q, k, v, cache_k, cache_v)


def _lower_bound(lb_ref, l):
    z = lb_ref[...]
    e = jnp.exp(z - z.max(axis=0, keepdims=True))
    p = e / e.sum(axis=0, keepdims=True)
    acc = p[0:1] * 0.0
    for r in range(1, l + 1):
        acc = acc + p[r:r + 1]
    return acc


def _hgrn_chunk(q, x, v, lb, st_ref, rev):
    f = lb + (1.0 - lb) * _sigmoid(x)
    kk = 1.0 - f
    g = jnp.log(f)
    row = lax.broadcasted_iota(jnp.int32, (CHUNK, CHUNK), 0)
    col = lax.broadcasted_iota(jnp.int32, (CHUNK, CHUNK), 1)
    tri = jnp.where((col >= row) if rev else (col <= row), 1.0, 0.0).astype(F32)
    b = _dot_exact(tri, g)
    edge = 0 if rev else CHUNK - 1
    b_end = b[edge:edge + 1, :]

    st = st_ref[...]
    vb = v.astype(BF16)
    o_inter = _dot_nt((q * jnp.exp(b)).astype(BF16), st.astype(BF16))
    k_dec = kk * jnp.exp(b_end - b)
    st_ref[...] = st * jnp.exp(b_end) + _dot_tn(vb, k_dec.astype(BF16))

    n_sub = CHUNK // SUB
    r16 = lax.broadcasted_iota(jnp.int32, (SUB, SUB), 0)
    c16 = lax.broadcasted_iota(jnp.int32, (SUB, SUB), 1)
    outs = []
    for i in range(n_sub):
        lo = i * SUB
        q_i = q[lo:lo + SUB]
        b_i = b[lo:lo + SUB]
        a_d = jnp.zeros((SUB, SUB), F32)
        for j in range(SUB):
            s = lo + j
            d = q_i * kk[s:s + 1] * jnp.exp(jnp.minimum(b_i - b[s:s + 1], 0.0))
            colsum = d.sum(axis=-1, keepdims=True)
            keep = (c16 == j) & ((r16 <= j) if rev else (r16 >= j))
            a_d = jnp.where(keep, colsum, a_d)
        o_i = _dot(a_d.astype(BF16), vb[lo:lo + SUB])
        if rev and i < n_sub - 1:
            oth = slice(lo + SUB, CHUNK)
            ref_row = b[lo + SUB:lo + SUB + 1]
        elif (not rev) and i > 0:
            oth = slice(0, lo)
            ref_row = b[lo - 1:lo]
        else:
            oth = None
        if oth is not None:
            qs = q_i * jnp.exp(b_i - ref_row)
            ks = kk[oth] * jnp.exp(ref_row - b[oth])
            a_o = _dot_nt(qs.astype(BF16), ks.astype(BF16))
            o_i = o_i + _dot(a_o.astype(BF16), vb[oth])
        outs.append(o_i)
    return o_inter + jnp.concatenate(outs, axis=0)


def _hgrn_kernel(l, seq_len, has_init, *refs):
    hq_ref, ff_ref, fb_ref, hi_ref, hg_ref, lbf_ref, lbb_ref, hn_ref = refs[:8]
    refs = refs[8:]
    if has_init:
        s0f_ref, s0b_ref, rec_ref, of_ref, ob_ref, stf_ref, stb_ref = refs
    else:
        rec_ref, sf_ref, sb_ref, of_ref, ob_ref, stf_ref, stb_ref = refs
    lb_f = _lower_bound(lbf_ref, l)
    lb_b = _lower_bound(lbb_ref, l)
    if has_init:
        stf_ref[...] = s0f_ref[...].T
        stb_ref[...] = s0b_ref[...].T
    else:
        stf_ref[...] = jnp.zeros((HGRN_V, HGRN_K), F32)
        stb_ref[...] = jnp.zeros((HGRN_V, HGRN_K), F32)
    n_chunks = seq_len // CHUNK

    def body(c, carry):
        rf = pl.ds(pl.multiple_of(c * CHUNK, CHUNK), CHUNK)
        of_ref[rf, :] = _hgrn_chunk(hq_ref[rf, :], ff_ref[rf, :], hi_ref[rf, :], lb_f, stf_ref, False)
        rb = pl.ds(pl.multiple_of((n_chunks - 1 - c) * CHUNK, CHUNK), CHUNK)
        ob_ref[rb, :] = _hgrn_chunk(hq_ref[rb, :], fb_ref[rb, :], hi_ref[rb, :], lb_b, stb_ref, True)
        return carry

    lax.fori_loop(0, n_chunks, body, 0)
    o = of_ref[...] + ob_ref[...]
    o = o * lax.rsqrt(jnp.mean(o * o, axis=-1, keepdims=True) + EPS) * hn_ref[...]
    rec_ref[...] = o * hg_ref[...]
    if not has_init:
        sf_ref[...] = stf_ref[...].T
        sb_ref[...] = stb_ref[...].T


def _hgrn(l, h5, row_off, n_seq, seq_len, lb_fwd, lb_bwd, hnorm_w, init=None):
    blocks_off = row_off // seq_len
    part = lambda k: pl.BlockSpec((seq_len, HGRN_K),
                                  lambda b, h: (blocks_off + b, k * HGRN_HEADS + h))
    lb_spec = pl.BlockSpec((DEPTH, HGRN_K), lambda b, h: (0, h))
    in_specs = [part(0), part(1), part(2), part(3), part(4), lb_spec, lb_spec,
                pl.BlockSpec((None, 1, HGRN_V), lambda b, h: (l, 0, 0))]
    args = [h5, h5, h5, h5, h5, lb_fwd, lb_bwd, hnorm_w]
    rec_spec = pl.BlockSpec((seq_len, HGRN_V), lambda b, h: (b, h))
    rec_shape = jax.ShapeDtypeStruct((n_seq * seq_len, HGRN_WIDTH), F32)
    if init is not None:
        st_spec = pl.BlockSpec((None, None, None, HGRN_K, HGRN_V), lambda b, h: (b, l, h, 0, 0))
        in_specs += [st_spec, st_spec]
        args += list(init)
        out_specs, out_shape = rec_spec, rec_shape
    else:
        st_spec = pl.BlockSpec((None, None, HGRN_K, HGRN_V), lambda b, h: (b, h, 0, 0))
        st_shape = jax.ShapeDtypeStruct((n_seq, HGRN_HEADS, HGRN_K, HGRN_V), F32)
        out_specs, out_shape = [rec_spec, st_spec, st_spec], [rec_shape, st_shape, st_shape]
    return pl.pallas_call(
        functools.partial(_hgrn_kernel, l, seq_len, init is not None),
        grid=(n_seq, HGRN_HEADS),
        in_specs=in_specs,
        out_specs=out_specs,
        out_shape=out_shape,
        scratch_shapes=[pltpu.VMEM((seq_len, HGRN_V), F32), pltpu.VMEM((seq_len, HGRN_V), F32),
                        pltpu.VMEM((HGRN_V, HGRN_K), F32), pltpu.VMEM((HGRN_V, HGRN_K), F32)],
        compiler_params=_params("arbitrary", "arbitrary"),
        name="hgrn_lat" if init is not None else "hgrn_ctx",
    )(*args)


def _route(scores, bias, sel_member, sel_group):
    sel = scores + bias
    members = [_dot_exact(sel, sel_member[k]) for k in range(EXPERTS_PER_GROUP)]
    pair = None
    for i in range(EXPERTS_PER_GROUP):
        for j in range(i + 1, EXPERTS_PER_GROUP):
            s = members[i] + members[j]
            pair = s if pair is None else jnp.maximum(pair, s)
    lane = lax.broadcasted_iota(jnp.int32, sel.shape, 1)
    best = _dot_exact(pair, sel_group[0])
    best_g = jnp.zeros(sel.shape, jnp.int32)
    for m in range(1, N_GROUPS):
        cand = _dot_exact(pair, sel_group[m])
        upd = cand > best
        best_g = jnp.where(upd, m, best_g)
        best = jnp.where(upd, cand, best)
    in_group = (lane >> 2) == best_g
    pos = lane & (EXPERTS_PER_GROUP - 1)
    rank = jnp.zeros(sel.shape, jnp.int32)
    for k in range(EXPERTS_PER_GROUP):
        ahead = (members[k] > sel) | ((members[k] == sel) & (k < pos))
        rank = rank + jnp.where(ahead, 1, 0)
    w = jnp.where(in_group & (rank < 2), scores, 0.0)
    return w / w.sum(axis=-1, keepdims=True)


def _merge_kernel(attn_ref, rec_ref, g_ref, x_ref, mod_ref, n2_ref, wa_ref, wh_ref, wo_ref,
                  wr_ref, br_ref, selm_ref, selg_ref, x1_ref, h2_ref, gate_ref):
    ya = _dot(attn_ref[...], wa_ref[...])
    yh = _dot(rec_ref[...].astype(BF16), wh_ref[...])
    merged = g_ref[:, 0:D_MODEL] * ya + g_ref[:, D_MODEL:2 * D_MODEL] * yh
    out = _dot(merged.astype(BF16), wo_ref[...])
    x1 = x_ref[...] + mod_ref[2:3, :] * out
    x1_ref[...] = x1
    xn = x1 * lax.rsqrt(jnp.mean(x1 * x1, axis=-1, keepdims=True) + EPS) * n2_ref[...]
    h2 = (xn * (1.0 + mod_ref[4:5, :]) + mod_ref[3:4, :]).astype(BF16)
    h2_ref[...] = h2
    scores = _sigmoid(_dot(h2, wr_ref[...]))
    gate_ref[...] = _route(scores, br_ref[...], [selm_ref[k] for k in range(EXPERTS_PER_GROUP)],
                           [selg_ref[m] for m in range(N_GROUPS)])


def _merge(l, attn, rec, gates, x, mod, norm2_w, w_br_attn, w_br_hgrn, w_out, w_router, b_router,
           sel_member, sel_group):
    row = lambda w: pl.BlockSpec((TOK_TILE, w), lambda i: (i, 0))
    layer = lambda r, c: pl.BlockSpec((None, r, c), lambda i: (l, 0, 0))
    full = lambda *s: pl.BlockSpec(s, lambda i: (0,) * len(s))
    return pl.pallas_call(
        _merge_kernel,
        grid=(N_TOK // TOK_TILE,),
        in_specs=[
            row(ATTN_WIDTH), row(HGRN_WIDTH), row(2 * D_MODEL), row(D_MODEL),
            pl.BlockSpec((None, None, N_MOD, D_MODEL), lambda i: (l, _tile_cond(i), 0, 0)),
            layer(1, D_MODEL),
            layer(ATTN_WIDTH, D_MODEL), layer(HGRN_WIDTH, D_MODEL), layer(D_MODEL, D_MODEL),
            full(D_MODEL, N_EXPERTS), full(1, N_EXPERTS),
            full(EXPERTS_PER_GROUP, N_EXPERTS, N_EXPERTS), full(N_GROUPS, N_EXPERTS, N_EXPERTS),
        ],
        out_specs=[row(D_MODEL), row(D_MODEL), row(N_EXPERTS)],
        out_shape=[
            jax.ShapeDtypeStruct((N_TOK, D_MODEL), F32),
            jax.ShapeDtypeStruct((N_TOK, D_MODEL), BF16),
            jax.ShapeDtypeStruct((N_TOK, N_EXPERTS), F32),
        ],
        compiler_params=_params("arbitrary"),
        name="merge_router",
    )(attn, rec, gates, x, mod, norm2_w, w_br_attn, w_br_hgrn, w_out, w_router, b_router,
      sel_member, sel_group)


def _moe_kernel(h_ref, gate_ref, x1_ref, mod_ref, wg_ref, wu_ref, wd_ref, o_ref, acc_ref):
    e = pl.program_id(1)

    @pl.when(e == 0)
    def _():
        acc_ref[...] = jnp.zeros_like(acc_ref)

    h = h_ref[...]
    a = _dot(h, wg_ref[...].astype(BF16))
    u = _dot(h, wu_ref[...].astype(BF16))
    pick = (lax.broadcasted_iota(jnp.int32, (N_EXPERTS, D_FF), 0) == e).astype(F32)
    ge = _dot_exact(gate_ref[...], pick)
    act = a * _sigmoid(a) * u * ge
    acc_ref[...] += _dot(act.astype(BF16), wd_ref[...].astype(BF16))

    @pl.when(e == N_EXPERTS - 1)
    def _():
        o_ref[...] = x1_ref[...] + mod_ref[5:6, :] * acc_ref[...]


_MOE_CTX_TILES = N_CTX // MOE_TILE
_MOE_LAT_TILES_PER_SEQ = DEC_SEQ // MOE_TILE


def _moe_cond(i):
    return jnp.where(i < _MOE_CTX_TILES, 0, 1 + (i - _MOE_CTX_TILES) // _MOE_LAT_TILES_PER_SEQ)


def _moe(l, h2, gates, x1, mod, w_gate, w_up, w_down):
    row = lambda w: pl.BlockSpec((MOE_TILE, w), lambda i, e: (i, 0))
    return pl.pallas_call(
        _moe_kernel,
        grid=(N_TOK // MOE_TILE, N_EXPERTS),
        in_specs=[
            row(D_MODEL), row(N_EXPERTS), row(D_MODEL),
            pl.BlockSpec((None, None, N_MOD, D_MODEL), lambda i, e: (l, _moe_cond(i), 0, 0)),
            pl.BlockSpec((None, None, D_MODEL, D_FF), lambda i, e: (l, e, 0, 0)),
            pl.BlockSpec((None, None, D_MODEL, D_FF), lambda i, e: (l, e, 0, 0)),
            pl.BlockSpec((None, None, D_FF, D_MODEL), lambda i, e: (l, e, 0, 0)),
        ],
        out_specs=row(D_MODEL),
        out_shape=jax.ShapeDtypeStruct((N_TOK, D_MODEL), F32),
        scratch_shapes=[pltpu.VMEM((MOE_TILE, D_MODEL), F32)],
        compiler_params=_params("arbitrary", "arbitrary"),
        name="experts",
    )(h2, gates, x1, mod, w_gate, w_up, w_down)


def _rope_tables():
    pos = np.arange(DEC_SEQ)
    inv_freq = ROPE_THETA ** (-np.arange(0, AXIS_DIM, 2, dtype=np.float32) / AXIS_DIM)
    ang_r = (pos // GRID_W).astype(np.float32)[:, None] * inv_freq[None, :]
    ang_c = (pos % GRID_W).astype(np.float32)[:, None] * inv_freq[None, :]
    ang = jnp.asarray(np.concatenate([ang_r, ang_r, ang_c, ang_c], axis=-1).astype(np.float32))
    sign = np.where(np.arange(HEAD_DIM) % AXIS_DIM < AXIS_DIM // 2, -1.0, 1.0).astype(np.float32)
    cos = jnp.concatenate([jnp.cos(ang), jnp.ones((TOK_TILE, HEAD_DIM), F32)], axis=0)
    sin = jnp.concatenate([jnp.sin(ang) * sign, jnp.zeros((TOK_TILE, HEAD_DIM), F32)], axis=0)
    return jnp.tile(cos, (1, N_Q_HEADS)), jnp.tile(sin, (1, N_Q_HEADS))


def _selectors():
    lane = np.arange(N_EXPERTS)
    member = np.stack([(lane[:, None] == (lane[None, :] // EXPERTS_PER_GROUP) * EXPERTS_PER_GROUP + k)
                       for k in range(EXPERTS_PER_GROUP)]).astype(np.float32)
    group = np.stack([np.broadcast_to(lane[:, None] == m * EXPERTS_PER_GROUP, (N_EXPERTS, N_EXPERTS))
                      for m in range(N_GROUPS)]).astype(np.float32)
    head = (np.arange(ATTN_WIDTH)[:, None] // HEAD_DIM == np.arange(ATTN_WIDTH)[None, :] // HEAD_DIM)
    return jnp.asarray(member), jnp.asarray(group), jnp.asarray(head.astype(np.float32))


def kernel(x_prompt, x_sample, cache_k, cache_v, state_fwd, state_bwd, c, c_ctx, norm1_w, norm2_w, w_mod, b_mod, w_in, q_norm_w, k_norm_w, hgrn_lb_fwd, hgrn_lb_bwd, hgrn_norm_w, w_br_attn, w_br_hgrn, w_out, w_router, b_router, w_exp_gate, w_exp_up, w_exp_down):
    cos_t, sin_t = _rope_tables()
    sel_member, sel_group, head_ones = _selectors()
    cond = jnp.concatenate([c_ctx[None, :], c, jnp.zeros((N_COND - 1 - DEC_BATCH, D_MODEL), F32)], axis=0)
    mod = _modulation(cond, w_mod, b_mod)

    w_in_b = w_in.astype(BF16)
    w_ba_b = w_br_attn.astype(BF16)
    w_bh_b = w_br_hgrn.astype(BF16)
    w_out_b = w_out.astype(BF16)
    w_router_b = w_router.astype(BF16)
    b_router_r = b_router.reshape(1, N_EXPERTS)
    norm1_r = norm1_w.reshape(DEPTH, 1, D_MODEL)
    norm2_r = norm2_w.reshape(DEPTH, 1, D_MODEL)
    hnorm_r = hgrn_norm_w.reshape(DEPTH, 1, HGRN_V)
    qn_r = jnp.tile(q_norm_w, (1, N_Q_HEADS)).reshape(DEPTH, 1, ATTN_WIDTH)
    kn_r = jnp.tile(k_norm_w, (1, N_KV_HEADS)).reshape(DEPTH, 1, KV_WIDTH)

    x = jnp.concatenate([x_prompt.reshape(N_CTX, D_MODEL), x_sample.reshape(N_LAT, D_MODEL)], axis=0)
    ks_out, vs_out, sf_out, sb_out = [], [], [], []
    for l in range(DEPTH):
        q, k, v, h5, gates = _input_projection(l, x, mod, norm1_r, w_in_b, cos_t, sin_t, qn_r, kn_r,
                                               head_ones)
        attn_c = _attention_ctx(q, k, v)
        attn_s = _attention_lat(l, q, k, v, cache_k, cache_v)
        rec_c, sf, sb = _hgrn(l, h5, 0, BATCH, SEQ, hgrn_lb_fwd, hgrn_lb_bwd, hnorm_r)
        rec_s = _hgrn(l, h5, N_CTX, DEC_BATCH, DEC_SEQ, hgrn_lb_fwd, hgrn_lb_bwd, hnorm_r,
                      init=(state_fwd, state_bwd))
        attn = jnp.concatenate([attn_c, attn_s], axis=0)
        rec = jnp.concatenate([rec_c, rec_s], axis=0)
        x1, h2, route = _merge(l, attn, rec, gates, x, mod, norm2_r, w_ba_b, w_bh_b, w_out_b,
                               w_router_b, b_router_r, sel_member, sel_group)
        x = _moe(l, h2, route, x1, mod, w_exp_gate, w_exp_up, w_exp_down)
        ks_out.append(k[:N_CTX].reshape(BATCH, SEQ, N_KV_HEADS, HEAD_DIM).transpose(0, 2, 1, 3))
        vs_out.append(v[:N_CTX].reshape(BATCH, SEQ, N_KV_HEADS, HEAD_DIM).transpose(0, 2, 1, 3))
        sf_out.append(sf)
        sb_out.append(sb)

    y_prompt = x[:N_CTX].reshape(BATCH, SEQ, D_MODEL)
    y_sample = x[N_CTX:].reshape(DEC_BATCH, DEC_SEQ, D_MODEL)
    return (y_prompt, y_sample, jnp.stack(ks_out, axis=1), jnp.stack(vs_out, axis=1),
            jnp.stack(sf_out, axis=1), jnp.stack(sb_out, axis=1))
```

```python
import functools

import numpy as np
import jax
import jax.numpy as jnp
from jax import lax
from jax.experimental import pallas as pl
from jax.experimental.pallas import tpu as pltpu

F32 = jnp.float32
BF16 = jnp.bfloat16

D_MODEL = 1024
BATCH = 16
SEQ = 256
DEPTH = 4
DEC_BATCH = 4
DEC_SEQ = 2048
PAST_LEN = 512
GRID_W = 64
N_Q_HEADS = 8
N_KV_HEADS = 2
GQA_GROUP = N_Q_HEADS // N_KV_HEADS
HEAD_DIM = 64
AXIS_DIM = HEAD_DIM // 2
ATTN_WIDTH = N_Q_HEADS * HEAD_DIM
KV_WIDTH = N_KV_HEADS * HEAD_DIM
ROPE_THETA = 10000.0
HGRN_HEADS = 4
HGRN_K = 128
HGRN_V = 128
HGRN_WIDTH = HGRN_HEADS * HGRN_K
N_EXPERTS = 16
N_GROUPS = 4
EXPERTS_PER_GROUP = N_EXPERTS // N_GROUPS
D_FF = 512
EPS = 1e-6
IN_COLS = ATTN_WIDTH + 2 * KV_WIDTH + 5 * HGRN_WIDTH + 2 * D_MODEL

N_CTX = BATCH * SEQ
N_LAT = DEC_BATCH * DEC_SEQ
N_TOK = N_CTX + N_LAT
N_COND = 8
N_MOD = 6

C_Q = 0
C_K = C_Q + ATTN_WIDTH
C_V = C_K + KV_WIDTH
C_HQ = C_V + KV_WIDTH
C_FF = C_HQ + HGRN_WIDTH
C_FB = C_FF + HGRN_WIDTH
C_HI = C_FB + HGRN_WIDTH
C_HG = C_HI + HGRN_WIDTH
C_GA = C_HG + HGRN_WIDTH
C_GH = C_GA + D_MODEL

TOK_TILE = 512
MOE_TILE = 1024
Q_TILE = 512
CHUNK = 64
HGRN_LEVELS = (32, 16, 8, 4, 2, 1)
BLOCK_CHUNKS = 4
LOG2E = 1.4426950408889634
VMEM_LIMIT = 56 * 1024 * 1024

_HI = lax.Precision.HIGHEST


def _sigmoid(x):
    return 1.0 / (1.0 + jnp.exp(-x))


def _dot(a, b):
    return jnp.dot(a, b, preferred_element_type=F32)


def _dot_nt(a, b, precision=None):
    return lax.dot_general(a, b, (((1,), (1,)), ((), ())), precision=precision,
                           preferred_element_type=F32)


def _dot_tn(a, b, precision=None):
    return lax.dot_general(a, b, (((0,), (0,)), ((), ())), precision=precision,
                           preferred_element_type=F32)


def _dot_exact(a, b):
    return jnp.dot(a, b, precision=_HI, preferred_element_type=F32)


def _params(*sem):
    return pltpu.CompilerParams(dimension_semantics=sem, vmem_limit_bytes=VMEM_LIMIT)


def _mod_kernel(cond_ref, w_ref, b_ref, out_ref):
    c = cond_ref[...]
    sc = c * _sigmoid(c)
    out_ref[...] = _dot(sc.astype(BF16), w_ref[...].astype(BF16)) + b_ref[...]


def _modulation(cond, w_mod, b_mod):
    out = pl.pallas_call(
        _mod_kernel,
        grid=(DEPTH, N_MOD),
        in_specs=[
            pl.BlockSpec((N_COND, D_MODEL), lambda l, j: (0, 0)),
            pl.BlockSpec((None, D_MODEL, D_MODEL), lambda l, j: (l, 0, j)),
            pl.BlockSpec((None, 1, D_MODEL), lambda l, j: (l, 0, j)),
        ],
        out_specs=pl.BlockSpec((None, None, N_COND, D_MODEL), lambda l, j: (l, j, 0, 0)),
        out_shape=jax.ShapeDtypeStruct((DEPTH, N_MOD, N_COND, D_MODEL), F32),
        compiler_params=_params("arbitrary", "arbitrary"),
        name="modulation",
    )(cond, w_mod, b_mod.reshape(DEPTH, 1, N_MOD * D_MODEL))
    return out.transpose(0, 2, 1, 3)


def _head_mean_sq(a, bd):
    return _dot_exact(a * a, bd) * (1.0 / HEAD_DIM)


def _rope(x, cos, sin_signed):
    width = x.shape[-1]
    lane = lax.broadcasted_iota(jnp.int32, x.shape, 1)
    first = (lane & (AXIS_DIM - 1)) < (AXIS_DIM // 2)
    rot = jnp.where(first, pltpu.roll(x, width - AXIS_DIM // 2, 1), pltpu.roll(x, AXIS_DIM // 2, 1))
    return x * cos + rot * sin_signed


def _inproj_kernel(x_ref, mod_ref, n1_ref, w_ref, cos_ref, sin_ref, qn_ref, kn_ref, bd_ref,
                   q_ref, k_ref, v_ref, h_ref, g_ref):
    x = x_ref[...]
    xn = x * lax.rsqrt(jnp.mean(x * x, axis=-1, keepdims=True) + EPS) * n1_ref[...]
    xb = (xn * (1.0 + mod_ref[1:2, :]) + mod_ref[0:1, :]).astype(BF16)

    def proj(c0, width):
        return _dot(xb, w_ref[:, c0:c0 + width])

    a = proj(C_Q, ATTN_WIDTH)
    qn = a * lax.rsqrt(_head_mean_sq(a, bd_ref[...]) + EPS) * qn_ref[...]
    q_ref[...] = (_rope(qn, cos_ref[...], sin_ref[...]) * (HEAD_DIM ** -0.5)).astype(BF16)

    a = proj(C_K, KV_WIDTH)
    kn = a * lax.rsqrt(_head_mean_sq(a, bd_ref[:KV_WIDTH, :KV_WIDTH]) + EPS) * kn_ref[...]
    k_ref[...] = _rope(kn, cos_ref[:, :KV_WIDTH], sin_ref[:, :KV_WIDTH])
    v_ref[...] = proj(C_V, KV_WIDTH)

    a = proj(C_HQ, HGRN_WIDTH)
    h_ref[:, 0:HGRN_WIDTH] = a * _sigmoid(a) * (HGRN_K ** -0.5)
    h_ref[:, HGRN_WIDTH:2 * HGRN_WIDTH] = proj(C_FF, HGRN_WIDTH)
    h_ref[:, 2 * HGRN_WIDTH:3 * HGRN_WIDTH] = proj(C_FB, HGRN_WIDTH)
    h_ref[:, 3 * HGRN_WIDTH:4 * HGRN_WIDTH] = proj(C_HI, HGRN_WIDTH)
    a = proj(C_HG, HGRN_WIDTH)
    h_ref[:, 4 * HGRN_WIDTH:5 * HGRN_WIDTH] = a * _sigmoid(a)

    g_ref[:, 0:D_MODEL] = _sigmoid(proj(C_GA, D_MODEL))
    g_ref[:, D_MODEL:2 * D_MODEL] = _sigmoid(proj(C_GH, D_MODEL))


_CTX_TILES = N_CTX // TOK_TILE
_LAT_TILES_PER_SEQ = DEC_SEQ // TOK_TILE


def _tile_cond(i):
    return jnp.where(i < _CTX_TILES, 0, 1 + (i - _CTX_TILES) // _LAT_TILES_PER_SEQ)


def _tile_rope_block(i):
    return jnp.where(i < _CTX_TILES, _LAT_TILES_PER_SEQ, (i - _CTX_TILES) % _LAT_TILES_PER_SEQ)


def _input_projection(l, x, mod, norm1_w, w_in, cos_t, sin_t, qn_w, kn_w, bd):
    row = lambda w: pl.BlockSpec((TOK_TILE, w), lambda i: (i, 0))
    return pl.pallas_call(
        _inproj_kernel,
        grid=(N_TOK // TOK_TILE,),
        in_specs=[
            row(D_MODEL),
            pl.BlockSpec((None, None, N_MOD, D_MODEL), lambda i: (l, _tile_cond(i), 0, 0)),
            pl.BlockSpec((None, 1, D_MODEL), lambda i: (l, 0, 0)),
            pl.BlockSpec((None, D_MODEL, IN_COLS), lambda i: (l, 0, 0),
                         pipeline_mode=pl.Buffered(1)),
            pl.BlockSpec((TOK_TILE, ATTN_WIDTH), lambda i: (_tile_rope_block(i), 0)),
            pl.BlockSpec((TOK_TILE, ATTN_WIDTH), lambda i: (_tile_rope_block(i), 0)),
            pl.BlockSpec((None, 1, ATTN_WIDTH), lambda i: (l, 0, 0)),
            pl.BlockSpec((None, 1, KV_WIDTH), lambda i: (l, 0, 0)),
            pl.BlockSpec((ATTN_WIDTH, ATTN_WIDTH), lambda i: (0, 0)),
        ],
        out_specs=[row(ATTN_WIDTH), row(KV_WIDTH), row(KV_WIDTH), row(5 * HGRN_WIDTH),
                   row(2 * D_MODEL)],
        out_shape=[
            jax.ShapeDtypeStruct((N_TOK, ATTN_WIDTH), BF16),
            jax.ShapeDtypeStruct((N_TOK, KV_WIDTH), F32),
            jax.ShapeDtypeStruct((N_TOK, KV_WIDTH), F32),
            jax.ShapeDtypeStruct((N_TOK, 5 * HGRN_WIDTH), F32),
            jax.ShapeDtypeStruct((N_TOK, 2 * D_MODEL), F32),
        ],
        compiler_params=_params("arbitrary"),
        name="input_projection",
    )(x, mod, norm1_w, w_in, cos_t, sin_t, qn_w, kn_w, bd)


def _softmax_pv(scores, values):
    m = scores[0].max(axis=-1, keepdims=True)
    for s in scores[1:]:
        m = jnp.maximum(m, s.max(axis=-1, keepdims=True))
    num, den = None, None
    for s, v in zip(scores, values):
        p = jnp.exp(s - m)
        d = p.sum(axis=-1, keepdims=True)
        o = _dot(p.astype(BF16), v)
        num = o if num is None else num + o
        den = d if den is None else den + d
    return num / den


def _attn_ctx_kernel(q_ref, k_ref, v_ref, o_ref):
    for g in range(N_KV_HEADS):
        cols = slice(g * HEAD_DIM, (g + 1) * HEAD_DIM)
        kg = k_ref[:, cols].astype(BF16)
        vg = v_ref[:, cols].astype(BF16)
        for hh in range(GQA_GROUP):
            h = g * GQA_GROUP + hh
            hc = slice(h * HEAD_DIM, (h + 1) * HEAD_DIM)
            o = _softmax_pv([_dot_nt(q_ref[:, hc], kg)], [vg])
            o_ref[:, hc] = o.astype(BF16)


def _attention_ctx(q, k, v):
    return pl.pallas_call(
        _attn_ctx_kernel,
        grid=(BATCH,),
        in_specs=[
            pl.BlockSpec((SEQ, ATTN_WIDTH), lambda b: (b, 0)),
            pl.BlockSpec((SEQ, KV_WIDTH), lambda b: (b, 0)),
            pl.BlockSpec((SEQ, KV_WIDTH), lambda b: (b, 0)),
        ],
        out_specs=pl.BlockSpec((SEQ, ATTN_WIDTH), lambda b: (b, 0)),
        out_shape=jax.ShapeDtypeStruct((N_CTX, ATTN_WIDTH), BF16),
        compiler_params=_params("arbitrary"),
        name="attention_ctx",
    )(q, k, v)


def _attn_lat_kernel(q_ref, k_ref, v_ref, ck_ref, cv_ref, o_ref):
    for g in range(N_KV_HEADS):
        cols = slice(g * HEAD_DIM, (g + 1) * HEAD_DIM)
        kg = k_ref[:, cols].astype(BF16)
        vg = v_ref[:, cols].astype(BF16)
        ckg = ck_ref[g].astype(BF16)
        cvg = cv_ref[g].astype(BF16)
        for hh in range(GQA_GROUP):
            h = g * GQA_GROUP + hh
            hc = slice(h * HEAD_DIM, (h + 1) * HEAD_DIM)
            qh = q_ref[:, hc]
            o = _softmax_pv([_dot_nt(qh, kg), _dot_nt(qh, ckg)], [vg, cvg])
            o_ref[:, hc] = o.astype(BF16)


def _attention_lat(l, q, k, v, cache_k, cache_v):
    q_blocks = DEC_SEQ // Q_TILE
    q_off = N_CTX // Q_TILE
    kv_off = N_CTX // DEC_SEQ
    cache_spec = pl.BlockSpec((None, None, N_KV_HEADS, PAST_LEN, HEAD_DIM),
                              lambda b, i: (b, l, 0, 0, 0))
    return pl.pallas_call(
        _attn_lat_kernel,
        grid=(DEC_BATCH, q_blocks),
        in_specs=[
            pl.BlockSpec((Q_TILE, ATTN_WIDTH), lambda b, i: (q_off + b * q_blocks + i, 0)),
            pl.BlockSpec((DEC_SEQ, KV_WIDTH), lambda b, i: (kv_off + b, 0)),
            pl.BlockSpec((DEC_SEQ, KV_WIDTH), lambda b, i: (kv_off + b, 0)),
            cache_spec,
            cache_spec,
        ],
        out_specs=pl.BlockSpec((Q_TILE, ATTN_WIDTH), lambda b, i: (b * q_blocks + i, 0)),
        out_shape=jax.ShapeDtypeStruct((N_LAT, ATTN_WIDTH), BF16),
        compiler_params=_params("arbitrary", "arbitrary"),
        name="attention_lat",
    )(q, k, v, cache_k, cache_v)


def _lower_bound(lb_ref, l):
    z = lb_ref[...]
    e = jnp.exp(z - z.max(axis=0, keepdims=True))
    p = e / e.sum(axis=0, keepdims=True)
    acc = p[0:1] * 0.0
    for r in range(1, l + 1):
        acc = acc + p[r:r + 1]
    return acc


def _hgrn_tables(rev):
    t = np.arange(CHUNK)
    tt, uu = t[:, None], t[None, :]
    groups = []
    masks = []
    for h in HGRN_LEVELS:
        right = (t & h) != 0
        edge = ((t // (2 * h)) * 2 * h + h - 1)[:, None]
        groups.append(np.where(right[:, None], (uu > edge) & (uu <= tt), (uu > tt) & (uu <= edge)))
        same = (tt // (2 * h)) == (uu // (2 * h))
        masks.append(same & right[:, None] & ~right[None, :])
    masks.append(tt == uu)
    groups += [uu <= tt, uu > tt]
    sums = np.stack(groups).astype(np.float32)
    masks = np.stack(masks).astype(np.float32)
    if rev:
        sums = sums[:, ::-1, ::-1]
        masks = masks[:, ::-1, ::-1]
    sums = sums.reshape(-1, CHUNK)
    return (jnp.asarray(np.concatenate([sums, sums], axis=1), dtype=BF16),
            jnp.asarray(np.ascontiguousarray(masks)))


def _hgrn_block(q, x, v, lb, sums_ref, masks_ref, st_ref, rev):
    n = q.shape[0] // CHUNK
    f = lb + (1.0 - lb) * _sigmoid(x)
    kk = 1.0 - f
    g = jnp.log(f) * LOG2E
    g_hi = g.astype(BF16)
    g_lo = (g - g_hi.astype(F32)).astype(BF16)
    vb = v.astype(BF16)
    rid = lax.broadcasted_iota(jnp.int32, (CHUNK, HGRN_K), 0)
    edge = 0 if rev else CHUNK - 1
    n_lev = len(HGRN_LEVELS)
    outs = [None] * n
    for c in (reversed(range(n)) if rev else range(n)):
        rs = slice(c * CHUNK, (c + 1) * CHUNK)
        e = jnp.exp2(_dot(sums_ref[...], jnp.concatenate([g_hi[rs], g_lo[rs]], axis=0)))
        q_c, k_c, v_c = q[rs], kk[rs], vb[rs]
        a = _dot_nt(q_c.astype(BF16), k_c.astype(BF16)) * masks_ref[n_lev]
        for i, h in enumerate(HGRN_LEVELS):
            later = ((rid & h) == 0) if rev else ((rid & h) != 0)
            z = (jnp.where(later, q_c, k_c) * e[i * CHUNK:(i + 1) * CHUNK]).astype(BF16)
            a = a + _dot_nt(z, z) * masks_ref[i]
        e_in = e[n_lev * CHUNK:(n_lev + 1) * CHUNK]
        e_out = e[(n_lev + 1) * CHUNK:(n_lev + 2) * CHUNK]
        st = st_ref[...]
        outs[c] = (_dot(a.astype(BF16), v_c)
                   + _dot_nt((q_c * e_in).astype(BF16), st.astype(BF16)))
        k_dec = (k_c * e_out).astype(BF16)
        st_ref[...] = st * e_in[edge:edge + 1] + _dot_tn(v_c, k_dec)
    return jnp.concatenate(outs, axis=0)


def _hgrn_kernel(l, seq_len, has_init, *refs):
    (hq_ref, ff_ref, fb_ref, hi_ref, hg_ref, lbf_ref, lbb_ref, hn_ref,
     sumf_ref, sumb_ref, mskf_ref, mskb_ref) = refs[:12]
    refs = refs[12:]
    if has_init:
        s0f_ref, s0b_ref, rec_ref, of_ref, ob_ref, stf_ref, stb_ref = refs
    else:
        rec_ref, sf_ref, sb_ref, of_ref, ob_ref, stf_ref, stb_ref = refs
    lb_f = _lower_bound(lbf_ref, l)
    lb_b = _lower_bound(lbb_ref, l)
    if has_init:
        stf_ref[...] = s0f_ref[...].T
        stb_ref[...] = s0b_ref[...].T
    else:
        stf_ref[...] = jnp.zeros((HGRN_V, HGRN_K), F32)
        stb_ref[...] = jnp.zeros((HGRN_V, HGRN_K), F32)
    rows = BLOCK_CHUNKS * CHUNK
    n_blocks = seq_len // rows

    def step(rf, rb):
        of_ref[rf, :] = _hgrn_block(hq_ref[rf, :], ff_ref[rf, :], hi_ref[rf, :], lb_f,
                                    sumf_ref, mskf_ref, stf_ref, False)
        ob_ref[rb, :] = _hgrn_block(hq_ref[rb, :], fb_ref[rb, :], hi_ref[rb, :], lb_b,
                                    sumb_ref, mskb_ref, stb_ref, True)

    if n_blocks == 1:
        step(pl.ds(0, rows), pl.ds(0, rows))
    else:
        def body(i, carry):
            step(pl.ds(pl.multiple_of(i * rows, rows), rows),
                 pl.ds(pl.multiple_of((n_blocks - 1 - i) * rows, rows), rows))
            return carry

        lax.fori_loop(0, n_blocks, body, 0)
    o = of_ref[...] + ob_ref[...]
    o = o * lax.rsqrt(jnp.mean(o * o, axis=-1, keepdims=True) + EPS) * hn_ref[...]
    rec_ref[...] = o * hg_ref[...]
    if not has_init:
        sf_ref[...] = stf_ref[...].T
        sb_ref[...] = stb_ref[...].T


def _hgrn(l, h5, row_off, n_seq, seq_len, lb_fwd, lb_bwd, hnorm_w, tables, init=None):
    blocks_off = row_off // seq_len
    part = lambda k: pl.BlockSpec((seq_len, HGRN_K),
                                  lambda b, h: (blocks_off + b, k * HGRN_HEADS + h))
    lb_spec = pl.BlockSpec((DEPTH, HGRN_K), lambda b, h: (0, h))
    sum_spec = pl.BlockSpec((8 * CHUNK, 2 * CHUNK), lambda b, h: (0, 0))
    msk_spec = pl.BlockSpec((len(HGRN_LEVELS) + 1, CHUNK, CHUNK), lambda b, h: (0, 0, 0))
    in_specs = [part(0), part(1), part(2), part(3), part(4), lb_spec, lb_spec,
                pl.BlockSpec((None, 1, HGRN_V), lambda b, h: (l, 0, 0)),
                sum_spec, sum_spec, msk_spec, msk_spec]
    args = [h5, h5, h5, h5, h5, lb_fwd, lb_bwd, hnorm_w, *tables]
    rec_spec = pl.BlockSpec((seq_len, HGRN_V), lambda b, h: (b, h))
    rec_shape = jax.ShapeDtypeStruct((n_seq * seq_len, HGRN_WIDTH), F32)
    if init is not None:
        st_spec = pl.BlockSpec((None, None, None, HGRN_K, HGRN_V), lambda b, h: (b, l, h, 0, 0))
        in_specs += [st_spec, st_spec]
        args += list(init)
        out_specs, out_shape = rec_spec, rec_shape
    else:
        st_spec = pl.BlockSpec((None, None, HGRN_K, HGRN_V), lambda b, h: (b, h, 0, 0))
        st_shape = jax.ShapeDtypeStruct((n_seq, HGRN_HEADS, HGRN_K, HGRN_V), F32)
        out_specs, out_shape = [rec_spec, st_spec, st_spec], [rec_shape, st_shape, st_shape]
    return pl.pallas_call(
        functools.partial(_hgrn_kernel, l, seq_len, init is not None),
        grid=(n_seq, HGRN_HEADS),
        in_specs=in_specs,
        out_specs=out_specs,
        out_shape=out_shape,
        scratch_shapes=[pltpu.VMEM((seq_len, HGRN_V), F32), pltpu.VMEM((seq_len, HGRN_V), F32),
                        pltpu.VMEM((HGRN_V, HGRN_K), F32), pltpu.VMEM((HGRN_V, HGRN_K), F32)],
        compiler_params=_params("arbitrary", "arbitrary"),
        name="hgrn_lat" if init is not None else "hgrn_ctx",
    )(*args)


def _route(scores, bias, sel_member, sel_group):
    sel = scores + bias
    members = [_dot_exact(sel, sel_member[k]) for k in range(EXPERTS_PER_GROUP)]
    pair = None
    for i in range(EXPERTS_PER_GROUP):
        for j in range(i + 1, EXPERTS_PER_GROUP):
            s = members[i] + members[j]
            pair = s if pair is None else jnp.maximum(pair, s)
    lane = lax.broadcasted_iota(jnp.int32, sel.shape, 1)
    best = _dot_exact(pair, sel_group[0])
    best_g = jnp.zeros(sel.shape, jnp.int32)
    for m in range(1, N_GROUPS):
        cand = _dot_exact(pair, sel_group[m])
        upd = cand > best
        best_g = jnp.where(upd, m, best_g)
        best = jnp.where(upd, cand, best)
    in_group = (lane >> 2) == best_g
    pos = lane & (EXPERTS_PER_GROUP - 1)
    rank = jnp.zeros(sel.shape, jnp.int32)
    for k in range(EXPERTS_PER_GROUP):
        ahead = (members[k] > sel) | ((members[k] == sel) & (k < pos))
        rank = rank + jnp.where(ahead, 1, 0)
    w = jnp.where(in_group & (rank < 2), scores, 0.0)
    return w / w.sum(axis=-1, keepdims=True)


def _merge_kernel(attn_ref, rec_ref, g_ref, x_ref, mod_ref, n2_ref, wa_ref, wh_ref, wo_ref,
                  wr_ref, br_ref, selm_ref, selg_ref, x1_ref, h2_ref, gate_ref):
    ya = _dot(attn_ref[...], wa_ref[...])
    yh = _dot(rec_ref[...].astype(BF16), wh_ref[...])
    merged = g_ref[:, 0:D_MODEL] * ya + g_ref[:, D_MODEL:2 * D_MODEL] * yh
    out = _dot(merged.astype(BF16), wo_ref[...])
    x1 = x_ref[...] + mod_ref[2:3, :] * out
    x1_ref[...] = x1
    xn = x1 * lax.rsqrt(jnp.mean(x1 * x1, axis=-1, keepdims=True) + EPS) * n2_ref[...]
    h2 = (xn * (1.0 + mod_ref[4:5, :]) + mod_ref[3:4, :]).astype(BF16)
    h2_ref[...] = h2
    scores = _sigmoid(_dot(h2, wr_ref[...]))
    gate_ref[...] = _route(scores, br_ref[...], [selm_ref[k] for k in range(EXPERTS_PER_GROUP)],
                           [selg_ref[m] for m in range(N_GROUPS)])


def _merge(l, attn, rec, gates, x, mod, norm2_w, w_br_attn, w_br_hgrn, w_out, w_router, b_router,
           sel_member, sel_group):
    row = lambda w: pl.BlockSpec((TOK_TILE, w), lambda i: (i, 0))
    layer = lambda r, c: pl.BlockSpec((None, r, c), lambda i: (l, 0, 0))
    full = lambda *s: pl.BlockSpec(s, lambda i: (0,) * len(s))
    return pl.pallas_call(
        _merge_kernel,
        grid=(N_TOK // TOK_TILE,),
        in_specs=[
            row(ATTN_WIDTH), row(HGRN_WIDTH), row(2 * D_MODEL), row(D_MODEL),
            pl.BlockSpec((None, None, N_MOD, D_MODEL), lambda i: (l, _tile_cond(i), 0, 0)),
            layer(1, D_MODEL),
            layer(ATTN_WIDTH, D_MODEL), layer(HGRN_WIDTH, D_MODEL), layer(D_MODEL, D_MODEL),
            full(D_MODEL, N_EXPERTS), full(1, N_EXPERTS),
            full(EXPERTS_PER_GROUP, N_EXPERTS, N_EXPERTS), full(N_GROUPS, N_EXPERTS, N_EXPERTS),
        ],
        out_specs=[row(D_MODEL), row(D_MODEL), row(N_EXPERTS)],
        out_shape=[
            jax.ShapeDtypeStruct((N_TOK, D_MODEL), F32),
            jax.ShapeDtypeStruct((N_TOK, D_MODEL), BF16),
            jax.ShapeDtypeStruct((N_TOK, N_EXPERTS), F32),
        ],
        compiler_params=_params("arbitrary"),
        name="merge_router",
    )(attn, rec, gates, x, mod, norm2_w, w_br_attn, w_br_hgrn, w_out, w_router, b_router,
      sel_member, sel_group)


def _moe_kernel(h_ref, gate_ref, x1_ref, mod_ref, wg_ref, wu_ref, wd_ref, o_ref, acc_ref):
    e = pl.program_id(1)

    @pl.when(e == 0)
    def _():
        acc_ref[...] = jnp.zeros_like(acc_ref)

    h = h_ref[...]
    a = _dot(h, wg_ref[...].astype(BF16))
    u = _dot(h, wu_ref[...].astype(BF16))
    pick = (lax.broadcasted_iota(jnp.int32, (N_EXPERTS, D_FF), 0) == e).astype(F32)
    ge = _dot_exact(gate_ref[...], pick)
    act = a * _sigmoid(a) * u * ge
    acc_ref[...] += _dot(act.astype(BF16), wd_ref[...].astype(BF16))

    @pl.when(e == N_EXPERTS - 1)
    def _():
        o_ref[...] = x1_ref[...] + mod_ref[5:6, :] * acc_ref[...]


_MOE_CTX_TILES = N_CTX // MOE_TILE
_MOE_LAT_TILES_PER_SEQ = DEC_SEQ // MOE_TILE


def _moe_cond(i):
    return jnp.where(i < _MOE_CTX_TILES, 0, 1 + (i - _MOE_CTX_TILES) // _MOE_LAT_TILES_PER_SEQ)


def _moe(l, h2, gates, x1, mod, w_gate, w_up, w_down):
    row = lambda w: pl.BlockSpec((MOE_TILE, w), lambda i, e: (i, 0))
    return pl.pallas_call(
        _moe_kernel,
        grid=(N_TOK // MOE_TILE, N_EXPERTS),
        in_specs=[
            row(D_MODEL), row(N_EXPERTS), row(D_MODEL),
            pl.BlockSpec((None, None, N_MOD, D_MODEL), lambda i, e: (l, _moe_cond(i), 0, 0)),
            pl.BlockSpec((None, None, D_MODEL, D_FF), lambda i, e: (l, e, 0, 0)),
            pl.BlockSpec((None, None, D_MODEL, D_FF), lambda i, e: (l, e, 0, 0)),
            pl.BlockSpec((None, None, D_FF, D_MODEL), lambda i, e: (l, e, 0, 0)),
        ],
        out_specs=row(D_MODEL),
        out_shape=jax.ShapeDtypeStruct((N_TOK, D_MODEL), F32),
        scratch_shapes=[pltpu.VMEM((MOE_TILE, D_MODEL), F32)],
        compiler_params=_params("arbitrary", "arbitrary"),
        name="experts",
    )(h2, gates, x1, mod, w_gate, w_up, w_down)


def _rope_tables():
    pos = np.arange(DEC_SEQ)
    inv_freq = ROPE_THETA ** (-np.arange(0, AXIS_DIM, 2, dtype=np.float32) / AXIS_DIM)
    ang_r = (pos // GRID_W).astype(np.float32)[:, None] * inv_freq[None, :]
    ang_c = (pos % GRID_W).astype(np.float32)[:, None] * inv_freq[None, :]
    ang = jnp.asarray(np.concatenate([ang_r, ang_r, ang_c, ang_c], axis=-1).astype(np.float32))
    sign = np.where(np.arange(HEAD_DIM) % AXIS_DIM < AXIS_DIM // 2, -1.0, 1.0).astype(np.float32)
    cos = jnp.concatenate([jnp.cos(ang), jnp.ones((TOK_TILE, HEAD_DIM), F32)], axis=0)
    sin = jnp.concatenate([jnp.sin(ang) * sign, jnp.zeros((TOK_TILE, HEAD_DIM), F32)], axis=0)
    return jnp.tile(cos, (1, N_Q_HEADS)), jnp.tile(sin, (1, N_Q_HEADS))


def _selectors():
    lane = np.arange(N_EXPERTS)
    member = np.stack([(lane[:, None] == (lane[None, :] // EXPERTS_PER_GROUP) * EXPERTS_PER_GROUP + k)
                       for k in range(EXPERTS_PER_GROUP)]).astype(np.float32)
    group = np.stack([np.broadcast_to(lane[:, None] == m * EXPERTS_PER_GROUP, (N_EXPERTS, N_EXPERTS))
                      for m in range(N_GROUPS)]).astype(np.float32)
    head = (np.arange(ATTN_WIDTH)[:, None] // HEAD_DIM == np.arange(ATTN_WIDTH)[None, :] // HEAD_DIM)
    return jnp.asarray(member), jnp.asarray(group), jnp.asarray(head.astype(np.float32))


def kernel(x_prompt, x_sample, cache_k, cache_v, state_fwd, state_bwd, c, c_ctx, norm1_w, norm2_w, w_mod, b_mod, w_in, q_norm_w, k_norm_w, hgrn_lb_fwd, hgrn_lb_bwd, hgrn_norm_w, w_br_attn, w_br_hgrn, w_out, w_router, b_router, w_exp_gate, w_exp_up, w_exp_down):
    cos_t, sin_t = _rope_tables()
    sel_member, sel_group, head_ones = _selectors()
    sums_f, masks_f = _hgrn_tables(False)
    sums_b, masks_b = _hgrn_tables(True)
    hgrn_tables = (sums_f, sums_b, masks_f, masks_b)
    cond = jnp.concatenate([c_ctx[None, :], c, jnp.zeros((N_COND - 1 - DEC_BATCH, D_MODEL), F32)], axis=0)
    mod = _modulation(cond, w_mod, b_mod)

    w_in_b = w_in.astype(BF16)
    w_ba_b = w_br_attn.astype(BF16)
    w_bh_b = w_br_hgrn.astype(BF16)
    w_out_b = w_out.astype(BF16)
    w_router_b = w_router.astype(BF16)
    b_router_r = b_router.reshape(1, N_EXPERTS)
    norm1_r = norm1_w.reshape(DEPTH, 1, D_MODEL)
    norm2_r = norm2_w.reshape(DEPTH, 1, D_MODEL)
    hnorm_r = hgrn_norm_w.reshape(DEPTH, 1, HGRN_V)
    qn_r = jnp.tile(q_norm_w, (1, N_Q_HEADS)).reshape(DEPTH, 1, ATTN_WIDTH)
    kn_r = jnp.tile(k_norm_w, (1, N_KV_HEADS)).reshape(DEPTH, 1, KV_WIDTH)

    x = jnp.concatenate([x_prompt.reshape(N_CTX, D_MODEL), x_sample.reshape(N_LAT, D_MODEL)], axis=0)
    ks_out, vs_out, sf_out, sb_out = [], [], [], []
    for l in range(DEPTH):
        q, k, v, h5, gates = _input_projection(l, x, mod, norm1_r, w_in_b, cos_t, sin_t, qn_r, kn_r,
                                               head_ones)
        attn_c = _attention_ctx(q, k, v)
        attn_s = _attention_lat(l, q, k, v, cache_k, cache_v)
        rec_c, sf, sb = _hgrn(l, h5, 0, BATCH, SEQ, hgrn_lb_fwd, hgrn_lb_bwd, hnorm_r, hgrn_tables)
        rec_s = _hgrn(l, h5, N_CTX, DEC_BATCH, DEC_SEQ, hgrn_lb_fwd, hgrn_lb_bwd, hnorm_r,
                      hgrn_tables, init=(state_fwd, state_bwd))
        attn = jnp.concatenate([attn_c, attn_s], axis=0)
        rec = jnp.concatenate([rec_c, rec_s], axis=0)
        x1, h2, route = _merge(l, attn, rec, gates, x, mod, norm2_r, w_ba_b, w_bh_b, w_out_b,
                               w_router_b, b_router_r, sel_member, sel_group)
        x = _moe(l, h2, route, x1, mod, w_exp_gate, w_exp_up, w_exp_down)
        ks_out.append(k[:N_CTX].reshape(BATCH, SEQ, N_KV_HEADS, HEAD_DIM).transpose(0, 2, 1, 3))
        vs_out.append(v[:N_CTX].reshape(BATCH, SEQ, N_KV_HEADS, HEAD_DIM).transpose(0, 2, 1, 3))
        sf_out.append(sf)
        sb_out.append(sb)

    y_prompt = x[:N_CTX].reshape(BATCH, SEQ, D_MODEL)
    y_sample = x[N_CTX:].reshape(DEC_BATCH, DEC_SEQ, D_MODEL)
    return (y_prompt, y_sample, jnp.stack(ks_out, axis=1), jnp.stack(vs_out, axis=1),
            jnp.stack(sf_out, axis=1), jnp.stack(sb_out, axis=1))
```

```python
import functools

import numpy as np
import jax
import jax.numpy as jnp
from jax import lax
from jax.experimental import pallas as pl
from jax.experimental.pallas import tpu as pltpu

F32 = jnp.float32
BF16 = jnp.bfloat16

D_MODEL = 1024
BATCH = 16
SEQ = 256
DEPTH = 4
DEC_BATCH = 4
DEC_SEQ = 2048
PAST_LEN = 512
GRID_W = 64
N_Q_HEADS = 8
N_KV_HEADS = 2
GQA_GROUP = N_Q_HEADS // N_KV_HEADS
HEAD_DIM = 64
AXIS_DIM = HEAD_DIM // 2
ATTN_WIDTH = N_Q_HEADS * HEAD_DIM
KV_WIDTH = N_KV_HEADS * HEAD_DIM
ROPE_THETA = 10000.0
HGRN_HEADS = 4
HGRN_K = 128
HGRN_V = 128
HGRN_WIDTH = HGRN_HEADS * HGRN_K
N_EXPERTS = 16
N_GROUPS = 4
EXPERTS_PER_GROUP = N_EXPERTS // N_GROUPS
D_FF = 512
EPS = 1e-6
IN_COLS = ATTN_WIDTH + 2 * KV_WIDTH + 5 * HGRN_WIDTH + 2 * D_MODEL

N_CTX = BATCH * SEQ
N_LAT = DEC_BATCH * DEC_SEQ
N_TOK = N_CTX + N_LAT
N_COND = 8
N_MOD = 6

C_Q = 0
C_K = C_Q + ATTN_WIDTH
C_V = C_K + KV_WIDTH
C_HQ = C_V + KV_WIDTH
C_FF = C_HQ + HGRN_WIDTH
C_FB = C_FF + HGRN_WIDTH
C_HI = C_FB + HGRN_WIDTH
C_HG = C_HI + HGRN_WIDTH
C_GA = C_HG + HGRN_WIDTH
C_GH = C_GA + D_MODEL

LANES = 128
SUBLANES = 8
FEAT_ROWS = D_MODEL // LANES
SLAB_ROWS = 2 * FEAT_ROWS

TOK_TILE = 512
PLAN_TILE = 1024
SORT_TILE = 512
N_SORT_TILES = N_TOK // SORT_TILE + N_GROUPS
Q_TILE = 512
CHUNK = 64
HGRN_LEVELS = (32, 16, 8, 4, 2, 1)
BLOCK_CHUNKS = 4
LOG2E = 1.4426950408889634
VMEM_LIMIT = 56 * 1024 * 1024

_HI = lax.Precision.HIGHEST


def _sigmoid(x):
    return 1.0 / (1.0 + jnp.exp(-x))


def _dot(a, b):
    return jnp.dot(a, b, preferred_element_type=F32)


def _dot_nt(a, b, precision=None):
    return lax.dot_general(a, b, (((1,), (1,)), ((), ())), precision=precision,
                           preferred_element_type=F32)


def _dot_tn(a, b, precision=None):
    return lax.dot_general(a, b, (((0,), (0,)), ((), ())), precision=precision,
                           preferred_element_type=F32)


def _dot_exact(a, b):
    return jnp.dot(a, b, precision=_HI, preferred_element_type=F32)


def _params(*sem):
    return pltpu.CompilerParams(dimension_semantics=sem, vmem_limit_bytes=VMEM_LIMIT)


def _mod_kernel(cond_ref, w_ref, b_ref, out_ref):
    c = cond_ref[...]
    sc = c * _sigmoid(c)
    out_ref[...] = _dot(sc.astype(BF16), w_ref[...].astype(BF16)) + b_ref[...]


def _modulation(cond, w_mod, b_mod):
    out = pl.pallas_call(
        _mod_kernel,
        grid=(DEPTH, N_MOD),
        in_specs=[
            pl.BlockSpec((N_COND, D_MODEL), lambda l, j: (0, 0)),
            pl.BlockSpec((None, D_MODEL, D_MODEL), lambda l, j: (l, 0, j)),
            pl.BlockSpec((None, 1, D_MODEL), lambda l, j: (l, 0, j)),
        ],
        out_specs=pl.BlockSpec((None, None, N_COND, D_MODEL), lambda l, j: (l, j, 0, 0)),
        out_shape=jax.ShapeDtypeStruct((DEPTH, N_MOD, N_COND, D_MODEL), F32),
        compiler_params=_params("arbitrary", "arbitrary"),
        name="modulation",
    )(cond, w_mod, b_mod.reshape(DEPTH, 1, N_MOD * D_MODEL))
    return out.transpose(0, 2, 1, 3)


def _head_mean_sq(a, bd):
    return _dot_exact(a * a, bd) * (1.0 / HEAD_DIM)


def _rope(x, cos, sin_signed):
    width = x.shape[-1]
    lane = lax.broadcasted_iota(jnp.int32, x.shape, 1)
    first = (lane & (AXIS_DIM - 1)) < (AXIS_DIM // 2)
    rot = jnp.where(first, pltpu.roll(x, width - AXIS_DIM // 2, 1), pltpu.roll(x, AXIS_DIM // 2, 1))
    return x * cos + rot * sin_signed


def _inproj_kernel(x_ref, mod_ref, n1_ref, w_ref, cos_ref, sin_ref, qn_ref, kn_ref, bd_ref,
                   q_ref, k_ref, v_ref, h_ref, g_ref):
    x = x_ref[...]
    xn = x * lax.rsqrt(jnp.mean(x * x, axis=-1, keepdims=True) + EPS) * n1_ref[...]
    xb = (xn * (1.0 + mod_ref[1:2, :]) + mod_ref[0:1, :]).astype(BF16)

    def proj(c0, width):
        return _dot(xb, w_ref[:, c0:c0 + width])

    a = proj(C_Q, ATTN_WIDTH)
    qn = a * lax.rsqrt(_head_mean_sq(a, bd_ref[...]) + EPS) * qn_ref[...]
    q_ref[...] = (_rope(qn, cos_ref[...], sin_ref[...]) * (HEAD_DIM ** -0.5)).astype(BF16)

    a = proj(C_K, KV_WIDTH)
    kn = a * lax.rsqrt(_head_mean_sq(a, bd_ref[:KV_WIDTH, :KV_WIDTH]) + EPS) * kn_ref[...]
    k_ref[...] = _rope(kn, cos_ref[:, :KV_WIDTH], sin_ref[:, :KV_WIDTH])
    v_ref[...] = proj(C_V, KV_WIDTH)

    a = proj(C_HQ, HGRN_WIDTH)
    h_ref[:, 0:HGRN_WIDTH] = a * _sigmoid(a) * (HGRN_K ** -0.5)
    h_ref[:, HGRN_WIDTH:2 * HGRN_WIDTH] = proj(C_FF, HGRN_WIDTH)
    h_ref[:, 2 * HGRN_WIDTH:3 * HGRN_WIDTH] = proj(C_FB, HGRN_WIDTH)
    h_ref[:, 3 * HGRN_WIDTH:4 * HGRN_WIDTH] = proj(C_HI, HGRN_WIDTH)
    a = proj(C_HG, HGRN_WIDTH)
    h_ref[:, 4 * HGRN_WIDTH:5 * HGRN_WIDTH] = a * _sigmoid(a)

    g_ref[:, 0:D_MODEL] = _sigmoid(proj(C_GA, D_MODEL))
    g_ref[:, D_MODEL:2 * D_MODEL] = _sigmoid(proj(C_GH, D_MODEL))


_CTX_TILES = N_CTX // TOK_TILE
_LAT_TILES_PER_SEQ = DEC_SEQ // TOK_TILE


def _tile_cond(i):
    return jnp.where(i < _CTX_TILES, 0, 1 + (i - _CTX_TILES) // _LAT_TILES_PER_SEQ)


def _tile_rope_block(i):
    return jnp.where(i < _CTX_TILES, _LAT_TILES_PER_SEQ, (i - _CTX_TILES) % _LAT_TILES_PER_SEQ)


def _input_projection(l, x, mod, norm1_w, w_in, cos_t, sin_t, qn_w, kn_w, bd):
    row = lambda w: pl.BlockSpec((TOK_TILE, w), lambda i: (i, 0))
    return pl.pallas_call(
        _inproj_kernel,
        grid=(N_TOK // TOK_TILE,),
        in_specs=[
            row(D_MODEL),
            pl.BlockSpec((None, None, N_MOD, D_MODEL), lambda i: (l, _tile_cond(i), 0, 0)),
            pl.BlockSpec((None, 1, D_MODEL), lambda i: (l, 0, 0)),
            pl.BlockSpec((None, D_MODEL, IN_COLS), lambda i: (l, 0, 0),
                         pipeline_mode=pl.Buffered(1)),
            pl.BlockSpec((TOK_TILE, ATTN_WIDTH), lambda i: (_tile_rope_block(i), 0)),
            pl.BlockSpec((TOK_TILE, ATTN_WIDTH), lambda i: (_tile_rope_block(i), 0)),
            pl.BlockSpec((None, 1, ATTN_WIDTH), lambda i: (l, 0, 0)),
            pl.BlockSpec((None, 1, KV_WIDTH), lambda i: (l, 0, 0)),
            pl.BlockSpec((ATTN_WIDTH, ATTN_WIDTH), lambda i: (0, 0)),
        ],
        out_specs=[row(ATTN_WIDTH), row(KV_WIDTH), row(KV_WIDTH), row(5 * HGRN_WIDTH),
                   row(2 * D_MODEL)],
        out_shape=[
            jax.ShapeDtypeStruct((N_TOK, ATTN_WIDTH), BF16),
            jax.ShapeDtypeStruct((N_TOK, KV_WIDTH), F32),
            jax.ShapeDtypeStruct((N_TOK, KV_WIDTH), F32),
            jax.ShapeDtypeStruct((N_TOK, 5 * HGRN_WIDTH), F32),
            jax.ShapeDtypeStruct((N_TOK, 2 * D_MODEL), F32),
        ],
        compiler_params=_params("arbitrary"),
        name="input_projection",
    )(x, mod, norm1_w, w_in, cos_t, sin_t, qn_w, kn_w, bd)


def _softmax_pv(scores, values):
    m = scores[0].max(axis=-1, keepdims=True)
    for s in scores[1:]:
        m = jnp.maximum(m, s.max(axis=-1, keepdims=True))
    num, den = None, None
    for s, v in zip(scores, values):
        p = jnp.exp(s - m)
        d = p.sum(axis=-1, keepdims=True)
        o = _dot(p.astype(BF16), v)
        num = o if num is None else num + o
        den = d if den is None else den + d
    return num / den


def _attn_ctx_kernel(q_ref, k_ref, v_ref, o_ref):
    for g in range(N_KV_HEADS):
        cols = slice(g * HEAD_DIM, (g + 1) * HEAD_DIM)
        kg = k_ref[:, cols].astype(BF16)
        vg = v_ref[:, cols].astype(BF16)
        for hh in range(GQA_GROUP):
            h = g * GQA_GROUP + hh
            hc = slice(h * HEAD_DIM, (h + 1) * HEAD_DIM)
            o = _softmax_pv([_dot_nt(q_ref[:, hc], kg)], [vg])
            o_ref[:, hc] = o.astype(BF16)


def _attention_ctx(q, k, v):
    return pl.pallas_call(
        _attn_ctx_kernel,
        grid=(BATCH,),
        in_specs=[
            pl.BlockSpec((SEQ, ATTN_WIDTH), lambda b: (b, 0)),
            pl.BlockSpec((SEQ, KV_WIDTH), lambda b: (b, 0)),
            pl.BlockSpec((SEQ, KV_WIDTH), lambda b: (b, 0)),
        ],
        out_specs=pl.BlockSpec((SEQ, ATTN_WIDTH), lambda b: (b, 0)),
        out_shape=jax.ShapeDtypeStruct((N_CTX, ATTN_WIDTH), BF16),
        compiler_params=_params("arbitrary"),
        name="attention_ctx",
    )(q, k, v)


def _attn_lat_kernel(q_ref, k_ref, v_ref, ck_ref, cv_ref, o_ref):
    for g in range(N_KV_HEADS):
        cols = slice(g * HEAD_DIM, (g + 1) * HEAD_DIM)
        kg = k_ref[:, cols].astype(BF16)
        vg = v_ref[:, cols].astype(BF16)
        ckg = ck_ref[g].astype(BF16)
        cvg = cv_ref[g].astype(BF16)
        for hh in range(GQA_GROUP):
            h = g * GQA_GROUP + hh
            hc = slice(h * HEAD_DIM, (h + 1) * HEAD_DIM)
            qh = q_ref[:, hc]
            o = _softmax_pv([_dot_nt(qh, kg), _dot_nt(qh, ckg)], [vg, cvg])
            o_ref[:, hc] = o.astype(BF16)


def _attention_lat(l, q, k, v, cache_k, cache_v):
    q_blocks = DEC_SEQ // Q_TILE
    q_off = N_CTX // Q_TILE
    kv_off = N_CTX // DEC_SEQ
    cache_spec = pl.BlockSpec((None, None, N_KV_HEADS, PAST_LEN, HEAD_DIM),
                              lambda b, i: (b, l, 0, 0, 0))
    return pl.pallas_call(
        _attn_lat_kernel,
        grid=(DEC_BATCH, q_blocks),
        in_specs=[
            pl.BlockSpec((Q_TILE, ATTN_WIDTH), lambda b, i: (q_off + b * q_blocks + i, 0)),
            pl.BlockSpec((DEC_SEQ, KV_WIDTH), lambda b, i: (kv_off + b, 0)),
            pl.BlockSpec((DEC_SEQ, KV_WIDTH), lambda b, i: (kv_off + b, 0)),
            cache_spec,
            cache_spec,
        ],
        out_specs=pl.BlockSpec((Q_TILE, ATTN_WIDTH), lambda b, i: (b * q_blocks + i, 0)),
        out_shape=jax.ShapeDtypeStruct((N_LAT, ATTN_WIDTH), BF16),
        compiler_params=_params("arbitrary", "arbitrary"),
        name="attention_lat",
    )(q, k, v, cache_k, cache_v)


def _lower_bound(lb_ref, l):
    z = lb_ref[...]
    e = jnp.exp(z - z.max(axis=0, keepdims=True))
    p = e / e.sum(axis=0, keepdims=True)
    acc = p[0:1] * 0.0
    for r in range(1, l + 1):
        acc = acc + p[r:r + 1]
    return acc


def _hgrn_tables(rev):
    t = np.arange(CHUNK)
    tt, uu = t[:, None], t[None, :]
    groups = []
    masks = []
    for h in HGRN_LEVELS:
        right = (t & h) != 0
        edge = ((t // (2 * h)) * 2 * h + h - 1)[:, None]
        groups.append(np.where(right[:, None], (uu > edge) & (uu <= tt), (uu > tt) & (uu <= edge)))
        same = (tt // (2 * h)) == (uu // (2 * h))
        masks.append(same & right[:, None] & ~right[None, :])
    masks.append(tt == uu)
    groups += [uu <= tt, uu > tt]
    sums = np.stack(groups).astype(np.float32)
    masks = np.stack(masks).astype(np.float32)
    if rev:
        sums = sums[:, ::-1, ::-1]
        masks = masks[:, ::-1, ::-1]
    sums = sums.reshape(-1, CHUNK)
    return (jnp.asarray(np.concatenate([sums, sums], axis=1), dtype=BF16),
            jnp.asarray(np.ascontiguousarray(masks)))


def _hgrn_block(q, x, v, lb, sums_ref, masks_ref, st_ref, rev):
    n = q.shape[0] // CHUNK
    f = lb + (1.0 - lb) * _sigmoid(x)
    kk = 1.0 - f
    g = jnp.log(f) * LOG2E
    g_hi = g.astype(BF16)
    g_lo = (g - g_hi.astype(F32)).astype(BF16)
    vb = v.astype(BF16)
    rid = lax.broadcasted_iota(jnp.int32, (CHUNK, HGRN_K), 0)
    edge = 0 if rev else CHUNK - 1
    n_lev = len(HGRN_LEVELS)
    outs = [None] * n
    for c in (reversed(range(n)) if rev else range(n)):
        rs = slice(c * CHUNK, (c + 1) * CHUNK)
        e = jnp.exp2(_dot(sums_ref[...], jnp.concatenate([g_hi[rs], g_lo[rs]], axis=0)))
        q_c, k_c, v_c = q[rs], kk[rs], vb[rs]
        a = _dot_nt(q_c.astype(BF16), k_c.astype(BF16)) * masks_ref[n_lev]
        for i, h in enumerate(HGRN_LEVELS):
            later = ((rid & h) == 0) if rev else ((rid & h) != 0)
            z = (jnp.where(later, q_c, k_c) * e[i * CHUNK:(i + 1) * CHUNK]).astype(BF16)
            a = a + _dot_nt(z, z) * masks_ref[i]
        e_in = e[n_lev * CHUNK:(n_lev + 1) * CHUNK]
        e_out = e[(n_lev + 1) * CHUNK:(n_lev + 2) * CHUNK]
        st = st_ref[...]
        outs[c] = (_dot(a.astype(BF16), v_c)
                   + _dot_nt((q_c * e_in).astype(BF16), st.astype(BF16)))
        k_dec = (k_c * e_out).astype(BF16)
        st_ref[...] = st * e_in[edge:edge + 1] + _dot_tn(v_c, k_dec)
    return jnp.concatenate(outs, axis=0)


def _hgrn_kernel(l, seq_len, has_init, *refs):
    (hq_ref, ff_ref, fb_ref, hi_ref, hg_ref, lbf_ref, lbb_ref, hn_ref,
     sumf_ref, sumb_ref, mskf_ref, mskb_ref) = refs[:12]
    refs = refs[12:]
    if has_init:
        s0f_ref, s0b_ref, rec_ref, of_ref, ob_ref, stf_ref, stb_ref = refs
    else:
        rec_ref, sf_ref, sb_ref, of_ref, ob_ref, stf_ref, stb_ref = refs
    lb_f = _lower_bound(lbf_ref, l)
    lb_b = _lower_bound(lbb_ref, l)
    if has_init:
        stf_ref[...] = s0f_ref[...].T
        stb_ref[...] = s0b_ref[...].T
    else:
        stf_ref[...] = jnp.zeros((HGRN_V, HGRN_K), F32)
        stb_ref[...] = jnp.zeros((HGRN_V, HGRN_K), F32)
    rows = BLOCK_CHUNKS * CHUNK
    n_blocks = seq_len // rows

    def step(rf, rb):
        of_ref[rf, :] = _hgrn_block(hq_ref[rf, :], ff_ref[rf, :], hi_ref[rf, :], lb_f,
                                    sumf_ref, mskf_ref, stf_ref, False)
        ob_ref[rb, :] = _hgrn_block(hq_ref[rb, :], fb_ref[rb, :], hi_ref[rb, :], lb_b,
                                    sumb_ref, mskb_ref, stb_ref, True)

    if n_blocks == 1:
        step(pl.ds(0, rows), pl.ds(0, rows))
    else:
        def body(i, carry):
            step(pl.ds(pl.multiple_of(i * rows, rows), rows),
                 pl.ds(pl.multiple_of((n_blocks - 1 - i) * rows, rows), rows))
            return carry

        lax.fori_loop(0, n_blocks, body, 0)
    o = of_ref[...] + ob_ref[...]
    o = o * lax.rsqrt(jnp.mean(o * o, axis=-1, keepdims=True) + EPS) * hn_ref[...]
    rec_ref[...] = o * hg_ref[...]
    if not has_init:
        sf_ref[...] = stf_ref[...].T
        sb_ref[...] = stb_ref[...].T


def _hgrn(l, h5, row_off, n_seq, seq_len, lb_fwd, lb_bwd, hnorm_w, tables, init=None):
    blocks_off = row_off // seq_len
    part = lambda k: pl.BlockSpec((seq_len, HGRN_K),
                                  lambda b, h: (blocks_off + b, k * HGRN_HEADS + h))
    lb_spec = pl.BlockSpec((DEPTH, HGRN_K), lambda b, h: (0, h))
    sum_spec = pl.BlockSpec((8 * CHUNK, 2 * CHUNK), lambda b, h: (0, 0))
    msk_spec = pl.BlockSpec((len(HGRN_LEVELS) + 1, CHUNK, CHUNK), lambda b, h: (0, 0, 0))
    in_specs = [part(0), part(1), part(2), part(3), part(4), lb_spec, lb_spec,
                pl.BlockSpec((None, 1, HGRN_V), lambda b, h: (l, 0, 0)),
                sum_spec, sum_spec, msk_spec, msk_spec]
    args = [h5, h5, h5, h5, h5, lb_fwd, lb_bwd, hnorm_w, *tables]
    rec_spec = pl.BlockSpec((seq_len, HGRN_V), lambda b, h: (b, h))
    rec_shape = jax.ShapeDtypeStruct((n_seq * seq_len, HGRN_WIDTH), F32)
    if init is not None:
        st_spec = pl.BlockSpec((None, None, None, HGRN_K, HGRN_V), lambda b, h: (b, l, h, 0, 0))
        in_specs += [st_spec, st_spec]
        args += list(init)
        out_specs, out_shape = rec_spec, rec_shape
    else:
        st_spec = pl.BlockSpec((None, None, HGRN_K, HGRN_V), lambda b, h: (b, h, 0, 0))
        st_shape = jax.ShapeDtypeStruct((n_seq, HGRN_HEADS, HGRN_K, HGRN_V), F32)
        out_specs, out_shape = [rec_spec, st_spec, st_spec], [rec_shape, st_shape, st_shape]
    return pl.pallas_call(
        functools.partial(_hgrn_kernel, l, seq_len, init is not None),
        grid=(n_seq, HGRN_HEADS),
        in_specs=in_specs,
        out_specs=out_specs,
        out_shape=out_shape,
        scratch_shapes=[pltpu.VMEM((seq_len, HGRN_V), F32), pltpu.VMEM((seq_len, HGRN_V), F32),
                        pltpu.VMEM((HGRN_V, HGRN_K), F32), pltpu.VMEM((HGRN_V, HGRN_K), F32)],
        compiler_params=_params("arbitrary", "arbitrary"),
        name="hgrn_lat" if init is not None else "hgrn_ctx",
    )(*args)


def _route(scores, bias, sel_member, sel_group):
    sel = scores + bias
    members = [_dot_exact(sel, sel_member[k]) for k in range(EXPERTS_PER_GROUP)]
    pair = None
    for i in range(EXPERTS_PER_GROUP):
        for j in range(i + 1, EXPERTS_PER_GROUP):
            s = members[i] + members[j]
            pair = s if pair is None else jnp.maximum(pair, s)
    lane = lax.broadcasted_iota(jnp.int32, sel.shape, 1)
    best = _dot_exact(pair, sel_group[0])
    best_g = jnp.zeros(sel.shape, jnp.int32)
    for m in range(1, N_GROUPS):
        cand = _dot_exact(pair, sel_group[m])
        upd = cand > best
        best_g = jnp.where(upd, m, best_g)
        best = jnp.where(upd, cand, best)
    in_group = (lane >> 2) == best_g
    pos = lane & (EXPERTS_PER_GROUP - 1)
    rank = jnp.zeros(sel.shape, jnp.int32)
    for k in range(EXPERTS_PER_GROUP):
        ahead = (members[k] > sel) | ((members[k] == sel) & (k < pos))
        rank = rank + jnp.where(ahead, 1, 0)
    w = jnp.where(in_group & (rank < 2), scores, 0.0)
    return w / w.sum(axis=-1, keepdims=True)


def _merge_kernel(attn_ref, rec_ref, g_ref, x_ref, mod_ref, n2_ref, wa_ref, wh_ref, wo_ref,
                  wr_ref, br_ref, selm_ref, selg_ref, pad_ref, x1_ref, slab_ref, route_ref):
    ya = _dot(attn_ref[...], wa_ref[...])
    yh = _dot(rec_ref[...].astype(BF16), wh_ref[...])
    merged = g_ref[:, 0:D_MODEL] * ya + g_ref[:, D_MODEL:2 * D_MODEL] * yh
    out = _dot(merged.astype(BF16), wo_ref[...])
    x1 = x_ref[...] + mod_ref[2:3, :] * out
    x1_ref[...] = x1
    xn = x1 * lax.rsqrt(jnp.mean(x1 * x1, axis=-1, keepdims=True) + EPS) * n2_ref[...]
    h2 = xn * (1.0 + mod_ref[4:5, :]) + mod_ref[3:4, :]
    scores = _sigmoid(_dot_exact(h2, wr_ref[...]))
    gates = _route(scores, br_ref[...], [selm_ref[k] for k in range(EXPERTS_PER_GROUP)],
                   [selg_ref[m] for m in range(N_GROUPS)])
    gates = _dot_exact(gates, pad_ref[...])
    route_ref[...] = gates
    for s in range(FEAT_ROWS):
        slab_ref[:, s, :] = h2[:, s * LANES:(s + 1) * LANES]
    slab_ref[:, FEAT_ROWS, :] = gates
    for s in range(FEAT_ROWS + 1, SLAB_ROWS):
        slab_ref[:, s, :] = jnp.zeros_like(gates)


def _merge(l, attn, rec, gates, x, mod, norm2_w, w_br_attn, w_br_hgrn, w_out, w_router, b_router,
           sel_member, sel_group, lane_pad):
    row = lambda w: pl.BlockSpec((TOK_TILE, w), lambda i: (i, 0))
    layer = lambda r, c: pl.BlockSpec((None, r, c), lambda i: (l, 0, 0))
    full = lambda *s: pl.BlockSpec(s, lambda i: (0,) * len(s))
    return pl.pallas_call(
        _merge_kernel,
        grid=(N_TOK // TOK_TILE,),
        in_specs=[
            row(ATTN_WIDTH), row(HGRN_WIDTH), row(2 * D_MODEL), row(D_MODEL),
            pl.BlockSpec((None, None, N_MOD, D_MODEL), lambda i: (l, _tile_cond(i), 0, 0)),
            layer(1, D_MODEL),
            layer(ATTN_WIDTH, D_MODEL), layer(HGRN_WIDTH, D_MODEL), layer(D_MODEL, D_MODEL),
            full(D_MODEL, N_EXPERTS), full(1, N_EXPERTS),
            full(EXPERTS_PER_GROUP, N_EXPERTS, N_EXPERTS), full(N_GROUPS, N_EXPERTS, N_EXPERTS),
            full(N_EXPERTS, LANES),
        ],
        out_specs=[row(D_MODEL),
                   pl.BlockSpec((TOK_TILE, SLAB_ROWS, LANES), lambda i: (i, 0, 0)),
                   row(LANES)],
        out_shape=[
            jax.ShapeDtypeStruct((N_TOK, D_MODEL), F32),
            jax.ShapeDtypeStruct((N_TOK, SLAB_ROWS, LANES), F32),
            jax.ShapeDtypeStruct((N_TOK, LANES), F32),
        ],
        compiler_params=_params("arbitrary"),
        name="merge_router",
    )(attn, rec, gates, x, mod, norm2_w, w_br_attn, w_br_hgrn, w_out, w_router, b_router,
      sel_member, sel_group, lane_pad)


def _plan_kernel(route_ref, dst_ref, meta_ref, tot_ref, run_ref):
    p, i = pl.program_id(0), pl.program_id(1)

    @pl.when(jnp.logical_and(p == 0, i == 0))
    def _():
        tot_ref[...] = jnp.zeros_like(tot_ref)

    @pl.when(i == 0)
    def _():
        run_ref[...] = jnp.zeros_like(run_ref)

    grp = lax.broadcasted_iota(jnp.int32, (SUBLANES, LANES), 0)
    lane = lax.broadcasted_iota(jnp.int32, (SUBLANES, LANES), 1)
    member = jnp.where(((lane >> 2) == grp) & (lane < N_EXPERTS), 1.0, 0.0)
    one_hot = jnp.where(_dot_nt(member, route_ref[...], precision=_HI) > 0.0, 1.0, 0.0)
    src = lax.broadcasted_iota(jnp.int32, (PLAN_TILE, PLAN_TILE), 0)
    tgt = lax.broadcasted_iota(jnp.int32, (PLAN_TILE, PLAN_TILE), 1)
    before = jnp.where(src < tgt, 1.0, 0.0).astype(BF16)
    rank = _dot(one_hot.astype(BF16), before) + run_ref[:, 0:1]
    tile_tot = one_hot.sum(axis=1, keepdims=True)

    @pl.when(p == 0)
    def _():
        tot_ref[...] += tile_tot

    @pl.when(p == 1)
    def _():
        run_ref[...] += tile_tot
        tot = tot_ref[:, 0:1]
        n_tiles = jnp.floor((tot + (SORT_TILE - 1)) * (1.0 / SORT_TILE))
        offs, acc = [], jnp.zeros((1, 1), F32)
        for m in range(N_GROUPS):
            offs.append(acc * SORT_TILE)
            acc = acc + n_tiles[m:m + 1]
        dst = sum(one_hot[m:m + 1] * (offs[m] + rank[m:m + 1]) for m in range(N_GROUPS))
        for r in range(PLAN_TILE // LANES):
            dst_ref[r:r + 1, :] = dst[:, r * LANES:(r + 1) * LANES].astype(jnp.int32)
        tile_row = (lax.broadcasted_iota(jnp.int32, (1, LANES), 1) * SORT_TILE).astype(F32)
        tile_group = sum(jnp.where(tile_row >= offs[m], 1.0, 0.0) for m in range(1, N_GROUPS))
        rows = [tile_group, jnp.broadcast_to(acc, (1, LANES))]
        rows += [jnp.broadcast_to(o, (1, LANES)) for o in offs]
        rows += [jnp.broadcast_to(tot[m:m + 1], (1, LANES)) for m in range(N_GROUPS)]
        rows += [jnp.zeros((1, LANES), F32)] * (2 * SUBLANES - len(rows))
        meta_ref[...] = jnp.concatenate(rows, axis=0).astype(jnp.int32)


def _plan(route):
    n_blocks = N_TOK // PLAN_TILE
    dst, meta = pl.pallas_call(
        _plan_kernel,
        grid=(2, n_blocks),
        in_specs=[pl.BlockSpec((PLAN_TILE, LANES), lambda p, i: (i, 0))],
        out_specs=[pl.BlockSpec((PLAN_TILE // LANES, LANES), lambda p, i: (i * p, 0)),
                   pl.BlockSpec((2 * SUBLANES, LANES), lambda p, i: (0, 0))],
        out_shape=[jax.ShapeDtypeStruct((N_TOK // LANES, LANES), jnp.int32),
                   jax.ShapeDtypeStruct((2 * SUBLANES, LANES), jnp.int32)],
        scratch_shapes=[pltpu.VMEM((SUBLANES, LANES), F32), pltpu.VMEM((SUBLANES, LANES), F32)],
        compiler_params=_params("arbitrary", "arbitrary"),
        name="route_plan",
    )(route)
    return (dst, meta[0, :N_SORT_TILES], meta[1, 0:1], meta[2:2 + N_GROUPS, 0],
            meta[2 + N_GROUPS:2 + 2 * N_GROUPS, 0])


def _row_copies(n, make_copy, sem):
    def start(i, carry):
        make_copy(i).start()
        return carry

    def wait(i, carry):
        make_copy(0).wait()
        return carry

    lax.fori_loop(0, n, start, 0)
    lax.fori_loop(0, n, wait, 0)


def _dispatch_kernel(seg_ref, cnt_ref, dst_ref, slab_ref, zero_ref, sorted_ref, sem):
    i = pl.program_id(0)
    for r in range(PLAN_TILE // LANES):
        base = i * PLAN_TILE + r * LANES
        _row_copies(LANES, lambda c, r=r, base=base: pltpu.make_async_copy(
            slab_ref.at[base + c], sorted_ref.at[dst_ref[r, c]], sem), sem)

    @pl.when(i == 0)
    def _():
        for m in range(N_GROUPS):
            used = cnt_ref[m]
            n_pad = (SORT_TILE - (used & (SORT_TILE - 1))) & (SORT_TILE - 1)
            first = seg_ref[m] + used
            _row_copies(n_pad, lambda c, first=first: pltpu.make_async_copy(
                zero_ref.at[0], sorted_ref.at[first + c], sem), sem)
        first_free = (first + n_pad) // SORT_TILE
        _row_copies(N_SORT_TILES - first_free, lambda t: pltpu.make_async_copy(
            zero_ref, sorted_ref.at[pl.ds((first_free + t) * SORT_TILE, SORT_TILE)], sem), sem)


def _dispatch(dst, seg, cnt, slab):
    zero = jnp.zeros((SORT_TILE, SLAB_ROWS, LANES), F32)
    return pl.pallas_call(
        _dispatch_kernel,
        grid_spec=pltpu.PrefetchScalarGridSpec(
            num_scalar_prefetch=2,
            grid=(N_TOK // PLAN_TILE,),
            in_specs=[
                pl.BlockSpec((PLAN_TILE // LANES, LANES), lambda i, seg, cnt: (i, 0),
                             memory_space=pltpu.SMEM),
                pl.BlockSpec(memory_space=pl.ANY),
                pl.BlockSpec(memory_space=pl.ANY),
            ],
            out_specs=pl.BlockSpec(memory_space=pl.ANY),
            scratch_shapes=[pltpu.SemaphoreType.DMA(())],
        ),
        out_shape=jax.ShapeDtypeStruct((N_SORT_TILES * SORT_TILE, SLAB_ROWS, LANES), F32),
        compiler_params=_params("arbitrary"),
        name="dispatch",
    )(seg, cnt, dst, slab, zero)


def _experts_kernel(tg_ref, nv_ref, x_ref, wg_ref, wu_ref, wd_ref, y_ref):
    j = pl.program_id(0)

    @pl.when(j < nv_ref[0])
    def _():
        first_expert = tg_ref[j] * EXPERTS_PER_GROUP
        h = jnp.concatenate([x_ref[:, s, :] for s in range(FEAT_ROWS)], axis=1).astype(BF16)
        gates = x_ref[:, FEAT_ROWS, :]
        lane = lax.broadcasted_iota(jnp.int32, (LANES, D_FF), 0)
        acc = None
        for k in range(EXPERTS_PER_GROUP):
            a = _dot(h, wg_ref[k])
            u = _dot(h, wu_ref[k])
            ge = _dot_exact(gates, jnp.where(lane == first_expert + k, 1.0, 0.0))
            y = _dot((a * _sigmoid(a) * u * ge).astype(BF16), wd_ref[k])
            acc = y if acc is None else acc + y
        for s in range(FEAT_ROWS):
            y_ref[:, s, :] = acc[:, s * LANES:(s + 1) * LANES]

    @pl.when(j >= nv_ref[0])
    def _():
        y_ref[...] = jnp.zeros_like(y_ref)


def _experts(l, tile_group, n_used, sorted_rows, w_gate, w_up, w_down):
    tile = lambda j, tg, nv: jnp.minimum(j, nv[0] - 1)
    w_spec = lambda r, c: pl.BlockSpec((None, None, EXPERTS_PER_GROUP, r, c),
                                       lambda j, tg, nv: (l, tg[tile(j, tg, nv)], 0, 0, 0))
    return pl.pallas_call(
        _experts_kernel,
        grid_spec=pltpu.PrefetchScalarGridSpec(
            num_scalar_prefetch=2,
            grid=(N_SORT_TILES,),
            in_specs=[
                pl.BlockSpec((SORT_TILE, SLAB_ROWS, LANES),
                             lambda j, tg, nv: (tile(j, tg, nv), 0, 0)),
                w_spec(D_MODEL, D_FF), w_spec(D_MODEL, D_FF), w_spec(D_FF, D_MODEL),
            ],
            out_specs=pl.BlockSpec((SORT_TILE, FEAT_ROWS, LANES), lambda j, tg, nv: (j, 0, 0)),
        ),
        out_shape=jax.ShapeDtypeStruct((N_SORT_TILES * SORT_TILE, FEAT_ROWS, LANES), F32),
        compiler_params=_params("arbitrary"),
        name="experts",
    )(tile_group, n_used, sorted_rows, w_gate, w_up, w_down)


def _combine_kernel(dst_ref, y_ref, x1_ref, mod_ref, o_ref, buf_ref, sem):
    for r in range(PLAN_TILE // LANES):
        _row_copies(LANES, lambda c, r=r: pltpu.make_async_copy(
            y_ref.at[dst_ref[r, c]], buf_ref.at[r * LANES + c], sem), sem)
    for s in range(FEAT_ROWS):
        cols = slice(s * LANES, (s + 1) * LANES)
        o_ref[:, cols] = x1_ref[:, cols] + mod_ref[5:6, cols] * buf_ref[:, s, :]


_PLAN_CTX_TILES = N_CTX // PLAN_TILE
_PLAN_LAT_TILES_PER_SEQ = DEC_SEQ // PLAN_TILE


def _plan_cond(i):
    return jnp.where(i < _PLAN_CTX_TILES, 0, 1 + (i - _PLAN_CTX_TILES) // _PLAN_LAT_TILES_PER_SEQ)


def _combine(l, dst, expert_out, x1, mod):
    return pl.pallas_call(
        _combine_kernel,
        grid=(N_TOK // PLAN_TILE,),
        in_specs=[
            pl.BlockSpec((PLAN_TILE // LANES, LANES), lambda i: (i, 0), memory_space=pltpu.SMEM),
            pl.BlockSpec(memory_space=pl.ANY),
            pl.BlockSpec((PLAN_TILE, D_MODEL), lambda i: (i, 0)),
            pl.BlockSpec((None, None, N_MOD, D_MODEL), lambda i: (l, _plan_cond(i), 0, 0)),
        ],
        out_specs=pl.BlockSpec((PLAN_TILE, D_MODEL), lambda i: (i, 0)),
        out_shape=jax.ShapeDtypeStruct((N_TOK, D_MODEL), F32),
        scratch_shapes=[pltpu.VMEM((PLAN_TILE, FEAT_ROWS, LANES), F32), pltpu.SemaphoreType.DMA(())],
        compiler_params=_params("arbitrary"),
        name="combine",
    )(dst, expert_out, x1, mod)


def _rope_tables():
    pos = np.arange(DEC_SEQ)
    inv_freq = ROPE_THETA ** (-np.arange(0, AXIS_DIM, 2, dtype=np.float32) / AXIS_DIM)
    ang_r = (pos // GRID_W).astype(np.float32)[:, None] * inv_freq[None, :]
    ang_c = (pos % GRID_W).astype(np.float32)[:, None] * inv_freq[None, :]
    ang = jnp.asarray(np.concatenate([ang_r, ang_r, ang_c, ang_c], axis=-1).astype(np.float32))
    sign = np.where(np.arange(HEAD_DIM) % AXIS_DIM < AXIS_DIM // 2, -1.0, 1.0).astype(np.float32)
    cos = jnp.concatenate([jnp.cos(ang), jnp.ones((TOK_TILE, HEAD_DIM), F32)], axis=0)
    sin = jnp.concatenate([jnp.sin(ang) * sign, jnp.zeros((TOK_TILE, HEAD_DIM), F32)], axis=0)
    return jnp.tile(cos, (1, N_Q_HEADS)), jnp.tile(sin, (1, N_Q_HEADS))


def _selectors():
    lane = np.arange(N_EXPERTS)
    member = np.stack([(lane[:, None] == (lane[None, :] // EXPERTS_PER_GROUP) * EXPERTS_PER_GROUP + k)
                       for k in range(EXPERTS_PER_GROUP)]).astype(np.float32)
    group = np.stack([np.broadcast_to(lane[:, None] == m * EXPERTS_PER_GROUP, (N_EXPERTS, N_EXPERTS))
                      for m in range(N_GROUPS)]).astype(np.float32)
    head = (np.arange(ATTN_WIDTH)[:, None] // HEAD_DIM == np.arange(ATTN_WIDTH)[None, :] // HEAD_DIM)
    return jnp.asarray(member), jnp.asarray(group), jnp.asarray(head.astype(np.float32))


def kernel(x_prompt, x_sample, cache_k, cache_v, state_fwd, state_bwd, c, c_ctx, norm1_w, norm2_w, w_mod, b_mod, w_in, q_norm_w, k_norm_w, hgrn_lb_fwd, hgrn_lb_bwd, hgrn_norm_w, w_br_attn, w_br_hgrn, w_out, w_router, b_router, w_exp_gate, w_exp_up, w_exp_down):
    cos_t, sin_t = _rope_tables()
    sel_member, sel_group, head_ones = _selectors()
    sums_f, masks_f = _hgrn_tables(False)
    sums_b, masks_b = _hgrn_tables(True)
    hgrn_tables = (sums_f, sums_b, masks_f, masks_b)
    cond = jnp.concatenate([c_ctx[None, :], c, jnp.zeros((N_COND - 1 - DEC_BATCH, D_MODEL), F32)], axis=0)
    mod = _modulation(cond, w_mod, b_mod)

    w_in_b = w_in.astype(BF16)
    w_ba_b = w_br_attn.astype(BF16)
    w_bh_b = w_br_hgrn.astype(BF16)
    w_out_b = w_out.astype(BF16)
    by_group = lambda w: w.astype(BF16).reshape(DEPTH, N_GROUPS, EXPERTS_PER_GROUP, *w.shape[2:])
    w_eg_b, w_eu_b, w_ed_b = by_group(w_exp_gate), by_group(w_exp_up), by_group(w_exp_down)
    lane_pad = jnp.eye(N_EXPERTS, LANES, dtype=F32)
    b_router_r = b_router.reshape(1, N_EXPERTS)
    norm1_r = norm1_w.reshape(DEPTH, 1, D_MODEL)
    norm2_r = norm2_w.reshape(DEPTH, 1, D_MODEL)
    hnorm_r = hgrn_norm_w.reshape(DEPTH, 1, HGRN_V)
    qn_r = jnp.tile(q_norm_w, (1, N_Q_HEADS)).reshape(DEPTH, 1, ATTN_WIDTH)
    kn_r = jnp.tile(k_norm_w, (1, N_KV_HEADS)).reshape(DEPTH, 1, KV_WIDTH)

    x = jnp.concatenate([x_prompt.reshape(N_CTX, D_MODEL), x_sample.reshape(N_LAT, D_MODEL)], axis=0)
    ks_out, vs_out, sf_out, sb_out = [], [], [], []
    for l in range(DEPTH):
        q, k, v, h5, gates = _input_projection(l, x, mod, norm1_r, w_in_b, cos_t, sin_t, qn_r, kn_r,
                                               head_ones)
        attn_c = _attention_ctx(q, k, v)
        attn_s = _attention_lat(l, q, k, v, cache_k, cache_v)
        rec_c, sf, sb = _hgrn(l, h5, 0, BATCH, SEQ, hgrn_lb_fwd, hgrn_lb_bwd, hnorm_r, hgrn_tables)
        rec_s = _hgrn(l, h5, N_CTX, DEC_BATCH, DEC_SEQ, hgrn_lb_fwd, hgrn_lb_bwd, hnorm_r,
                      hgrn_tables, init=(state_fwd, state_bwd))
        attn = jnp.concatenate([attn_c, attn_s], axis=0)
        rec = jnp.concatenate([rec_c, rec_s], axis=0)
        x1, slab, route = _merge(l, attn, rec, gates, x, mod, norm2_r, w_ba_b, w_bh_b, w_out_b,
                                 w_router, b_router_r, sel_member, sel_group, lane_pad)
        dst, tile_group, n_used, seg, cnt = _plan(route)
        sorted_rows = _dispatch(dst, seg, cnt, slab)
        expert_out = _experts(l, tile_group, n_used, sorted_rows, w_eg_b, w_eu_b, w_ed_b)
        x = _combine(l, dst, expert_out, x1, mod)
        ks_out.append(k[:N_CTX].reshape(BATCH, SEQ, N_KV_HEADS, HEAD_DIM).transpose(0, 2, 1, 3))
        vs_out.append(v[:N_CTX].reshape(BATCH, SEQ, N_KV_HEADS, HEAD_DIM).transpose(0, 2, 1, 3))
        sf_out.append(sf)
        sb_out.append(sb)

    y_prompt = x[:N_CTX].reshape(BATCH, SEQ, D_MODEL)
    y_sample = x[N_CTX:].reshape(DEC_BATCH, DEC_SEQ, D_MODEL)
    return (y_prompt, y_sample, jnp.stack(ks_out, axis=1), jnp.stack(vs_out, axis=1),
            jnp.stack(sf_out, axis=1), jnp.stack(sb_out, axis=1))
```

```python
import functools

import numpy as np
import jax
import jax.numpy as jnp
from jax import lax
from jax.experimental import pallas as pl
from jax.experimental.pallas import tpu as pltpu

F32 = jnp.float32
BF16 = jnp.bfloat16

D_MODEL = 1024
BATCH = 16
SEQ = 256
DEPTH = 4
DEC_BATCH = 4
DEC_SEQ = 2048
PAST_LEN = 512
GRID_W = 64
N_Q_HEADS = 8
N_KV_HEADS = 2
GQA_GROUP = N_Q_HEADS // N_KV_HEADS
HEAD_DIM = 64
AXIS_DIM = HEAD_DIM // 2
ATTN_WIDTH = N_Q_HEADS * HEAD_DIM
KV_WIDTH = N_KV_HEADS * HEAD_DIM
ROPE_THETA = 10000.0
HGRN_HEADS = 4
HGRN_K = 128
HGRN_V = 128
HGRN_WIDTH = HGRN_HEADS * HGRN_K
N_EXPERTS = 16
N_GROUPS = 4
EXPERTS_PER_GROUP = N_EXPERTS // N_GROUPS
D_FF = 512
EPS = 1e-6
IN_COLS = ATTN_WIDTH + 2 * KV_WIDTH + 5 * HGRN_WIDTH + 2 * D_MODEL

N_CTX = BATCH * SEQ
N_LAT = DEC_BATCH * DEC_SEQ
N_TOK = N_CTX + N_LAT
N_COND = 8
N_MOD = 6

C_Q = 0
C_K = C_Q + ATTN_WIDTH
C_V = C_K + KV_WIDTH
C_HQ = C_V + KV_WIDTH
C_FF = C_HQ + HGRN_WIDTH
C_FB = C_FF + HGRN_WIDTH
C_HI = C_FB + HGRN_WIDTH
C_HG = C_HI + HGRN_WIDTH
C_GA = C_HG + HGRN_WIDTH
C_GH = C_GA + D_MODEL

LANES = 128
SUBLANES = 8
FEAT_ROWS = D_MODEL // LANES
SLAB_ROWS = 2 * FEAT_ROWS

TOK_TILE = 512
PLAN_TILE = 1024
SORT_TILE = 512
N_SORT_TILES = N_TOK // SORT_TILE + N_GROUPS
Q_TILE = 512
CHUNK = 64
HGRN_LEVELS = (32, 16, 8, 4, 2, 1)
BLOCK_CHUNKS = 4
LOG2E = 1.4426950408889634
VMEM_LIMIT = 56 * 1024 * 1024

_HI = lax.Precision.HIGHEST


def _sigmoid(x):
    return 1.0 / (1.0 + jnp.exp(-x))


def _dot(a, b):
    return jnp.dot(a, b, preferred_element_type=F32)


def _dot_nt(a, b, precision=None):
    return lax.dot_general(a, b, (((1,), (1,)), ((), ())), precision=precision,
                           preferred_element_type=F32)


def _dot_tn(a, b, precision=None):
    return lax.dot_general(a, b, (((0,), (0,)), ((), ())), precision=precision,
                           preferred_element_type=F32)


def _dot_exact(a, b):
    return jnp.dot(a, b, precision=_HI, preferred_element_type=F32)


def _split_bf16(a):
    hi = a.astype(BF16)
    return hi, (a - hi.astype(F32)).astype(BF16)


def _params(*sem):
    return pltpu.CompilerParams(dimension_semantics=sem, vmem_limit_bytes=VMEM_LIMIT)


def _mod_kernel(cond_ref, w_ref, b_ref, out_ref):
    c = cond_ref[...]
    sc = c * _sigmoid(c)
    out_ref[...] = _dot(sc.astype(BF16), w_ref[...].astype(BF16)) + b_ref[...]


def _modulation(cond, w_mod, b_mod):
    out = pl.pallas_call(
        _mod_kernel,
        grid=(DEPTH, N_MOD),
        in_specs=[
            pl.BlockSpec((N_COND, D_MODEL), lambda l, j: (0, 0)),
            pl.BlockSpec((None, D_MODEL, D_MODEL), lambda l, j: (l, 0, j)),
            pl.BlockSpec((None, 1, D_MODEL), lambda l, j: (l, 0, j)),
        ],
        out_specs=pl.BlockSpec((None, None, N_COND, D_MODEL), lambda l, j: (l, j, 0, 0)),
        out_shape=jax.ShapeDtypeStruct((DEPTH, N_MOD, N_COND, D_MODEL), F32),
        compiler_params=_params("arbitrary", "arbitrary"),
        name="modulation",
    )(cond, w_mod, b_mod.reshape(DEPTH, 1, N_MOD * D_MODEL))
    return out.transpose(0, 2, 1, 3)


def _head_mean_sq(a, bd):
    return _dot_exact(a * a, bd) * (1.0 / HEAD_DIM)


def _rope(x, cos, sin_signed):
    width = x.shape[-1]
    lane = lax.broadcasted_iota(jnp.int32, x.shape, 1)
    first = (lane & (AXIS_DIM - 1)) < (AXIS_DIM // 2)
    rot = jnp.where(first, pltpu.roll(x, width - AXIS_DIM // 2, 1), pltpu.roll(x, AXIS_DIM // 2, 1))
    return x * cos + rot * sin_signed


def _inproj_kernel(x_ref, mod_ref, n1_ref, w_ref, cos_ref, sin_ref, qn_ref, kn_ref, bd_ref,
                   q_ref, k_ref, v_ref, h_ref, g_ref):
    x = x_ref[...]
    xn = x * lax.rsqrt(jnp.mean(x * x, axis=-1, keepdims=True) + EPS) * n1_ref[...]
    xb = (xn * (1.0 + mod_ref[1:2, :]) + mod_ref[0:1, :]).astype(BF16)

    def proj(c0, width):
        return _dot(xb, w_ref[:, c0:c0 + width])

    a = proj(C_Q, ATTN_WIDTH)
    qn = a * lax.rsqrt(_head_mean_sq(a, bd_ref[...]) + EPS) * qn_ref[...]
    q_ref[...] = (_rope(qn, cos_ref[...], sin_ref[...]) * (HEAD_DIM ** -0.5)).astype(BF16)

    a = proj(C_K, KV_WIDTH)
    kn = a * lax.rsqrt(_head_mean_sq(a, bd_ref[:KV_WIDTH, :KV_WIDTH]) + EPS) * kn_ref[...]
    k_ref[...] = _rope(kn, cos_ref[:, :KV_WIDTH], sin_ref[:, :KV_WIDTH])
    v_ref[...] = proj(C_V, KV_WIDTH)

    a = proj(C_HQ, HGRN_WIDTH)
    h_ref[:, 0:HGRN_WIDTH] = a * _sigmoid(a) * (HGRN_K ** -0.5)
    h_ref[:, HGRN_WIDTH:2 * HGRN_WIDTH] = proj(C_FF, HGRN_WIDTH)
    h_ref[:, 2 * HGRN_WIDTH:3 * HGRN_WIDTH] = proj(C_FB, HGRN_WIDTH)
    h_ref[:, 3 * HGRN_WIDTH:4 * HGRN_WIDTH] = proj(C_HI, HGRN_WIDTH)
    a = proj(C_HG, HGRN_WIDTH)
    h_ref[:, 4 * HGRN_WIDTH:5 * HGRN_WIDTH] = a * _sigmoid(a)

    g_ref[:, 0:D_MODEL] = _sigmoid(proj(C_GA, D_MODEL))
    g_ref[:, D_MODEL:2 * D_MODEL] = _sigmoid(proj(C_GH, D_MODEL))


_CTX_TILES = N_CTX // TOK_TILE
_LAT_TILES_PER_SEQ = DEC_SEQ // TOK_TILE


def _tile_cond(i):
    return jnp.where(i < _CTX_TILES, 0, 1 + (i - _CTX_TILES) // _LAT_TILES_PER_SEQ)


def _tile_rope_block(i):
    return jnp.where(i < _CTX_TILES, _LAT_TILES_PER_SEQ, (i - _CTX_TILES) % _LAT_TILES_PER_SEQ)


def _input_projection(l, x, mod, norm1_w, w_in, cos_t, sin_t, qn_w, kn_w, bd):
    row = lambda w: pl.BlockSpec((TOK_TILE, w), lambda i: (i, 0))
    return pl.pallas_call(
        _inproj_kernel,
        grid=(N_TOK // TOK_TILE,),
        in_specs=[
            row(D_MODEL),
            pl.BlockSpec((None, None, N_MOD, D_MODEL), lambda i: (l, _tile_cond(i), 0, 0)),
            pl.BlockSpec((None, 1, D_MODEL), lambda i: (l, 0, 0)),
            pl.BlockSpec((None, D_MODEL, IN_COLS), lambda i: (l, 0, 0),
                         pipeline_mode=pl.Buffered(1)),
            pl.BlockSpec((TOK_TILE, ATTN_WIDTH), lambda i: (_tile_rope_block(i), 0)),
            pl.BlockSpec((TOK_TILE, ATTN_WIDTH), lambda i: (_tile_rope_block(i), 0)),
            pl.BlockSpec((None, 1, ATTN_WIDTH), lambda i: (l, 0, 0)),
            pl.BlockSpec((None, 1, KV_WIDTH), lambda i: (l, 0, 0)),
            pl.BlockSpec((ATTN_WIDTH, ATTN_WIDTH), lambda i: (0, 0)),
        ],
        out_specs=[row(ATTN_WIDTH), row(KV_WIDTH), row(KV_WIDTH), row(5 * HGRN_WIDTH),
                   row(2 * D_MODEL)],
        out_shape=[
            jax.ShapeDtypeStruct((N_TOK, ATTN_WIDTH), BF16),
            jax.ShapeDtypeStruct((N_TOK, KV_WIDTH), F32),
            jax.ShapeDtypeStruct((N_TOK, KV_WIDTH), F32),
            jax.ShapeDtypeStruct((N_TOK, 5 * HGRN_WIDTH), F32),
            jax.ShapeDtypeStruct((N_TOK, 2 * D_MODEL), F32),
        ],
        compiler_params=_params("arbitrary"),
        name="input_projection",
    )(x, mod, norm1_w, w_in, cos_t, sin_t, qn_w, kn_w, bd)


def _softmax_pv(scores, values):
    m = scores[0].max(axis=-1, keepdims=True)
    for s in scores[1:]:
        m = jnp.maximum(m, s.max(axis=-1, keepdims=True))
    num, den = None, None
    for s, v in zip(scores, values):
        p = jnp.exp(s - m)
        d = p.sum(axis=-1, keepdims=True)
        o = _dot(p.astype(BF16), v)
        num = o if num is None else num + o
        den = d if den is None else den + d
    return num / den


def _attn_ctx_kernel(q_ref, k_ref, v_ref, o_ref):
    for g in range(N_KV_HEADS):
        cols = slice(g * HEAD_DIM, (g + 1) * HEAD_DIM)
        kg = k_ref[:, cols].astype(BF16)
        vg = v_ref[:, cols].astype(BF16)
        for hh in range(GQA_GROUP):
            h = g * GQA_GROUP + hh
            hc = slice(h * HEAD_DIM, (h + 1) * HEAD_DIM)
            o = _softmax_pv([_dot_nt(q_ref[:, hc], kg)], [vg])
            o_ref[:, hc] = o.astype(BF16)


def _attention_ctx(q, k, v):
    return pl.pallas_call(
        _attn_ctx_kernel,
        grid=(BATCH,),
        in_specs=[
            pl.BlockSpec((SEQ, ATTN_WIDTH), lambda b: (b, 0)),
            pl.BlockSpec((SEQ, KV_WIDTH), lambda b: (b, 0)),
            pl.BlockSpec((SEQ, KV_WIDTH), lambda b: (b, 0)),
        ],
        out_specs=pl.BlockSpec((SEQ, ATTN_WIDTH), lambda b: (b, 0)),
        out_shape=jax.ShapeDtypeStruct((N_CTX, ATTN_WIDTH), BF16),
        compiler_params=_params("arbitrary"),
        name="attention_ctx",
    )(q, k, v)


def _attn_lat_kernel(q_ref, k_ref, v_ref, ck_ref, cv_ref, o_ref):
    for g in range(N_KV_HEADS):
        cols = slice(g * HEAD_DIM, (g + 1) * HEAD_DIM)
        kg = k_ref[:, cols].astype(BF16)
        vg = v_ref[:, cols].astype(BF16)
        ckg = ck_ref[g].astype(BF16)
        cvg = cv_ref[g].astype(BF16)
        for hh in range(GQA_GROUP):
            h = g * GQA_GROUP + hh
            hc = slice(h * HEAD_DIM, (h + 1) * HEAD_DIM)
            qh = q_ref[:, hc]
            o = _softmax_pv([_dot_nt(qh, kg), _dot_nt(qh, ckg)], [vg, cvg])
            o_ref[:, hc] = o.astype(BF16)


def _attention_lat(l, q, k, v, cache_k, cache_v):
    q_blocks = DEC_SEQ // Q_TILE
    q_off = N_CTX // Q_TILE
    kv_off = N_CTX // DEC_SEQ
    cache_spec = pl.BlockSpec((None, None, N_KV_HEADS, PAST_LEN, HEAD_DIM),
                              lambda b, i: (b, l, 0, 0, 0))
    return pl.pallas_call(
        _attn_lat_kernel,
        grid=(DEC_BATCH, q_blocks),
        in_specs=[
            pl.BlockSpec((Q_TILE, ATTN_WIDTH), lambda b, i: (q_off + b * q_blocks + i, 0)),
            pl.BlockSpec((DEC_SEQ, KV_WIDTH), lambda b, i: (kv_off + b, 0)),
            pl.BlockSpec((DEC_SEQ, KV_WIDTH), lambda b, i: (kv_off + b, 0)),
            cache_spec,
            cache_spec,
        ],
        out_specs=pl.BlockSpec((Q_TILE, ATTN_WIDTH), lambda b, i: (b * q_blocks + i, 0)),
        out_shape=jax.ShapeDtypeStruct((N_LAT, ATTN_WIDTH), BF16),
        compiler_params=_params("arbitrary", "arbitrary"),
        name="attention_lat",
    )(q, k, v, cache_k, cache_v)


def _lower_bound(lb_ref, l):
    z = lb_ref[...]
    e = jnp.exp(z - z.max(axis=0, keepdims=True))
    p = e / e.sum(axis=0, keepdims=True)
    acc = p[0:1] * 0.0
    for r in range(1, l + 1):
        acc = acc + p[r:r + 1]
    return acc


def _hgrn_tables(rev):
    t = np.arange(CHUNK)
    tt, uu = t[:, None], t[None, :]
    groups = []
    masks = []
    for h in HGRN_LEVELS:
        right = (t & h) != 0
        edge = ((t // (2 * h)) * 2 * h + h - 1)[:, None]
        groups.append(np.where(right[:, None], (uu > edge) & (uu <= tt), (uu > tt) & (uu <= edge)))
        same = (tt // (2 * h)) == (uu // (2 * h))
        masks.append(same & right[:, None] & ~right[None, :])
    masks.append(tt == uu)
    groups += [uu <= tt, uu > tt]
    sums = np.stack(groups).astype(np.float32)
    masks = np.stack(masks).astype(np.float32)
    if rev:
        sums = sums[:, ::-1, ::-1]
        masks = masks[:, ::-1, ::-1]
    sums = sums.reshape(-1, CHUNK)
    return (jnp.asarray(np.concatenate([sums, sums], axis=1), dtype=BF16),
            jnp.asarray(np.ascontiguousarray(masks)))


def _hgrn_block(q, x, v, lb, sums_ref, masks_ref, st_ref, rev):
    n = q.shape[0] // CHUNK
    f = lb + (1.0 - lb) * _sigmoid(x)
    kk = 1.0 - f
    g = jnp.log(f) * LOG2E
    g_hi, g_lo = _split_bf16(g)
    vb = v.astype(BF16)
    rid = lax.broadcasted_iota(jnp.int32, (CHUNK, HGRN_K), 0)
    edge = 0 if rev else CHUNK - 1
    n_lev = len(HGRN_LEVELS)
    outs = [None] * n
    for c in (reversed(range(n)) if rev else range(n)):
        rs = slice(c * CHUNK, (c + 1) * CHUNK)
        e = jnp.exp2(_dot(sums_ref[...], jnp.concatenate([g_hi[rs], g_lo[rs]], axis=0)))
        q_c, k_c, v_c = q[rs], kk[rs], vb[rs]
        a = _dot_nt(q_c.astype(BF16), k_c.astype(BF16)) * masks_ref[n_lev]
        for i, h in enumerate(HGRN_LEVELS):
            later = ((rid & h) == 0) if rev else ((rid & h) != 0)
            z = (jnp.where(later, q_c, k_c) * e[i * CHUNK:(i + 1) * CHUNK]).astype(BF16)
            a = a + _dot_nt(z, z) * masks_ref[i]
        e_in = e[n_lev * CHUNK:(n_lev + 1) * CHUNK]
        e_out = e[(n_lev + 1) * CHUNK:(n_lev + 2) * CHUNK]
        st = st_ref[...]
        outs[c] = (_dot(a.astype(BF16), v_c)
                   + _dot_nt((q_c * e_in).astype(BF16), st.astype(BF16)))
        k_dec = (k_c * e_out).astype(BF16)
        st_ref[...] = st * e_in[edge:edge + 1] + _dot_tn(v_c, k_dec)
    return jnp.concatenate(outs, axis=0)


def _hgrn_kernel(l, seq_len, has_init, *refs):
    (hq_ref, ff_ref, fb_ref, hi_ref, hg_ref, lbf_ref, lbb_ref, hn_ref,
     sumf_ref, sumb_ref, mskf_ref, mskb_ref) = refs[:12]
    refs = refs[12:]
    if has_init:
        s0f_ref, s0b_ref, rec_ref, of_ref, ob_ref, stf_ref, stb_ref = refs
    else:
        rec_ref, sf_ref, sb_ref, of_ref, ob_ref, stf_ref, stb_ref = refs
    lb_f = _lower_bound(lbf_ref, l)
    lb_b = _lower_bound(lbb_ref, l)
    if has_init:
        stf_ref[...] = s0f_ref[...].T
        stb_ref[...] = s0b_ref[...].T
    else:
        stf_ref[...] = jnp.zeros((HGRN_V, HGRN_K), F32)
        stb_ref[...] = jnp.zeros((HGRN_V, HGRN_K), F32)
    rows = BLOCK_CHUNKS * CHUNK
    n_blocks = seq_len // rows

    def step(rf, rb):
        of_ref[rf, :] = _hgrn_block(hq_ref[rf, :], ff_ref[rf, :], hi_ref[rf, :], lb_f,
                                    sumf_ref, mskf_ref, stf_ref, False)
        ob_ref[rb, :] = _hgrn_block(hq_ref[rb, :], fb_ref[rb, :], hi_ref[rb, :], lb_b,
                                    sumb_ref, mskb_ref, stb_ref, True)

    if n_blocks == 1:
        step(pl.ds(0, rows), pl.ds(0, rows))
    else:
        def body(i, carry):
            step(pl.ds(pl.multiple_of(i * rows, rows), rows),
                 pl.ds(pl.multiple_of((n_blocks - 1 - i) * rows, rows), rows))
            return carry

        lax.fori_loop(0, n_blocks, body, 0)
    o = of_ref[...] + ob_ref[...]
    o = o * lax.rsqrt(jnp.mean(o * o, axis=-1, keepdims=True) + EPS) * hn_ref[...]
    rec_ref[...] = o * hg_ref[...]
    if not has_init:
        sf_ref[...] = stf_ref[...].T
        sb_ref[...] = stb_ref[...].T


def _hgrn(l, h5, row_off, n_seq, seq_len, lb_fwd, lb_bwd, hnorm_w, tables, init=None):
    blocks_off = row_off // seq_len
    part = lambda k: pl.BlockSpec((seq_len, HGRN_K),
                                  lambda b, h: (blocks_off + b, k * HGRN_HEADS + h))
    lb_spec = pl.BlockSpec((DEPTH, HGRN_K), lambda b, h: (0, h))
    sum_spec = pl.BlockSpec((8 * CHUNK, 2 * CHUNK), lambda b, h: (0, 0))
    msk_spec = pl.BlockSpec((len(HGRN_LEVELS) + 1, CHUNK, CHUNK), lambda b, h: (0, 0, 0))
    in_specs = [part(0), part(1), part(2), part(3), part(4), lb_spec, lb_spec,
                pl.BlockSpec((None, 1, HGRN_V), lambda b, h: (l, 0, 0)),
                sum_spec, sum_spec, msk_spec, msk_spec]
    args = [h5, h5, h5, h5, h5, lb_fwd, lb_bwd, hnorm_w, *tables]
    rec_spec = pl.BlockSpec((seq_len, HGRN_V), lambda b, h: (b, h))
    rec_shape = jax.ShapeDtypeStruct((n_seq * seq_len, HGRN_WIDTH), F32)
    if init is not None:
        st_spec = pl.BlockSpec((None, None, None, HGRN_K, HGRN_V), lambda b, h: (b, l, h, 0, 0))
        in_specs += [st_spec, st_spec]
        args += list(init)
        out_specs, out_shape = rec_spec, rec_shape
    else:
        st_spec = pl.BlockSpec((None, None, HGRN_K, HGRN_V), lambda b, h: (b, h, 0, 0))
        st_shape = jax.ShapeDtypeStruct((n_seq, HGRN_HEADS, HGRN_K, HGRN_V), F32)
        out_specs, out_shape = [rec_spec, st_spec, st_spec], [rec_shape, st_shape, st_shape]
    return pl.pallas_call(
        functools.partial(_hgrn_kernel, l, seq_len, init is not None),
        grid=(n_seq, HGRN_HEADS),
        in_specs=in_specs,
        out_specs=out_specs,
        out_shape=out_shape,
        scratch_shapes=[pltpu.VMEM((seq_len, HGRN_V), F32), pltpu.VMEM((seq_len, HGRN_V), F32),
                        pltpu.VMEM((HGRN_V, HGRN_K), F32), pltpu.VMEM((HGRN_V, HGRN_K), F32)],
        compiler_params=_params("arbitrary", "arbitrary"),
        name="hgrn_lat" if init is not None else "hgrn_ctx",
    )(*args)


def _route(scores, bias, sel_member, sel_group):
    sel = scores + bias
    members = [_dot_exact(sel, sel_member[k]) for k in range(EXPERTS_PER_GROUP)]
    pair = None
    for i in range(EXPERTS_PER_GROUP):
        for j in range(i + 1, EXPERTS_PER_GROUP):
            s = members[i] + members[j]
            pair = s if pair is None else jnp.maximum(pair, s)
    lane = lax.broadcasted_iota(jnp.int32, sel.shape, 1)
    best = _dot_exact(pair, sel_group[0])
    best_g = jnp.zeros(sel.shape, jnp.int32)
    for m in range(1, N_GROUPS):
        cand = _dot_exact(pair, sel_group[m])
        upd = cand > best
        best_g = jnp.where(upd, m, best_g)
        best = jnp.where(upd, cand, best)
    in_group = (lane >> 2) == best_g
    pos = lane & (EXPERTS_PER_GROUP - 1)
    rank = jnp.zeros(sel.shape, jnp.int32)
    for k in range(EXPERTS_PER_GROUP):
        ahead = (members[k] > sel) | ((members[k] == sel) & (k < pos))
        rank = rank + jnp.where(ahead, 1, 0)
    w = jnp.where(in_group & (rank < 2), scores, 0.0)
    return w / w.sum(axis=-1, keepdims=True)


def _merge_kernel(attn_ref, rec_ref, g_ref, x_ref, mod_ref, n2_ref, wa_ref, wh_ref, wo_ref,
                  wr_ref, br_ref, selm_ref, selg_ref, pad_ref, x1_ref, slab_ref, route_ref):
    ya = _dot(attn_ref[...], wa_ref[...])
    yh = _dot(rec_ref[...].astype(BF16), wh_ref[...])
    merged = g_ref[:, 0:D_MODEL] * ya + g_ref[:, D_MODEL:2 * D_MODEL] * yh
    out = _dot(merged.astype(BF16), wo_ref[...])
    x1 = x_ref[...] + mod_ref[2:3, :] * out
    x1_ref[...] = x1
    xn = x1 * lax.rsqrt(jnp.mean(x1 * x1, axis=-1, keepdims=True) + EPS) * n2_ref[...]
    h2 = xn * (1.0 + mod_ref[4:5, :]) + mod_ref[3:4, :]
    h_hi, h_lo = _split_bf16(h2)
    w_hi, w_lo = _split_bf16(wr_ref[...])
    scores = _sigmoid(_dot(jnp.concatenate([h_hi, h_lo, h_hi], axis=1),
                           jnp.concatenate([w_hi, w_hi, w_lo], axis=0)))
    gates = _route(scores, br_ref[...], [selm_ref[k] for k in range(EXPERTS_PER_GROUP)],
                   [selg_ref[m] for m in range(N_GROUPS)])
    gates = _dot_exact(gates, pad_ref[...])
    route_ref[...] = gates
    for s in range(FEAT_ROWS):
        slab_ref[:, s, :] = h2[:, s * LANES:(s + 1) * LANES]
    slab_ref[:, FEAT_ROWS, :] = gates
    for s in range(FEAT_ROWS + 1, SLAB_ROWS):
        slab_ref[:, s, :] = jnp.zeros_like(gates)


def _merge(l, attn, rec, gates, x, mod, norm2_w, w_br_attn, w_br_hgrn, w_out, w_router, b_router,
           sel_member, sel_group, lane_pad):
    row = lambda w: pl.BlockSpec((TOK_TILE, w), lambda i: (i, 0))
    layer = lambda r, c: pl.BlockSpec((None, r, c), lambda i: (l, 0, 0))
    full = lambda *s: pl.BlockSpec(s, lambda i: (0,) * len(s))
    return pl.pallas_call(
        _merge_kernel,
        grid=(N_TOK // TOK_TILE,),
        in_specs=[
            row(ATTN_WIDTH), row(HGRN_WIDTH), row(2 * D_MODEL), row(D_MODEL),
            pl.BlockSpec((None, None, N_MOD, D_MODEL), lambda i: (l, _tile_cond(i), 0, 0)),
            layer(1, D_MODEL),
            layer(ATTN_WIDTH, D_MODEL), layer(HGRN_WIDTH, D_MODEL), layer(D_MODEL, D_MODEL),
            full(D_MODEL, N_EXPERTS), full(1, N_EXPERTS),
            full(EXPERTS_PER_GROUP, N_EXPERTS, N_EXPERTS), full(N_GROUPS, N_EXPERTS, N_EXPERTS),
            full(N_EXPERTS, LANES),
        ],
        out_specs=[row(D_MODEL),
                   pl.BlockSpec((TOK_TILE, SLAB_ROWS, LANES), lambda i: (i, 0, 0)),
                   row(LANES)],
        out_shape=[
            jax.ShapeDtypeStruct((N_TOK, D_MODEL), F32),
            jax.ShapeDtypeStruct((N_TOK, SLAB_ROWS, LANES), F32),
            jax.ShapeDtypeStruct((N_TOK, LANES), F32),
        ],
        compiler_params=_params("arbitrary"),
        name="merge_router",
    )(attn, rec, gates, x, mod, norm2_w, w_br_attn, w_br_hgrn, w_out, w_router, b_router,
      sel_member, sel_group, lane_pad)


def _plan_kernel(route_ref, dst_ref, meta_ref, tot_ref, run_ref):
    p, i = pl.program_id(0), pl.program_id(1)

    @pl.when(jnp.logical_and(p == 0, i == 0))
    def _():
        tot_ref[...] = jnp.zeros_like(tot_ref)

    @pl.when(i == 0)
    def _():
        run_ref[...] = jnp.zeros_like(run_ref)

    grp = lax.broadcasted_iota(jnp.int32, (SUBLANES, LANES), 0)
    lane = lax.broadcasted_iota(jnp.int32, (SUBLANES, LANES), 1)
    member = jnp.where(((lane >> 2) == grp) & (lane < N_EXPERTS), 1.0, 0.0)
    one_hot = jnp.where(_dot_nt(member, route_ref[...], precision=_HI) > 0.0, 1.0, 0.0)
    src = lax.broadcasted_iota(jnp.int32, (PLAN_TILE, PLAN_TILE), 0)
    tgt = lax.broadcasted_iota(jnp.int32, (PLAN_TILE, PLAN_TILE), 1)
    before = jnp.where(src < tgt, 1.0, 0.0).astype(BF16)
    rank = _dot(one_hot.astype(BF16), before) + run_ref[:, 0:1]
    tile_tot = one_hot.sum(axis=1, keepdims=True)

    @pl.when(p == 0)
    def _():
        tot_ref[...] += tile_tot

    @pl.when(p == 1)
    def _():
        run_ref[...] += tile_tot
        tot = tot_ref[:, 0:1]
        n_tiles = jnp.floor((tot + (SORT_TILE - 1)) * (1.0 / SORT_TILE))
        offs, acc = [], jnp.zeros((1, 1), F32)
        for m in range(N_GROUPS):
            offs.append(acc * SORT_TILE)
            acc = acc + n_tiles[m:m + 1]
        dst = sum(one_hot[m:m + 1] * (offs[m] + rank[m:m + 1]) for m in range(N_GROUPS))
        for r in range(PLAN_TILE // LANES):
            dst_ref[r:r + 1, :] = dst[:, r * LANES:(r + 1) * LANES].astype(jnp.int32)
        tile_row = (lax.broadcasted_iota(jnp.int32, (1, LANES), 1) * SORT_TILE).astype(F32)
        tile_group = sum(jnp.where(tile_row >= offs[m], 1.0, 0.0) for m in range(1, N_GROUPS))
        rows = [tile_group, jnp.broadcast_to(acc, (1, LANES))]
        rows += [jnp.broadcast_to(o, (1, LANES)) for o in offs]
        rows += [jnp.broadcast_to(tot[m:m + 1], (1, LANES)) for m in range(N_GROUPS)]
        rows += [jnp.zeros((1, LANES), F32)] * (2 * SUBLANES - len(rows))
        meta_ref[...] = jnp.concatenate(rows, axis=0).astype(jnp.int32)


def _plan(route):
    n_blocks = N_TOK // PLAN_TILE
    dst, meta = pl.pallas_call(
        _plan_kernel,
        grid=(2, n_blocks),
        in_specs=[pl.BlockSpec((PLAN_TILE, LANES), lambda p, i: (i, 0))],
        out_specs=[pl.BlockSpec((PLAN_TILE // LANES, LANES), lambda p, i: (i * p, 0)),
                   pl.BlockSpec((2 * SUBLANES, LANES), lambda p, i: (0, 0))],
        out_shape=[jax.ShapeDtypeStruct((N_TOK // LANES, LANES), jnp.int32),
                   jax.ShapeDtypeStruct((2 * SUBLANES, LANES), jnp.int32)],
        scratch_shapes=[pltpu.VMEM((SUBLANES, LANES), F32), pltpu.VMEM((SUBLANES, LANES), F32)],
        compiler_params=_params("arbitrary", "arbitrary"),
        name="route_plan",
    )(route)
    return (dst, meta[0, :N_SORT_TILES], meta[1, 0:1], meta[2:2 + N_GROUPS, 0],
            meta[2 + N_GROUPS:2 + 2 * N_GROUPS, 0])


def _start_rows(make_copy):
    def start(c, carry):
        make_copy(c).start()
        return carry

    lax.fori_loop(0, LANES, start, 0, unroll=8)


def _dispatch_kernel(seg_ref, cnt_ref, dst_ref, slab_ref, zero_ref, sorted_ref, sem):
    i = pl.program_id(0)

    @pl.when(i == 0)
    def _():
        def zero_tile(t):
            cp = pltpu.make_async_copy(
                zero_ref, sorted_ref.at[pl.ds(pl.multiple_of(t * SORT_TILE, SORT_TILE), SORT_TILE)], sem)
            cp.start()
            cp.wait()

        for m in range(N_GROUPS):
            end = seg_ref[m] + cnt_ref[m]

            @pl.when((end & (SORT_TILE - 1)) != 0)
            def _():
                zero_tile(lax.shift_right_logical(end, SORT_TILE.bit_length() - 1))

        first_free = lax.shift_right_logical(end + (SORT_TILE - 1), SORT_TILE.bit_length() - 1)
        lax.fori_loop(first_free, N_SORT_TILES, lambda t, c: (zero_tile(t), c)[1], 0)

    groups = range(PLAN_TILE // LANES)
    for r in groups:
        _start_rows(lambda c, r=r: pltpu.make_async_copy(
            slab_ref.at[r * LANES + c], sorted_ref.at[dst_ref[r, c]], sem))
    for r in groups:
        pltpu.make_async_copy(slab_ref.at[pl.ds(r * LANES, LANES)],
                              sorted_ref.at[pl.ds(0, LANES)], sem).wait()


def _dispatch(dst, seg, cnt, slab):
    zero = jnp.zeros((SORT_TILE, SLAB_ROWS, LANES), F32)
    return pl.pallas_call(
        _dispatch_kernel,
        grid_spec=pltpu.PrefetchScalarGridSpec(
            num_scalar_prefetch=2,
            grid=(N_TOK // PLAN_TILE,),
            in_specs=[
                pl.BlockSpec((PLAN_TILE // LANES, LANES), lambda i, seg, cnt: (i, 0),
                             memory_space=pltpu.SMEM),
                pl.BlockSpec((PLAN_TILE, SLAB_ROWS, LANES), lambda i, seg, cnt: (i, 0, 0)),
                pl.BlockSpec(memory_space=pl.ANY),
            ],
            out_specs=pl.BlockSpec(memory_space=pl.ANY),
            scratch_shapes=[pltpu.SemaphoreType.DMA(())],
        ),
        out_shape=jax.ShapeDtypeStruct((N_SORT_TILES * SORT_TILE, SLAB_ROWS, LANES), F32),
        compiler_params=_params("arbitrary"),
        name="dispatch",
    )(seg, cnt, dst, slab, zero)


def _experts_kernel(tg_ref, nv_ref, x_ref, wg_ref, wu_ref, wd_ref, y_ref):
    j = pl.program_id(0)

    @pl.when(j < nv_ref[0])
    def _():
        first_expert = tg_ref[j] * EXPERTS_PER_GROUP
        h = jnp.concatenate([x_ref[:, s, :] for s in range(FEAT_ROWS)], axis=1).astype(BF16)
        gates = x_ref[:, FEAT_ROWS, :]
        lane = lax.broadcasted_iota(jnp.int32, gates.shape, 1)
        acc = None
        for k in range(EXPERTS_PER_GROUP):
            a = _dot(h, wg_ref[k])
            u = _dot(h, wu_ref[k])
            ge = jnp.sum(jnp.where(lane == first_expert + k, gates, 0.0), axis=-1, keepdims=True)
            y = _dot((a * _sigmoid(a) * u * ge).astype(BF16), wd_ref[k])
            acc = y if acc is None else acc + y
        for s in range(FEAT_ROWS):
            y_ref[:, s, :] = acc[:, s * LANES:(s + 1) * LANES]

    @pl.when(j >= nv_ref[0])
    def _():
        y_ref[...] = jnp.zeros_like(y_ref)


def _experts(l, tile_group, n_used, sorted_rows, w_gate, w_up, w_down):
    tile = lambda j, tg, nv: jnp.minimum(j, nv[0] - 1)
    w_spec = lambda r, c: pl.BlockSpec((None, None, EXPERTS_PER_GROUP, r, c),
                                       lambda j, tg, nv: (l, tg[tile(j, tg, nv)], 0, 0, 0))
    return pl.pallas_call(
        _experts_kernel,
        grid_spec=pltpu.PrefetchScalarGridSpec(
            num_scalar_prefetch=2,
            grid=(N_SORT_TILES,),
            in_specs=[
                pl.BlockSpec((SORT_TILE, SLAB_ROWS, LANES),
                             lambda j, tg, nv: (tile(j, tg, nv), 0, 0)),
                w_spec(D_MODEL, D_FF), w_spec(D_MODEL, D_FF), w_spec(D_FF, D_MODEL),
            ],
            out_specs=pl.BlockSpec((SORT_TILE, FEAT_ROWS, LANES), lambda j, tg, nv: (j, 0, 0)),
        ),
        out_shape=jax.ShapeDtypeStruct((N_SORT_TILES * SORT_TILE, FEAT_ROWS, LANES), F32),
        compiler_params=_params("arbitrary"),
        name="experts",
    )(tile_group, n_used, sorted_rows, w_gate, w_up, w_down)


def _combine_kernel(dst_ref, y_ref, x1_ref, mod_ref, o_ref, buf_ref, sem):
    groups = range(PLAN_TILE // LANES)
    for r in groups:
        _start_rows(lambda c, r=r: pltpu.make_async_copy(
            y_ref.at[dst_ref[r, c]], buf_ref.at[r * LANES + c], sem))
    for r in groups:
        pltpu.make_async_copy(y_ref.at[pl.ds(0, LANES)],
                              buf_ref.at[pl.ds(r * LANES, LANES)], sem).wait()
    for s in range(FEAT_ROWS):
        cols = slice(s * LANES, (s + 1) * LANES)
        o_ref[:, cols] = x1_ref[:, cols] + mod_ref[5:6, cols] * buf_ref[:, s, :]


_PLAN_CTX_TILES = N_CTX // PLAN_TILE
_PLAN_LAT_TILES_PER_SEQ = DEC_SEQ // PLAN_TILE


def _plan_cond(i):
    return jnp.where(i < _PLAN_CTX_TILES, 0, 1 + (i - _PLAN_CTX_TILES) // _PLAN_LAT_TILES_PER_SEQ)


def _combine(l, dst, expert_out, x1, mod):
    return pl.pallas_call(
        _combine_kernel,
        grid=(N_TOK // PLAN_TILE,),
        in_specs=[
            pl.BlockSpec((PLAN_TILE // LANES, LANES), lambda i: (i, 0), memory_space=pltpu.SMEM),
            pl.BlockSpec(memory_space=pl.ANY),
            pl.BlockSpec((PLAN_TILE, D_MODEL), lambda i: (i, 0)),
            pl.BlockSpec((None, None, N_MOD, D_MODEL), lambda i: (l, _plan_cond(i), 0, 0)),
        ],
        out_specs=pl.BlockSpec((PLAN_TILE, D_MODEL), lambda i: (i, 0)),
        out_shape=jax.ShapeDtypeStruct((N_TOK, D_MODEL), F32),
        scratch_shapes=[pltpu.VMEM((PLAN_TILE, FEAT_ROWS, LANES), F32), pltpu.SemaphoreType.DMA(())],
        compiler_params=_params("arbitrary"),
        name="combine",
    )(dst, expert_out, x1, mod)


def _rope_tables():
    pos = np.arange(DEC_SEQ)
    inv_freq = ROPE_THETA ** (-np.arange(0, AXIS_DIM, 2, dtype=np.float32) / AXIS_DIM)
    ang_r = (pos // GRID_W).astype(np.float32)[:, None] * inv_freq[None, :]
    ang_c = (pos % GRID_W).astype(np.float32)[:, None] * inv_freq[None, :]
    ang = jnp.asarray(np.concatenate([ang_r, ang_r, ang_c, ang_c], axis=-1).astype(np.float32))
    sign = np.where(np.arange(HEAD_DIM) % AXIS_DIM < AXIS_DIM // 2, -1.0, 1.0).astype(np.float32)
    cos = jnp.concatenate([jnp.cos(ang), jnp.ones((TOK_TILE, HEAD_DIM), F32)], axis=0)
    sin = jnp.concatenate([jnp.sin(ang) * sign, jnp.zeros((TOK_TILE, HEAD_DIM), F32)], axis=0)
    return jnp.tile(cos, (1, N_Q_HEADS)), jnp.tile(sin, (1, N_Q_HEADS))


def _selectors():
    lane = np.arange(N_EXPERTS)
    member = np.stack([(lane[:, None] == (lane[None, :] // EXPERTS_PER_GROUP) * EXPERTS_PER_GROUP + k)
                       for k in range(EXPERTS_PER_GROUP)]).astype(np.float32)
    group = np.stack([np.broadcast_to(lane[:, None] == m * EXPERTS_PER_GROUP, (N_EXPERTS, N_EXPERTS))
                      for m in range(N_GROUPS)]).astype(np.float32)
    head = (np.arange(ATTN_WIDTH)[:, None] // HEAD_DIM == np.arange(ATTN_WIDTH)[None, :] // HEAD_DIM)
    return jnp.asarray(member), jnp.asarray(group), jnp.asarray(head.astype(np.float32))


def kernel(x_prompt, x_sample, cache_k, cache_v, state_fwd, state_bwd, c, c_ctx, norm1_w, norm2_w, w_mod, b_mod, w_in, q_norm_w, k_norm_w, hgrn_lb_fwd, hgrn_lb_bwd, hgrn_norm_w, w_br_attn, w_br_hgrn, w_out, w_router, b_router, w_exp_gate, w_exp_up, w_exp_down):
    cos_t, sin_t = _rope_tables()
    sel_member, sel_group, head_ones = _selectors()
    sums_f, masks_f = _hgrn_tables(False)
    sums_b, masks_b = _hgrn_tables(True)
    hgrn_tables = (sums_f, sums_b, masks_f, masks_b)
    cond = jnp.concatenate([c_ctx[None, :], c, jnp.zeros((N_COND - 1 - DEC_BATCH, D_MODEL), F32)], axis=0)
    mod = _modulation(cond, w_mod, b_mod)

    w_in_b = w_in.astype(BF16)
    w_ba_b = w_br_attn.astype(BF16)
    w_bh_b = w_br_hgrn.astype(BF16)
    w_out_b = w_out.astype(BF16)
    by_group = lambda w: w.astype(BF16).reshape(DEPTH, N_GROUPS, EXPERTS_PER_GROUP, *w.shape[2:])
    w_eg_b, w_eu_b, w_ed_b = by_group(w_exp_gate), by_group(w_exp_up), by_group(w_exp_down)
    lane_pad = jnp.eye(N_EXPERTS, LANES, dtype=F32)
    b_router_r = b_router.reshape(1, N_EXPERTS)
    norm1_r = norm1_w.reshape(DEPTH, 1, D_MODEL)
    norm2_r = norm2_w.reshape(DEPTH, 1, D_MODEL)
    hnorm_r = hgrn_norm_w.reshape(DEPTH, 1, HGRN_V)
    qn_r = jnp.tile(q_norm_w, (1, N_Q_HEADS)).reshape(DEPTH, 1, ATTN_WIDTH)
    kn_r = jnp.tile(k_norm_w, (1, N_KV_HEADS)).reshape(DEPTH, 1, KV_WIDTH)

    x = jnp.concatenate([x_prompt.reshape(N_CTX, D_MODEL), x_sample.reshape(N_LAT, D_MODEL)], axis=0)
    ks_out, vs_out, sf_out, sb_out = [], [], [], []
    for l in range(DEPTH):
        q, k, v, h5, gates = _input_projection(l, x, mod, norm1_r, w_in_b, cos_t, sin_t, qn_r, kn_r,
                                               head_ones)
        attn_c = _attention_ctx(q, k, v)
        attn_s = _attention_lat(l, q, k, v, cache_k, cache_v)
        rec_c, sf, sb = _hgrn(l, h5, 0, BATCH, SEQ, hgrn_lb_fwd, hgrn_lb_bwd, hnorm_r, hgrn_tables)
        rec_s = _hgrn(l, h5, N_CTX, DEC_BATCH, DEC_SEQ, hgrn_lb_fwd, hgrn_lb_bwd, hnorm_r,
                      hgrn_tables, init=(state_fwd, state_bwd))
        attn = jnp.concatenate([attn_c, attn_s], axis=0)
        rec = jnp.concatenate([rec_c, rec_s], axis=0)
        x1, slab, route = _merge(l, attn, rec, gates, x, mod, norm2_r, w_ba_b, w_bh_b, w_out_b,
                                 w_router, b_router_r, sel_member, sel_group, lane_pad)
        dst, tile_group, n_used, seg, cnt = _plan(route)
        sorted_rows = _dispatch(dst, seg, cnt, slab)
        expert_out = _experts(l, tile_group, n_used, sorted_rows, w_eg_b, w_eu_b, w_ed_b)
        x = _combine(l, dst, expert_out, x1, mod)
        ks_out.append(k[:N_CTX].reshape(BATCH, SEQ, N_KV_HEADS, HEAD_DIM).transpose(0, 2, 1, 3))
        vs_out.append(v[:N_CTX].reshape(BATCH, SEQ, N_KV_HEADS, HEAD_DIM).transpose(0, 2, 1, 3))
        sf_out.append(sf)
        sb_out.append(sb)

    y_prompt = x[:N_CTX].reshape(BATCH, SEQ, D_MODEL)
    y_sample = x[N_CTX:].reshape(DEC_BATCH, DEC_SEQ, D_MODEL)
    return (y_prompt, y_sample, jnp.stack(ks_out, axis=1), jnp.stack(vs_out, axis=1),
            jnp.stack(sf_out, axis=1), jnp.stack(sb_out, axis=1))
```

```python
import functools

import numpy as np
import jax
import jax.numpy as jnp
from jax import lax
from jax.experimental import pallas as pl
from jax.experimental.pallas import tpu as pltpu

F32 = jnp.float32
BF16 = jnp.bfloat16

D_MODEL = 1024
BATCH = 16
SEQ = 256
DEPTH = 4
DEC_BATCH = 4
DEC_SEQ = 2048
PAST_LEN = 512
GRID_W = 64
N_Q_HEADS = 8
N_KV_HEADS = 2
GQA_GROUP = N_Q_HEADS // N_KV_HEADS
HEAD_DIM = 64
AXIS_DIM = HEAD_DIM // 2
ATTN_WIDTH = N_Q_HEADS * HEAD_DIM
KV_WIDTH = N_KV_HEADS * HEAD_DIM
ROPE_THETA = 10000.0
HGRN_HEADS = 4
HGRN_K = 128
HGRN_V = 128
HGRN_WIDTH = HGRN_HEADS * HGRN_K
N_EXPERTS = 16
N_GROUPS = 4
EXPERTS_PER_GROUP = N_EXPERTS // N_GROUPS
D_FF = 512
EPS = 1e-6
IN_COLS = ATTN_WIDTH + 2 * KV_WIDTH + 5 * HGRN_WIDTH + 2 * D_MODEL

N_CTX = BATCH * SEQ
N_LAT = DEC_BATCH * DEC_SEQ
N_TOK = N_CTX + N_LAT
N_COND = 8
N_MOD = 6

C_Q = 0
C_K = C_Q + ATTN_WIDTH
C_V = C_K + KV_WIDTH
C_HQ = C_V + KV_WIDTH
C_FF = C_HQ + HGRN_WIDTH
C_FB = C_FF + HGRN_WIDTH
C_HI = C_FB + HGRN_WIDTH
C_HG = C_HI + HGRN_WIDTH
C_GA = C_HG + HGRN_WIDTH
C_GH = C_GA + D_MODEL

LANES = 128
SUBLANES = 8
FEAT_ROWS = D_MODEL // LANES
SLAB_ROWS = 2 * FEAT_ROWS

TOK_TILE = 512
PLAN_TILE = 1024
SORT_TILE = 512
N_SORT_TILES = N_TOK // SORT_TILE + N_GROUPS
Q_TILE = 512
CHUNK = 64
HGRN_LEVELS = (32, 16, 8, 4, 2, 1)
BLOCK_CHUNKS = 4
LOG2E = 1.4426950408889634
VMEM_LIMIT = 56 * 1024 * 1024

_HI = lax.Precision.HIGHEST


def _sigmoid(x):
    return 1.0 / (1.0 + jnp.exp(-x))


def _dot(a, b):
    return jnp.dot(a, b, preferred_element_type=F32)


def _dot_nt(a, b, precision=None):
    return lax.dot_general(a, b, (((1,), (1,)), ((), ())), precision=precision,
                           preferred_element_type=F32)


def _dot_tn(a, b, precision=None):
    return lax.dot_general(a, b, (((0,), (0,)), ((), ())), precision=precision,
                           preferred_element_type=F32)


def _dot_exact(a, b):
    return jnp.dot(a, b, precision=_HI, preferred_element_type=F32)


def _split_bf16(a):
    hi = a.astype(BF16)
    return hi, (a - hi.astype(F32)).astype(BF16)


def _params(*sem):
    return pltpu.CompilerParams(dimension_semantics=sem, vmem_limit_bytes=VMEM_LIMIT)


def _mod_kernel(cond_ref, w_ref, b_ref, out_ref):
    c = cond_ref[...]
    sc = c * _sigmoid(c)
    out_ref[...] = _dot(sc.astype(BF16), w_ref[...].astype(BF16)) + b_ref[...]


def _modulation(cond, w_mod, b_mod):
    out = pl.pallas_call(
        _mod_kernel,
        grid=(DEPTH, N_MOD),
        in_specs=[
            pl.BlockSpec((N_COND, D_MODEL), lambda l, j: (0, 0)),
            pl.BlockSpec((None, D_MODEL, D_MODEL), lambda l, j: (l, 0, j)),
            pl.BlockSpec((None, 1, D_MODEL), lambda l, j: (l, 0, j)),
        ],
        out_specs=pl.BlockSpec((None, None, N_COND, D_MODEL), lambda l, j: (l, j, 0, 0)),
        out_shape=jax.ShapeDtypeStruct((DEPTH, N_MOD, N_COND, D_MODEL), F32),
        compiler_params=_params("arbitrary", "arbitrary"),
        name="modulation",
    )(cond, w_mod, b_mod.reshape(DEPTH, 1, N_MOD * D_MODEL))
    return out.transpose(0, 2, 1, 3)


def _head_mean_sq(a, bd):
    return _dot_exact(a * a, bd) * (1.0 / HEAD_DIM)


def _rope(x, cos, sin_signed):
    width = x.shape[-1]
    lane = lax.broadcasted_iota(jnp.int32, x.shape, 1)
    first = (lane & (AXIS_DIM - 1)) < (AXIS_DIM // 2)
    rot = jnp.where(first, pltpu.roll(x, width - AXIS_DIM // 2, 1), pltpu.roll(x, AXIS_DIM // 2, 1))
    return x * cos + rot * sin_signed


def _inproj_kernel(x_ref, mod_ref, n1_ref, w_ref, cos_ref, sin_ref, qn_ref, kn_ref, bd_ref,
                   q_ref, k_ref, v_ref, h_ref, g_ref):
    x = x_ref[...]
    xn = x * lax.rsqrt(jnp.mean(x * x, axis=-1, keepdims=True) + EPS) * n1_ref[...]
    xb = (xn * (1.0 + mod_ref[1:2, :]) + mod_ref[0:1, :]).astype(BF16)

    def proj(c0, width):
        return _dot(xb, w_ref[:, c0:c0 + width])

    a = proj(C_Q, ATTN_WIDTH)
    qn = a * lax.rsqrt(_head_mean_sq(a, bd_ref[...]) + EPS) * qn_ref[...]
    q_ref[...] = (_rope(qn, cos_ref[...], sin_ref[...]) * (HEAD_DIM ** -0.5)).astype(BF16)

    a = proj(C_K, KV_WIDTH)
    kn = a * lax.rsqrt(_head_mean_sq(a, bd_ref[:KV_WIDTH, :KV_WIDTH]) + EPS) * kn_ref[...]
    k_ref[...] = _rope(kn, cos_ref[:, :KV_WIDTH], sin_ref[:, :KV_WIDTH])
    v_ref[...] = proj(C_V, KV_WIDTH)

    a = proj(C_HQ, HGRN_WIDTH)
    h_ref[:, 0:HGRN_WIDTH] = a * _sigmoid(a) * (HGRN_K ** -0.5)
    h_ref[:, HGRN_WIDTH:2 * HGRN_WIDTH] = proj(C_FF, HGRN_WIDTH)
    h_ref[:, 2 * HGRN_WIDTH:3 * HGRN_WIDTH] = proj(C_FB, HGRN_WIDTH)
    h_ref[:, 3 * HGRN_WIDTH:4 * HGRN_WIDTH] = proj(C_HI, HGRN_WIDTH)
    a = proj(C_HG, HGRN_WIDTH)
    h_ref[:, 4 * HGRN_WIDTH:5 * HGRN_WIDTH] = a * _sigmoid(a)

    g_ref[:, 0:D_MODEL] = _sigmoid(proj(C_GA, D_MODEL))
    g_ref[:, D_MODEL:2 * D_MODEL] = _sigmoid(proj(C_GH, D_MODEL))


_CTX_TILES = N_CTX // TOK_TILE
_LAT_TILES_PER_SEQ = DEC_SEQ // TOK_TILE


def _tile_cond(i):
    return jnp.where(i < _CTX_TILES, 0, 1 + (i - _CTX_TILES) // _LAT_TILES_PER_SEQ)


def _tile_rope_block(i):
    return jnp.where(i < _CTX_TILES, _LAT_TILES_PER_SEQ, (i - _CTX_TILES) % _LAT_TILES_PER_SEQ)


def _input_projection(l, x, mod, norm1_w, w_in, cos_t, sin_t, qn_w, kn_w, bd):
    row = lambda w: pl.BlockSpec((TOK_TILE, w), lambda i: (i, 0))
    return pl.pallas_call(
        _inproj_kernel,
        grid=(N_TOK // TOK_TILE,),
        in_specs=[
            row(D_MODEL),
            pl.BlockSpec((None, None, N_MOD, D_MODEL), lambda i: (l, _tile_cond(i), 0, 0)),
            pl.BlockSpec((None, 1, D_MODEL), lambda i: (l, 0, 0)),
            pl.BlockSpec((None, D_MODEL, IN_COLS), lambda i: (l, 0, 0),
                         pipeline_mode=pl.Buffered(1)),
            pl.BlockSpec((TOK_TILE, ATTN_WIDTH), lambda i: (_tile_rope_block(i), 0)),
            pl.BlockSpec((TOK_TILE, ATTN_WIDTH), lambda i: (_tile_rope_block(i), 0)),
            pl.BlockSpec((None, 1, ATTN_WIDTH), lambda i: (l, 0, 0)),
            pl.BlockSpec((None, 1, KV_WIDTH), lambda i: (l, 0, 0)),
            pl.BlockSpec((ATTN_WIDTH, ATTN_WIDTH), lambda i: (0, 0)),
        ],
        out_specs=[row(ATTN_WIDTH), row(KV_WIDTH), row(KV_WIDTH), row(5 * HGRN_WIDTH),
                   row(2 * D_MODEL)],
        out_shape=[
            jax.ShapeDtypeStruct((N_TOK, ATTN_WIDTH), BF16),
            jax.ShapeDtypeStruct((N_TOK, KV_WIDTH), F32),
            jax.ShapeDtypeStruct((N_TOK, KV_WIDTH), F32),
            jax.ShapeDtypeStruct((N_TOK, 5 * HGRN_WIDTH), F32),
            jax.ShapeDtypeStruct((N_TOK, 2 * D_MODEL), F32),
        ],
        compiler_params=_params("arbitrary"),
        name="input_projection",
    )(x, mod, norm1_w, w_in, cos_t, sin_t, qn_w, kn_w, bd)


def _softmax_pv(scores, values):
    m = scores[0].max(axis=-1, keepdims=True)
    for s in scores[1:]:
        m = jnp.maximum(m, s.max(axis=-1, keepdims=True))
    num, den = None, None
    for s, v in zip(scores, values):
        p = jnp.exp(s - m)
        d = p.sum(axis=-1, keepdims=True)
        o = _dot(p.astype(BF16), v)
        num = o if num is None else num + o
        den = d if den is None else den + d
    return num / den


def _attn_ctx_kernel(q_ref, k_ref, v_ref, o_ref):
    for g in range(N_KV_HEADS):
        cols = slice(g * HEAD_DIM, (g + 1) * HEAD_DIM)
        kg = k_ref[:, cols].astype(BF16)
        vg = v_ref[:, cols].astype(BF16)
        for hh in range(GQA_GROUP):
            h = g * GQA_GROUP + hh
            hc = slice(h * HEAD_DIM, (h + 1) * HEAD_DIM)
            o = _softmax_pv([_dot_nt(q_ref[:, hc], kg)], [vg])
            o_ref[:, hc] = o.astype(BF16)


def _attention_ctx(q, k, v):
    return pl.pallas_call(
        _attn_ctx_kernel,
        grid=(BATCH,),
        in_specs=[
            pl.BlockSpec((SEQ, ATTN_WIDTH), lambda b: (b, 0)),
            pl.BlockSpec((SEQ, KV_WIDTH), lambda b: (b, 0)),
            pl.BlockSpec((SEQ, KV_WIDTH), lambda b: (b, 0)),
        ],
        out_specs=pl.BlockSpec((SEQ, ATTN_WIDTH), lambda b: (b, 0)),
        out_shape=jax.ShapeDtypeStruct((N_CTX, ATTN_WIDTH), BF16),
        compiler_params=_params("arbitrary"),
        name="attention_ctx",
    )(q, k, v)


def _attn_lat_kernel(q_ref, k_ref, v_ref, ck_ref, cv_ref, o_ref):
    for g in range(N_KV_HEADS):
        cols = slice(g * HEAD_DIM, (g + 1) * HEAD_DIM)
        kg = k_ref[:, cols].astype(BF16)
        vg = v_ref[:, cols].astype(BF16)
        ckg = ck_ref[g].astype(BF16)
        cvg = cv_ref[g].astype(BF16)
        for hh in range(GQA_GROUP):
            h = g * GQA_GROUP + hh
            hc = slice(h * HEAD_DIM, (h + 1) * HEAD_DIM)
            qh = q_ref[:, hc]
            o = _softmax_pv([_dot_nt(qh, kg), _dot_nt(qh, ckg)], [vg, cvg])
            o_ref[:, hc] = o.astype(BF16)


def _attention_lat(l, q, k, v, cache_k, cache_v):
    q_blocks = DEC_SEQ // Q_TILE
    q_off = N_CTX // Q_TILE
    kv_off = N_CTX // DEC_SEQ
    cache_spec = pl.BlockSpec((None, None, N_KV_HEADS, PAST_LEN, HEAD_DIM),
                              lambda b, i: (b, l, 0, 0, 0))
    return pl.pallas_call(
        _attn_lat_kernel,
        grid=(DEC_BATCH, q_blocks),
        in_specs=[
            pl.BlockSpec((Q_TILE, ATTN_WIDTH), lambda b, i: (q_off + b * q_blocks + i, 0)),
            pl.BlockSpec((DEC_SEQ, KV_WIDTH), lambda b, i: (kv_off + b, 0)),
            pl.BlockSpec((DEC_SEQ, KV_WIDTH), lambda b, i: (kv_off + b, 0)),
            cache_spec,
            cache_spec,
        ],
        out_specs=pl.BlockSpec((Q_TILE, ATTN_WIDTH), lambda b, i: (b * q_blocks + i, 0)),
        out_shape=jax.ShapeDtypeStruct((N_LAT, ATTN_WIDTH), BF16),
        compiler_params=_params("arbitrary", "arbitrary"),
        name="attention_lat",
    )(q, k, v, cache_k, cache_v)


def _lower_bound(lb_ref, l):
    z = lb_ref[...]
    e = jnp.exp(z - z.max(axis=0, keepdims=True))
    p = e / e.sum(axis=0, keepdims=True)
    acc = p[0:1] * 0.0
    for r in range(1, l + 1):
        acc = acc + p[r:r + 1]
    return acc


def _hgrn_tables(rev):
    t = np.arange(CHUNK)
    tt, uu = t[:, None], t[None, :]
    groups = []
    masks = []
    for h in HGRN_LEVELS:
        right = (t & h) != 0
        edge = ((t // (2 * h)) * 2 * h + h - 1)[:, None]
        groups.append(np.where(right[:, None], (uu > edge) & (uu <= tt), (uu > tt) & (uu <= edge)))
        same = (tt // (2 * h)) == (uu // (2 * h))
        masks.append(same & right[:, None] & ~right[None, :])
    masks.append(tt == uu)
    groups += [uu <= tt, uu > tt]
    sums = np.stack(groups).astype(np.float32)
    masks = np.stack(masks).astype(np.float32)
    if rev:
        sums = sums[:, ::-1, ::-1]
        masks = masks[:, ::-1, ::-1]
    sums = sums.reshape(-1, CHUNK)
    return (jnp.asarray(np.concatenate([sums, sums], axis=1), dtype=BF16),
            jnp.asarray(np.ascontiguousarray(masks)))


def _hgrn_block(q, x, v, lb, sums_ref, masks_ref, st_ref, rev):
    n = q.shape[0] // CHUNK
    f = lb + (1.0 - lb) * _sigmoid(x)
    kk = 1.0 - f
    g = jnp.log(f) * LOG2E
    g_hi, g_lo = _split_bf16(g)
    vb = v.astype(BF16)
    rid = lax.broadcasted_iota(jnp.int32, (CHUNK, HGRN_K), 0)
    edge = 0 if rev else CHUNK - 1
    n_lev = len(HGRN_LEVELS)
    outs = [None] * n
    for c in (reversed(range(n)) if rev else range(n)):
        rs = slice(c * CHUNK, (c + 1) * CHUNK)
        e = jnp.exp2(_dot(sums_ref[...], jnp.concatenate([g_hi[rs], g_lo[rs]], axis=0)))
        q_c, k_c, v_c = q[rs], kk[rs], vb[rs]
        a = _dot_nt(q_c.astype(BF16), k_c.astype(BF16)) * masks_ref[n_lev]
        for i, h in enumerate(HGRN_LEVELS):
            later = ((rid & h) == 0) if rev else ((rid & h) != 0)
            z = (jnp.where(later, q_c, k_c) * e[i * CHUNK:(i + 1) * CHUNK]).astype(BF16)
            a = a + _dot_nt(z, z) * masks_ref[i]
        e_in = e[n_lev * CHUNK:(n_lev + 1) * CHUNK]
        e_out = e[(n_lev + 1) * CHUNK:(n_lev + 2) * CHUNK]
        st = st_ref[...]
        outs[c] = (_dot(a.astype(BF16), v_c)
                   + _dot_nt((q_c * e_in).astype(BF16), st.astype(BF16)))
        k_dec = (k_c * e_out).astype(BF16)
        st_ref[...] = st * e_in[edge:edge + 1] + _dot_tn(v_c, k_dec)
    return jnp.concatenate(outs, axis=0)


def _hgrn_kernel(l, seq_len, has_init, *refs):
    (hq_ref, ff_ref, fb_ref, hi_ref, hg_ref, lbf_ref, lbb_ref, hn_ref,
     sumf_ref, sumb_ref, mskf_ref, mskb_ref) = refs[:12]
    refs = refs[12:]
    if has_init:
        s0f_ref, s0b_ref, rec_ref, of_ref, ob_ref, stf_ref, stb_ref = refs
    else:
        rec_ref, sf_ref, sb_ref, of_ref, ob_ref, stf_ref, stb_ref = refs
    lb_f = _lower_bound(lbf_ref, l)
    lb_b = _lower_bound(lbb_ref, l)
    if has_init:
        stf_ref[...] = s0f_ref[...].T
        stb_ref[...] = s0b_ref[...].T
    else:
        stf_ref[...] = jnp.zeros((HGRN_V, HGRN_K), F32)
        stb_ref[...] = jnp.zeros((HGRN_V, HGRN_K), F32)
    rows = BLOCK_CHUNKS * CHUNK
    n_blocks = seq_len // rows

    def step(rf, rb):
        of_ref[rf, :] = _hgrn_block(hq_ref[rf, :], ff_ref[rf, :], hi_ref[rf, :], lb_f,
                                    sumf_ref, mskf_ref, stf_ref, False)
        ob_ref[rb, :] = _hgrn_block(hq_ref[rb, :], fb_ref[rb, :], hi_ref[rb, :], lb_b,
                                    sumb_ref, mskb_ref, stb_ref, True)

    if n_blocks == 1:
        step(pl.ds(0, rows), pl.ds(0, rows))
    else:
        def body(i, carry):
            step(pl.ds(pl.multiple_of(i * rows, rows), rows),
                 pl.ds(pl.multiple_of((n_blocks - 1 - i) * rows, rows), rows))
            return carry

        lax.fori_loop(0, n_blocks, body, 0)
    o = of_ref[...] + ob_ref[...]
    o = o * lax.rsqrt(jnp.mean(o * o, axis=-1, keepdims=True) + EPS) * hn_ref[...]
    rec_ref[...] = o * hg_ref[...]
    if not has_init:
        sf_ref[...] = stf_ref[...].T
        sb_ref[...] = stb_ref[...].T


def _hgrn(l, h5, row_off, n_seq, seq_len, lb_fwd, lb_bwd, hnorm_w, tables, init=None):
    blocks_off = row_off // seq_len
    part = lambda k: pl.BlockSpec((seq_len, HGRN_K),
                                  lambda b, h: (blocks_off + b, k * HGRN_HEADS + h))
    lb_spec = pl.BlockSpec((DEPTH, HGRN_K), lambda b, h: (0, h))
    sum_spec = pl.BlockSpec((8 * CHUNK, 2 * CHUNK), lambda b, h: (0, 0))
    msk_spec = pl.BlockSpec((len(HGRN_LEVELS) + 1, CHUNK, CHUNK), lambda b, h: (0, 0, 0))
    in_specs = [part(0), part(1), part(2), part(3), part(4), lb_spec, lb_spec,
                pl.BlockSpec((None, 1, HGRN_V), lambda b, h: (l, 0, 0)),
                sum_spec, sum_spec, msk_spec, msk_spec]
    args = [h5, h5, h5, h5, h5, lb_fwd, lb_bwd, hnorm_w, *tables]
    rec_spec = pl.BlockSpec((seq_len, HGRN_V), lambda b, h: (b, h))
    rec_shape = jax.ShapeDtypeStruct((n_seq * seq_len, HGRN_WIDTH), F32)
    if init is not None:
        st_spec = pl.BlockSpec((None, None, None, HGRN_K, HGRN_V), lambda b, h: (b, l, h, 0, 0))
        in_specs += [st_spec, st_spec]
        args += list(init)
        out_specs, out_shape = rec_spec, rec_shape
    else:
        st_spec = pl.BlockSpec((None, None, HGRN_K, HGRN_V), lambda b, h: (b, h, 0, 0))
        st_shape = jax.ShapeDtypeStruct((n_seq, HGRN_HEADS, HGRN_K, HGRN_V), F32)
        out_specs, out_shape = [rec_spec, st_spec, st_spec], [rec_shape, st_shape, st_shape]
    return pl.pallas_call(
        functools.partial(_hgrn_kernel, l, seq_len, init is not None),
        grid=(n_seq, HGRN_HEADS),
        in_specs=in_specs,
        out_specs=out_specs,
        out_shape=out_shape,
        scratch_shapes=[pltpu.VMEM((seq_len, HGRN_V), F32), pltpu.VMEM((seq_len, HGRN_V), F32),
                        pltpu.VMEM((HGRN_V, HGRN_K), F32), pltpu.VMEM((HGRN_V, HGRN_K), F32)],
        compiler_params=_params("arbitrary", "arbitrary"),
        name="hgrn_lat" if init is not None else "hgrn_ctx",
    )(*args)


def _route(scores, bias, sel_member, sel_group):
    sel = scores + bias
    members = [_dot_exact(sel, sel_member[k]) for k in range(EXPERTS_PER_GROUP)]
    pair = None
    for i in range(EXPERTS_PER_GROUP):
        for j in range(i + 1, EXPERTS_PER_GROUP):
            s = members[i] + members[j]
            pair = s if pair is None else jnp.maximum(pair, s)
    lane = lax.broadcasted_iota(jnp.int32, sel.shape, 1)
    best = _dot_exact(pair, sel_group[0])
    best_g = jnp.zeros(sel.shape, jnp.int32)
    for m in range(1, N_GROUPS):
        cand = _dot_exact(pair, sel_group[m])
        upd = cand > best
        best_g = jnp.where(upd, m, best_g)
        best = jnp.where(upd, cand, best)
    in_group = (lane >> 2) == best_g
    pos = lane & (EXPERTS_PER_GROUP - 1)
    rank = jnp.zeros(sel.shape, jnp.int32)
    for k in range(EXPERTS_PER_GROUP):
        ahead = (members[k] > sel) | ((members[k] == sel) & (k < pos))
        rank = rank + jnp.where(ahead, 1, 0)
    w = jnp.where(in_group & (rank < 2), scores, 0.0)
    return w / w.sum(axis=-1, keepdims=True)


def _merge_kernel(attn_ref, rec_ref, g_ref, x_ref, mod_ref, n2_ref, wa_ref, wh_ref, wo_ref,
                  wr_ref, br_ref, selm_ref, selg_ref, pad_ref, x1_ref, slab_ref, route_ref):
    ya = _dot(attn_ref[...], wa_ref[...])
    yh = _dot(rec_ref[...].astype(BF16), wh_ref[...])
    merged = g_ref[:, 0:D_MODEL] * ya + g_ref[:, D_MODEL:2 * D_MODEL] * yh
    out = _dot(merged.astype(BF16), wo_ref[...])
    x1 = x_ref[...] + mod_ref[2:3, :] * out
    x1_ref[...] = x1
    xn = x1 * lax.rsqrt(jnp.mean(x1 * x1, axis=-1, keepdims=True) + EPS) * n2_ref[...]
    h2 = xn * (1.0 + mod_ref[4:5, :]) + mod_ref[3:4, :]
    h_hi, h_lo = _split_bf16(h2)
    w_hi, w_lo = _split_bf16(wr_ref[...])
    scores = _sigmoid(_dot(jnp.concatenate([h_hi, h_lo, h_hi], axis=1),
                           jnp.concatenate([w_hi, w_hi, w_lo], axis=0)))
    gates = _route(scores, br_ref[...], [selm_ref[k] for k in range(EXPERTS_PER_GROUP)],
                   [selg_ref[m] for m in range(N_GROUPS)])
    gates = _dot_exact(gates, pad_ref[...])
    route_ref[...] = gates
    for s in range(FEAT_ROWS):
        slab_ref[:, s, :] = h2[:, s * LANES:(s + 1) * LANES]
    slab_ref[:, FEAT_ROWS, :] = gates
    for s in range(FEAT_ROWS + 1, SLAB_ROWS):
        slab_ref[:, s, :] = jnp.zeros_like(gates)


def _merge(l, attn, rec, gates, x, mod, norm2_w, w_br_attn, w_br_hgrn, w_out, w_router, b_router,
           sel_member, sel_group, lane_pad):
    row = lambda w: pl.BlockSpec((TOK_TILE, w), lambda i: (i, 0))
    layer = lambda r, c: pl.BlockSpec((None, r, c), lambda i: (l, 0, 0))
    full = lambda *s: pl.BlockSpec(s, lambda i: (0,) * len(s))
    return pl.pallas_call(
        _merge_kernel,
        grid=(N_TOK // TOK_TILE,),
        in_specs=[
            row(ATTN_WIDTH), row(HGRN_WIDTH), row(2 * D_MODEL), row(D_MODEL),
            pl.BlockSpec((None, None, N_MOD, D_MODEL), lambda i: (l, _tile_cond(i), 0, 0)),
            layer(1, D_MODEL),
            layer(ATTN_WIDTH, D_MODEL), layer(HGRN_WIDTH, D_MODEL), layer(D_MODEL, D_MODEL),
            full(D_MODEL, N_EXPERTS), full(1, N_EXPERTS),
            full(EXPERTS_PER_GROUP, N_EXPERTS, N_EXPERTS), full(N_GROUPS, N_EXPERTS, N_EXPERTS),
            full(N_EXPERTS, LANES),
        ],
        out_specs=[row(D_MODEL),
                   pl.BlockSpec((TOK_TILE, SLAB_ROWS, LANES), lambda i: (i, 0, 0)),
                   row(LANES)],
        out_shape=[
            jax.ShapeDtypeStruct((N_TOK, D_MODEL), F32),
            jax.ShapeDtypeStruct((N_TOK, SLAB_ROWS, LANES), F32),
            jax.ShapeDtypeStruct((N_TOK, LANES), F32),
        ],
        compiler_params=_params("arbitrary"),
        name="merge_router",
    )(attn, rec, gates, x, mod, norm2_w, w_br_attn, w_br_hgrn, w_out, w_router, b_router,
      sel_member, sel_group, lane_pad)


def _plan_kernel(route_ref, dst_ref, meta_ref, tot_ref, run_ref):
    p, i = pl.program_id(0), pl.program_id(1)

    @pl.when(jnp.logical_and(p == 0, i == 0))
    def _():
        tot_ref[...] = jnp.zeros_like(tot_ref)

    @pl.when(i == 0)
    def _():
        run_ref[...] = jnp.zeros_like(run_ref)

    grp = lax.broadcasted_iota(jnp.int32, (SUBLANES, LANES), 0)
    lane = lax.broadcasted_iota(jnp.int32, (SUBLANES, LANES), 1)
    member = jnp.where(((lane >> 2) == grp) & (lane < N_EXPERTS), 1.0, 0.0)
    one_hot = jnp.where(_dot_nt(member, route_ref[...], precision=_HI) > 0.0, 1.0, 0.0)
    src = lax.broadcasted_iota(jnp.int32, (PLAN_TILE, PLAN_TILE), 0)
    tgt = lax.broadcasted_iota(jnp.int32, (PLAN_TILE, PLAN_TILE), 1)
    before = jnp.where(src < tgt, 1.0, 0.0).astype(BF16)
    rank = _dot(one_hot.astype(BF16), before) + run_ref[:, 0:1]
    tile_tot = one_hot.sum(axis=1, keepdims=True)

    @pl.when(p == 0)
    def _():
        tot_ref[...] += tile_tot

    @pl.when(p == 1)
    def _():
        run_ref[...] += tile_tot
        tot = tot_ref[:, 0:1]
        n_tiles = jnp.floor((tot + (SORT_TILE - 1)) * (1.0 / SORT_TILE))
        offs, acc = [], jnp.zeros((1, 1), F32)
        for m in range(N_GROUPS):
            offs.append(acc * SORT_TILE)
            acc = acc + n_tiles[m:m + 1]
        dst = sum(one_hot[m:m + 1] * (offs[m] + rank[m:m + 1]) for m in range(N_GROUPS))
        for r in range(PLAN_TILE // LANES):
            dst_ref[r:r + 1, :] = dst[:, r * LANES:(r + 1) * LANES].astype(jnp.int32)
        tile_row = (lax.broadcasted_iota(jnp.int32, (1, LANES), 1) * SORT_TILE).astype(F32)
        tile_group = sum(jnp.where(tile_row >= offs[m], 1.0, 0.0) for m in range(1, N_GROUPS))
        rows = [tile_group, jnp.broadcast_to(acc, (1, LANES))]
        rows += [jnp.zeros((1, LANES), F32)] * (SUBLANES - len(rows))
        meta_ref[...] = jnp.concatenate(rows, axis=0).astype(jnp.int32)


def _plan(route):
    n_blocks = N_TOK // PLAN_TILE
    dst, meta = pl.pallas_call(
        _plan_kernel,
        grid=(2, n_blocks),
        in_specs=[pl.BlockSpec((PLAN_TILE, LANES), lambda p, i: (i, 0))],
        out_specs=[pl.BlockSpec((PLAN_TILE // LANES, LANES), lambda p, i: (i * p, 0)),
                   pl.BlockSpec((SUBLANES, LANES), lambda p, i: (0, 0))],
        out_shape=[jax.ShapeDtypeStruct((N_TOK // LANES, LANES), jnp.int32),
                   jax.ShapeDtypeStruct((SUBLANES, LANES), jnp.int32)],
        scratch_shapes=[pltpu.VMEM((SUBLANES, LANES), F32), pltpu.VMEM((SUBLANES, LANES), F32)],
        compiler_params=_params("arbitrary", "arbitrary"),
        name="route_plan",
    )(route)
    return dst, meta[0, :N_SORT_TILES], meta[1, 0:1]


def _start_rows(make_copy):
    def start(c, carry):
        make_copy(c).start()
        return carry

    lax.fori_loop(0, LANES, start, 0, unroll=8)


def _invert_kernel(dst_ref, src_ref):
    i = pl.program_id(0)

    @pl.when(i == 0)
    def _():
        def clear(n, carry):
            src_ref[lax.shift_right_logical(n, 7), n & (LANES - 1)] = 0
            return carry

        lax.fori_loop(0, N_SORT_TILES * SORT_TILE, clear, 0, unroll=8)

    for r in range(PLAN_TILE // LANES):
        def put(c, carry, r=r):
            d = dst_ref[r, c]
            src_ref[lax.shift_right_logical(d, 7), d & (LANES - 1)] = i * PLAN_TILE + r * LANES + c
            return carry

        lax.fori_loop(0, LANES, put, 0, unroll=8)


def _invert(dst):
    return pl.pallas_call(
        _invert_kernel,
        grid=(N_TOK // PLAN_TILE,),
        in_specs=[pl.BlockSpec((PLAN_TILE // LANES, LANES), lambda i: (i, 0), memory_space=pltpu.SMEM)],
        out_specs=pl.BlockSpec((N_SORT_TILES * SORT_TILE // LANES, LANES), lambda i: (0, 0),
                               memory_space=pltpu.SMEM),
        out_shape=jax.ShapeDtypeStruct((N_SORT_TILES * SORT_TILE // LANES, LANES), jnp.int32),
        compiler_params=_params("arbitrary"),
        name="invert_plan",
    )(dst)


_SRC_BLOCK_TILES = SUBLANES * LANES // SORT_TILE


def _experts_kernel(tg_ref, nv_ref, src_ref, slab_ref, wg_ref, wu_ref, wd_ref, y_ref, x_ref, sem):
    j = pl.program_id(0)

    @pl.when(j < nv_ref[0])
    def _():
        groups = range(SORT_TILE // LANES)
        first_row = (j % _SRC_BLOCK_TILES) * (SORT_TILE // LANES)
        for r in groups:
            _start_rows(lambda c, r=r: pltpu.make_async_copy(
                slab_ref.at[src_ref[first_row + r, c]], x_ref.at[r * LANES + c], sem))
        for r in groups:
            pltpu.make_async_copy(slab_ref.at[pl.ds(0, LANES)],
                                  x_ref.at[pl.ds(r * LANES, LANES)], sem).wait()
        first_expert = tg_ref[j] * EXPERTS_PER_GROUP
        h = jnp.concatenate([x_ref[:, s, :] for s in range(FEAT_ROWS)], axis=1).astype(BF16)
        gates = x_ref[:, FEAT_ROWS, :]
        lane = lax.broadcasted_iota(jnp.int32, gates.shape, 1)
        acc = None
        for k in range(EXPERTS_PER_GROUP):
            a = _dot(h, wg_ref[k])
            u = _dot(h, wu_ref[k])
            ge = jnp.sum(jnp.where(lane == first_expert + k, gates, 0.0), axis=-1, keepdims=True)
            y = _dot((a * _sigmoid(a) * u * ge).astype(BF16), wd_ref[k])
            acc = y if acc is None else acc + y
        for s in range(FEAT_ROWS):
            y_ref[:, s, :] = acc[:, s * LANES:(s + 1) * LANES]

    @pl.when(j >= nv_ref[0])
    def _():
        y_ref[...] = jnp.zeros_like(y_ref)


def _experts(l, tile_group, n_used, src, slab, w_gate, w_up, w_down):
    tile = lambda j, tg, nv: jnp.minimum(j, nv[0] - 1)
    w_spec = lambda r, c: pl.BlockSpec((None, None, EXPERTS_PER_GROUP, r, c),
                                       lambda j, tg, nv: (l, tg[tile(j, tg, nv)], 0, 0, 0))
    return pl.pallas_call(
        _experts_kernel,
        grid_spec=pltpu.PrefetchScalarGridSpec(
            num_scalar_prefetch=2,
            grid=(N_SORT_TILES,),
            in_specs=[
                pl.BlockSpec((SUBLANES, LANES), lambda j, tg, nv: (j // _SRC_BLOCK_TILES, 0),
                             memory_space=pltpu.SMEM),
                pl.BlockSpec(memory_space=pl.ANY),
                w_spec(D_MODEL, D_FF), w_spec(D_MODEL, D_FF), w_spec(D_FF, D_MODEL),
            ],
            out_specs=pl.BlockSpec((SORT_TILE, FEAT_ROWS, LANES), lambda j, tg, nv: (j, 0, 0)),
            scratch_shapes=[pltpu.VMEM((SORT_TILE, SLAB_ROWS, LANES), F32),
                            pltpu.SemaphoreType.DMA(())],
        ),
        out_shape=jax.ShapeDtypeStruct((N_SORT_TILES * SORT_TILE, FEAT_ROWS, LANES), F32),
        compiler_params=_params("arbitrary"),
        name="experts",
    )(tile_group, n_used, src, slab, w_gate, w_up, w_down)


def _combine_kernel(dst_ref, y_ref, x1_ref, mod_ref, o_ref, buf_ref, sem):
    groups = range(PLAN_TILE // LANES)
    for r in groups:
        _start_rows(lambda c, r=r: pltpu.make_async_copy(
            y_ref.at[dst_ref[r, c]], buf_ref.at[r * LANES + c], sem))
    for r in groups:
        pltpu.make_async_copy(y_ref.at[pl.ds(0, LANES)],
                              buf_ref.at[pl.ds(r * LANES, LANES)], sem).wait()
    for s in range(FEAT_ROWS):
        cols = slice(s * LANES, (s + 1) * LANES)
        o_ref[:, cols] = x1_ref[:, cols] + mod_ref[5:6, cols] * buf_ref[:, s, :]


_PLAN_CTX_TILES = N_CTX // PLAN_TILE
_PLAN_LAT_TILES_PER_SEQ = DEC_SEQ // PLAN_TILE


def _plan_cond(i):
    return jnp.where(i < _PLAN_CTX_TILES, 0, 1 + (i - _PLAN_CTX_TILES) // _PLAN_LAT_TILES_PER_SEQ)


def _combine(l, dst, expert_out, x1, mod):
    return pl.pallas_call(
        _combine_kernel,
        grid=(N_TOK // PLAN_TILE,),
        in_specs=[
            pl.BlockSpec((PLAN_TILE // LANES, LANES), lambda i: (i, 0), memory_space=pltpu.SMEM),
            pl.BlockSpec(memory_space=pl.ANY),
            pl.BlockSpec((PLAN_TILE, D_MODEL), lambda i: (i, 0)),
            pl.BlockSpec((None, None, N_MOD, D_MODEL), lambda i: (l, _plan_cond(i), 0, 0)),
        ],
        out_specs=pl.BlockSpec((PLAN_TILE, D_MODEL), lambda i: (i, 0)),
        out_shape=jax.ShapeDtypeStruct((N_TOK, D_MODEL), F32),
        scratch_shapes=[pltpu.VMEM((PLAN_TILE, FEAT_ROWS, LANES), F32), pltpu.SemaphoreType.DMA(())],
        compiler_params=_params("arbitrary"),
        name="combine",
    )(dst, expert_out, x1, mod)


def _rope_tables():
    pos = np.arange(DEC_SEQ)
    inv_freq = ROPE_THETA ** (-np.arange(0, AXIS_DIM, 2, dtype=np.float32) / AXIS_DIM)
    ang_r = (pos // GRID_W).astype(np.float32)[:, None] * inv_freq[None, :]
    ang_c = (pos % GRID_W).astype(np.float32)[:, None] * inv_freq[None, :]
    ang = jnp.asarray(np.concatenate([ang_r, ang_r, ang_c, ang_c], axis=-1).astype(np.float32))
    sign = np.where(np.arange(HEAD_DIM) % AXIS_DIM < AXIS_DIM // 2, -1.0, 1.0).astype(np.float32)
    cos = jnp.concatenate([jnp.cos(ang), jnp.ones((TOK_TILE, HEAD_DIM), F32)], axis=0)
    sin = jnp.concatenate([jnp.sin(ang) * sign, jnp.zeros((TOK_TILE, HEAD_DIM), F32)], axis=0)
    return jnp.tile(cos, (1, N_Q_HEADS)), jnp.tile(sin, (1, N_Q_HEADS))


def _selectors():
    lane = np.arange(N_EXPERTS)
    member = np.stack([(lane[:, None] == (lane[None, :] // EXPERTS_PER_GROUP) * EXPERTS_PER_GROUP + k)
                       for k in range(EXPERTS_PER_GROUP)]).astype(np.float32)
    group = np.stack([np.broadcast_to(lane[:, None] == m * EXPERTS_PER_GROUP, (N_EXPERTS, N_EXPERTS))
                      for m in range(N_GROUPS)]).astype(np.float32)
    head = (np.arange(ATTN_WIDTH)[:, None] // HEAD_DIM == np.arange(ATTN_WIDTH)[None, :] // HEAD_DIM)
    return jnp.asarray(member), jnp.asarray(group), jnp.asarray(head.astype(np.float32))


def kernel(x_prompt, x_sample, cache_k, cache_v, state_fwd, state_bwd, c, c_ctx, norm1_w, norm2_w, w_mod, b_mod, w_in, q_norm_w, k_norm_w, hgrn_lb_fwd, hgrn_lb_bwd, hgrn_norm_w, w_br_attn, w_br_hgrn, w_out, w_router, b_router, w_exp_gate, w_exp_up, w_exp_down):
    cos_t, sin_t = _rope_tables()
    sel_member, sel_group, head_ones = _selectors()
    sums_f, masks_f = _hgrn_tables(False)
    sums_b, masks_b = _hgrn_tables(True)
    hgrn_tables = (sums_f, sums_b, masks_f, masks_b)
    cond = jnp.concatenate([c_ctx[None, :], c, jnp.zeros((N_COND - 1 - DEC_BATCH, D_MODEL), F32)], axis=0)
    mod = _modulation(cond, w_mod, b_mod)

    w_in_b = w_in.astype(BF16)
    w_ba_b = w_br_attn.astype(BF16)
    w_bh_b = w_br_hgrn.astype(BF16)
    w_out_b = w_out.astype(BF16)
    by_group = lambda w: w.astype(BF16).reshape(DEPTH, N_GROUPS, EXPERTS_PER_GROUP, *w.shape[2:])
    w_eg_b, w_eu_b, w_ed_b = by_group(w_exp_gate), by_group(w_exp_up), by_group(w_exp_down)
    lane_pad = jnp.eye(N_EXPERTS, LANES, dtype=F32)
    b_router_r = b_router.reshape(1, N_EXPERTS)
    norm1_r = norm1_w.reshape(DEPTH, 1, D_MODEL)
    norm2_r = norm2_w.reshape(DEPTH, 1, D_MODEL)
    hnorm_r = hgrn_norm_w.reshape(DEPTH, 1, HGRN_V)
    qn_r = jnp.tile(q_norm_w, (1, N_Q_HEADS)).reshape(DEPTH, 1, ATTN_WIDTH)
    kn_r = jnp.tile(k_norm_w, (1, N_KV_HEADS)).reshape(DEPTH, 1, KV_WIDTH)

    x = jnp.concatenate([x_prompt.reshape(N_CTX, D_MODEL), x_sample.reshape(N_LAT, D_MODEL)], axis=0)
    ks_out, vs_out, sf_out, sb_out = [], [], [], []
    for l in range(DEPTH):
        q, k, v, h5, gates = _input_projection(l, x, mod, norm1_r, w_in_b, cos_t, sin_t, qn_r, kn_r,
                                               head_ones)
        attn_c = _attention_ctx(q, k, v)
        attn_s = _attention_lat(l, q, k, v, cache_k, cache_v)
        rec_c, sf, sb = _hgrn(l, h5, 0, BATCH, SEQ, hgrn_lb_fwd, hgrn_lb_bwd, hnorm_r, hgrn_tables)
        rec_s = _hgrn(l, h5, N_CTX, DEC_BATCH, DEC_SEQ, hgrn_lb_fwd, hgrn_lb_bwd, hnorm_r,
                      hgrn_tables, init=(state_fwd, state_bwd))
        attn = jnp.concatenate([attn_c, attn_s], axis=0)
        rec = jnp.concatenate([rec_c, rec_s], axis=0)
        x1, slab, route = _merge(l, attn, rec, gates, x, mod, norm2_r, w_ba_b, w_bh_b, w_out_b,
                                 w_router, b_router_r, sel_member, sel_group, lane_pad)
        dst, tile_group, n_used = _plan(route)
        expert_out = _experts(l, tile_group, n_used, _invert(dst), slab, w_eg_b, w_eu_b, w_ed_b)
        x = _combine(l, dst, expert_out, x1, mod)
        ks_out.append(k[:N_CTX].reshape(BATCH, SEQ, N_KV_HEADS, HEAD_DIM).transpose(0, 2, 1, 3))
        vs_out.append(v[:N_CTX].reshape(BATCH, SEQ, N_KV_HEADS, HEAD_DIM).transpose(0, 2, 1, 3))
        sf_out.append(sf)
        sb_out.append(sb)

    y_prompt = x[:N_CTX].reshape(BATCH, SEQ, D_MODEL)
    y_sample = x[N_CTX:].reshape(DEC_BATCH, DEC_SEQ, D_MODEL)
    return (y_prompt, y_sample, jnp.stack(ks_out, axis=1), jnp.stack(vs_out, axis=1),
            jnp.stack(sf_out, axis=1), jnp.stack(sb_out, axis=1))
```

```python
import functools

import numpy as np
import jax
import jax.numpy as jnp
from jax import lax
from jax.experimental import pallas as pl
from jax.experimental.pallas import tpu as pltpu

F32 = jnp.float32
BF16 = jnp.bfloat16

D_MODEL = 1024
BATCH = 16
SEQ = 256
DEPTH = 4
DEC_BATCH = 4
DEC_SEQ = 2048
PAST_LEN = 512
GRID_W = 64
N_Q_HEADS = 8
N_KV_HEADS = 2
GQA_GROUP = N_Q_HEADS // N_KV_HEADS
HEAD_DIM = 64
AXIS_DIM = HEAD_DIM // 2
ATTN_WIDTH = N_Q_HEADS * HEAD_DIM
KV_WIDTH = N_KV_HEADS * HEAD_DIM
ROPE_THETA = 10000.0
HGRN_HEADS = 4
HGRN_K = 128
HGRN_V = 128
HGRN_WIDTH = HGRN_HEADS * HGRN_K
N_EXPERTS = 16
N_GROUPS = 4
EXPERTS_PER_GROUP = N_EXPERTS // N_GROUPS
D_FF = 512
EPS = 1e-6
IN_COLS = ATTN_WIDTH + 2 * KV_WIDTH + 5 * HGRN_WIDTH + 2 * D_MODEL

N_CTX = BATCH * SEQ
N_LAT = DEC_BATCH * DEC_SEQ
N_TOK = N_CTX + N_LAT
N_COND = 8
N_MOD = 6

C_Q = 0
C_K = C_Q + ATTN_WIDTH
C_V = C_K + KV_WIDTH
C_HQ = C_V + KV_WIDTH
C_FF = C_HQ + HGRN_WIDTH
C_FB = C_FF + HGRN_WIDTH
C_HI = C_FB + HGRN_WIDTH
C_HG = C_HI + HGRN_WIDTH
C_GA = C_HG + HGRN_WIDTH
C_GH = C_GA + D_MODEL

LANES = 128
SUBLANES = 8
FEAT_ROWS = D_MODEL // LANES
SLAB_ROWS = 2 * FEAT_ROWS

TOK_TILE = 512
PLAN_TILE = 1024
SORT_TILE = 512
N_SORT_TILES = N_TOK // SORT_TILE + N_GROUPS
Q_TILE = 512
CHUNK = 64
HGRN_LEVELS = (32, 16, 8, 4, 2, 1)
BLOCK_CHUNKS = 4
LOG2E = 1.4426950408889634
VMEM_LIMIT = 56 * 1024 * 1024

_HI = lax.Precision.HIGHEST


def _sigmoid(x):
    return 1.0 / (1.0 + jnp.exp(-x))


def _dot(a, b):
    return jnp.dot(a, b, preferred_element_type=F32)


def _dot_nt(a, b, precision=None):
    return lax.dot_general(a, b, (((1,), (1,)), ((), ())), precision=precision,
                           preferred_element_type=F32)


def _dot_tn(a, b, precision=None):
    return lax.dot_general(a, b, (((0,), (0,)), ((), ())), precision=precision,
                           preferred_element_type=F32)


def _dot_exact(a, b):
    return jnp.dot(a, b, precision=_HI, preferred_element_type=F32)


def _split_bf16(a):
    hi = a.astype(BF16)
    return hi, (a - hi.astype(F32)).astype(BF16)


def _params(*sem):
    return pltpu.CompilerParams(dimension_semantics=sem, vmem_limit_bytes=VMEM_LIMIT)


def _mod_kernel(cond_ref, w_ref, b_ref, out_ref):
    c = cond_ref[...]
    sc = c * _sigmoid(c)
    out_ref[...] = _dot(sc.astype(BF16), w_ref[...].astype(BF16)) + b_ref[...]


def _modulation(cond, w_mod, b_mod):
    out = pl.pallas_call(
        _mod_kernel,
        grid=(DEPTH, N_MOD),
        in_specs=[
            pl.BlockSpec((N_COND, D_MODEL), lambda l, j: (0, 0)),
            pl.BlockSpec((None, D_MODEL, D_MODEL), lambda l, j: (l, 0, j)),
            pl.BlockSpec((None, 1, D_MODEL), lambda l, j: (l, 0, j)),
        ],
        out_specs=pl.BlockSpec((None, None, N_COND, D_MODEL), lambda l, j: (l, j, 0, 0)),
        out_shape=jax.ShapeDtypeStruct((DEPTH, N_MOD, N_COND, D_MODEL), F32),
        compiler_params=_params("arbitrary", "arbitrary"),
        name="modulation",
    )(cond, w_mod, b_mod.reshape(DEPTH, 1, N_MOD * D_MODEL))
    return out.transpose(0, 2, 1, 3)


def _head_mean_sq(a, bd):
    hi, lo = _split_bf16(a * a)
    return (_dot(hi, bd) + _dot(lo, bd)) * (1.0 / HEAD_DIM)


def _rope(x, cos, sin_signed):
    width = x.shape[-1]
    lane = lax.broadcasted_iota(jnp.int32, x.shape, 1)
    first = (lane & (AXIS_DIM - 1)) < (AXIS_DIM // 2)
    rot = jnp.where(first, pltpu.roll(x, width - AXIS_DIM // 2, 1), pltpu.roll(x, AXIS_DIM // 2, 1))
    return x * cos + rot * sin_signed


def _inproj_kernel(x_ref, mod_ref, n1_ref, w_ref, cos_ref, sin_ref, qn_ref, kn_ref, bd_ref,
                   q_ref, k_ref, v_ref, h_ref, g_ref):
    x = x_ref[...]
    xn = x * lax.rsqrt(jnp.mean(x * x, axis=-1, keepdims=True) + EPS) * n1_ref[...]
    xb = (xn * (1.0 + mod_ref[1:2, :]) + mod_ref[0:1, :]).astype(BF16)

    def proj(c0, width):
        return _dot(xb, w_ref[:, c0:c0 + width])

    a = proj(C_Q, ATTN_WIDTH)
    qn = a * lax.rsqrt(_head_mean_sq(a, bd_ref[...]) + EPS) * qn_ref[...]
    q_ref[...] = (_rope(qn, cos_ref[...], sin_ref[...]) * (HEAD_DIM ** -0.5)).astype(BF16)

    a = proj(C_K, KV_WIDTH)
    kn = a * lax.rsqrt(_head_mean_sq(a, bd_ref[:KV_WIDTH, :KV_WIDTH]) + EPS) * kn_ref[...]
    k_ref[...] = _rope(kn, cos_ref[:, :KV_WIDTH], sin_ref[:, :KV_WIDTH])
    v_ref[...] = proj(C_V, KV_WIDTH)

    a = proj(C_HQ, HGRN_WIDTH)
    h_ref[:, 0:HGRN_WIDTH] = a * _sigmoid(a) * (HGRN_K ** -0.5)
    h_ref[:, HGRN_WIDTH:2 * HGRN_WIDTH] = proj(C_FF, HGRN_WIDTH)
    h_ref[:, 2 * HGRN_WIDTH:3 * HGRN_WIDTH] = proj(C_FB, HGRN_WIDTH)
    h_ref[:, 3 * HGRN_WIDTH:4 * HGRN_WIDTH] = proj(C_HI, HGRN_WIDTH)
    a = proj(C_HG, HGRN_WIDTH)
    h_ref[:, 4 * HGRN_WIDTH:5 * HGRN_WIDTH] = a * _sigmoid(a)

    g_ref[:, 0:D_MODEL] = _sigmoid(proj(C_GA, D_MODEL))
    g_ref[:, D_MODEL:2 * D_MODEL] = _sigmoid(proj(C_GH, D_MODEL))


_CTX_TILES = N_CTX // TOK_TILE
_LAT_TILES_PER_SEQ = DEC_SEQ // TOK_TILE


def _tile_cond(i):
    return jnp.where(i < _CTX_TILES, 0, 1 + (i - _CTX_TILES) // _LAT_TILES_PER_SEQ)


def _tile_rope_block(i):
    return jnp.where(i < _CTX_TILES, _LAT_TILES_PER_SEQ, (i - _CTX_TILES) % _LAT_TILES_PER_SEQ)


def _input_projection(l, x, mod, norm1_w, w_in, cos_t, sin_t, qn_w, kn_w, bd):
    row = lambda w: pl.BlockSpec((TOK_TILE, w), lambda i: (i, 0))
    return pl.pallas_call(
        _inproj_kernel,
        grid=(N_TOK // TOK_TILE,),
        in_specs=[
            row(D_MODEL),
            pl.BlockSpec((None, None, N_MOD, D_MODEL), lambda i: (l, _tile_cond(i), 0, 0)),
            pl.BlockSpec((None, 1, D_MODEL), lambda i: (l, 0, 0)),
            pl.BlockSpec((None, D_MODEL, IN_COLS), lambda i: (l, 0, 0),
                         pipeline_mode=pl.Buffered(1)),
            pl.BlockSpec((TOK_TILE, ATTN_WIDTH), lambda i: (_tile_rope_block(i), 0)),
            pl.BlockSpec((TOK_TILE, ATTN_WIDTH), lambda i: (_tile_rope_block(i), 0)),
            pl.BlockSpec((None, 1, ATTN_WIDTH), lambda i: (l, 0, 0)),
            pl.BlockSpec((None, 1, KV_WIDTH), lambda i: (l, 0, 0)),
            pl.BlockSpec((ATTN_WIDTH, ATTN_WIDTH), lambda i: (0, 0)),
        ],
        out_specs=[row(ATTN_WIDTH), row(KV_WIDTH), row(KV_WIDTH), row(5 * HGRN_WIDTH),
                   row(2 * D_MODEL)],
        out_shape=[
            jax.ShapeDtypeStruct((N_TOK, ATTN_WIDTH), BF16),
            jax.ShapeDtypeStruct((N_TOK, KV_WIDTH), F32),
            jax.ShapeDtypeStruct((N_TOK, KV_WIDTH), F32),
            jax.ShapeDtypeStruct((N_TOK, 5 * HGRN_WIDTH), F32),
            jax.ShapeDtypeStruct((N_TOK, 2 * D_MODEL), F32),
        ],
        compiler_params=_params("arbitrary"),
        name="input_projection",
    )(x, mod, norm1_w, w_in, cos_t, sin_t, qn_w, kn_w, bd)


def _softmax_pv(scores, values):
    m = scores[0].max(axis=-1, keepdims=True)
    for s in scores[1:]:
        m = jnp.maximum(m, s.max(axis=-1, keepdims=True))
    num, den = None, None
    for s, v in zip(scores, values):
        p = jnp.exp(s - m)
        d = p.sum(axis=-1, keepdims=True)
        o = _dot(p.astype(BF16), v)
        num = o if num is None else num + o
        den = d if den is None else den + d
    return num / den


def _attn_ctx_kernel(q_ref, k_ref, v_ref, o_ref):
    for g in range(N_KV_HEADS):
        cols = slice(g * HEAD_DIM, (g + 1) * HEAD_DIM)
        kg = k_ref[:, cols].astype(BF16)
        vg = v_ref[:, cols].astype(BF16)
        for hh in range(GQA_GROUP):
            h = g * GQA_GROUP + hh
            hc = slice(h * HEAD_DIM, (h + 1) * HEAD_DIM)
            o = _softmax_pv([_dot_nt(q_ref[:, hc], kg)], [vg])
            o_ref[:, hc] = o.astype(BF16)


def _attention_ctx(q, k, v):
    return pl.pallas_call(
        _attn_ctx_kernel,
        grid=(BATCH,),
        in_specs=[
            pl.BlockSpec((SEQ, ATTN_WIDTH), lambda b: (b, 0)),
            pl.BlockSpec((SEQ, KV_WIDTH), lambda b: (b, 0)),
            pl.BlockSpec((SEQ, KV_WIDTH), lambda b: (b, 0)),
        ],
        out_specs=pl.BlockSpec((SEQ, ATTN_WIDTH), lambda b: (b, 0)),
        out_shape=jax.ShapeDtypeStruct((N_CTX, ATTN_WIDTH), BF16),
        compiler_params=_params("arbitrary"),
        name="attention_ctx",
    )(q, k, v)


def _attn_lat_kernel(q_ref, k_ref, v_ref, ck_ref, cv_ref, o_ref):
    for g in range(N_KV_HEADS):
        cols = slice(g * HEAD_DIM, (g + 1) * HEAD_DIM)
        kg = k_ref[:, cols].astype(BF16)
        vg = v_ref[:, cols].astype(BF16)
        ckg = ck_ref[g].astype(BF16)
        cvg = cv_ref[g].astype(BF16)
        for hh in range(GQA_GROUP):
            h = g * GQA_GROUP + hh
            hc = slice(h * HEAD_DIM, (h + 1) * HEAD_DIM)
            qh = q_ref[:, hc]
            o = _softmax_pv([_dot_nt(qh, kg), _dot_nt(qh, ckg)], [vg, cvg])
            o_ref[:, hc] = o.astype(BF16)


def _attention_lat(l, q, k, v, cache_k, cache_v):
    q_blocks = DEC_SEQ // Q_TILE
    q_off = N_CTX // Q_TILE
    kv_off = N_CTX // DEC_SEQ
    cache_spec = pl.BlockSpec((None, None, N_KV_HEADS, PAST_LEN, HEAD_DIM),
                              lambda b, i: (b, l, 0, 0, 0))
    return pl.pallas_call(
        _attn_lat_kernel,
        grid=(DEC_BATCH, q_blocks),
        in_specs=[
            pl.BlockSpec((Q_TILE, ATTN_WIDTH), lambda b, i: (q_off + b * q_blocks + i, 0)),
            pl.BlockSpec((DEC_SEQ, KV_WIDTH), lambda b, i: (kv_off + b, 0)),
            pl.BlockSpec((DEC_SEQ, KV_WIDTH), lambda b, i: (kv_off + b, 0)),
            cache_spec,
            cache_spec,
        ],
        out_specs=pl.BlockSpec((Q_TILE, ATTN_WIDTH), lambda b, i: (b * q_blocks + i, 0)),
        out_shape=jax.ShapeDtypeStruct((N_LAT, ATTN_WIDTH), BF16),
        compiler_params=_params("arbitrary", "arbitrary"),
        name="attention_lat",
    )(q, k, v, cache_k, cache_v)


def _lower_bound(lb_ref, l):
    z = lb_ref[...]
    e = jnp.exp(z - z.max(axis=0, keepdims=True))
    p = e / e.sum(axis=0, keepdims=True)
    acc = p[0:1] * 0.0
    for r in range(1, l + 1):
        acc = acc + p[r:r + 1]
    return acc


def _hgrn_tables(rev):
    t = np.arange(CHUNK)
    tt, uu = t[:, None], t[None, :]
    groups = []
    masks = []
    for h in HGRN_LEVELS:
        right = (t & h) != 0
        edge = ((t // (2 * h)) * 2 * h + h - 1)[:, None]
        groups.append(np.where(right[:, None], (uu > edge) & (uu <= tt), (uu > tt) & (uu <= edge)))
        same = (tt // (2 * h)) == (uu // (2 * h))
        masks.append(same & right[:, None] & ~right[None, :])
    masks.append(tt == uu)
    groups += [uu <= tt, uu > tt]
    sums = np.stack(groups).astype(np.float32)
    masks = np.stack(masks).astype(np.float32)
    if rev:
        sums = sums[:, ::-1, ::-1]
        masks = masks[:, ::-1, ::-1]
    sums = sums.reshape(-1, CHUNK)
    return (jnp.asarray(np.concatenate([sums, sums], axis=1), dtype=BF16),
            jnp.asarray(np.ascontiguousarray(masks)))


def _hgrn_block(q, x, v, lb, sums_ref, masks_ref, st_ref, rev):
    n = q.shape[0] // CHUNK
    f = lb + (1.0 - lb) * _sigmoid(x)
    kk = 1.0 - f
    g = jnp.log(f) * LOG2E
    g_hi, g_lo = _split_bf16(g)
    vb = v.astype(BF16)
    rid = lax.broadcasted_iota(jnp.int32, (CHUNK, HGRN_K), 0)
    edge = 0 if rev else CHUNK - 1
    n_lev = len(HGRN_LEVELS)
    outs = [None] * n
    for c in (reversed(range(n)) if rev else range(n)):
        rs = slice(c * CHUNK, (c + 1) * CHUNK)
        e = jnp.exp2(_dot(sums_ref[...], jnp.concatenate([g_hi[rs], g_lo[rs]], axis=0)))
        q_c, k_c, v_c = q[rs], kk[rs], vb[rs]
        a = _dot_nt(q_c.astype(BF16), k_c.astype(BF16)) * masks_ref[n_lev]
        for i, h in enumerate(HGRN_LEVELS):
            later = ((rid & h) == 0) if rev else ((rid & h) != 0)
            z = (jnp.where(later, q_c, k_c) * e[i * CHUNK:(i + 1) * CHUNK]).astype(BF16)
            a = a + _dot_nt(z, z) * masks_ref[i]
        e_in = e[n_lev * CHUNK:(n_lev + 1) * CHUNK]
        e_out = e[(n_lev + 1) * CHUNK:(n_lev + 2) * CHUNK]
        st = st_ref[...]
        outs[c] = (_dot(a.astype(BF16), v_c)
                   + _dot_nt((q_c * e_in).astype(BF16), st.astype(BF16)))
        k_dec = (k_c * e_out).astype(BF16)
        st_ref[...] = st * e_in[edge:edge + 1] + _dot_tn(v_c, k_dec)
    return jnp.concatenate(outs, axis=0)


def _hgrn_kernel(l, seq_len, has_init, *refs):
    (hq_ref, ff_ref, fb_ref, hi_ref, hg_ref, lbf_ref, lbb_ref, hn_ref,
     sumf_ref, sumb_ref, mskf_ref, mskb_ref) = refs[:12]
    refs = refs[12:]
    if has_init:
        s0f_ref, s0b_ref, rec_ref, of_ref, ob_ref, stf_ref, stb_ref = refs
    else:
        rec_ref, sf_ref, sb_ref, of_ref, ob_ref, stf_ref, stb_ref = refs
    lb_f = _lower_bound(lbf_ref, l)
    lb_b = _lower_bound(lbb_ref, l)
    if has_init:
        stf_ref[...] = s0f_ref[...].T
        stb_ref[...] = s0b_ref[...].T
    else:
        stf_ref[...] = jnp.zeros((HGRN_V, HGRN_K), F32)
        stb_ref[...] = jnp.zeros((HGRN_V, HGRN_K), F32)
    rows = BLOCK_CHUNKS * CHUNK
    n_blocks = seq_len // rows

    def step(rf, rb):
        of_ref[rf, :] = _hgrn_block(hq_ref[rf, :], ff_ref[rf, :], hi_ref[rf, :], lb_f,
                                    sumf_ref, mskf_ref, stf_ref, False)
        ob_ref[rb, :] = _hgrn_block(hq_ref[rb, :], fb_ref[rb, :], hi_ref[rb, :], lb_b,
                                    sumb_ref, mskb_ref, stb_ref, True)

    if n_blocks == 1:
        step(pl.ds(0, rows), pl.ds(0, rows))
    else:
        def body(i, carry):
            step(pl.ds(pl.multiple_of(i * rows, rows), rows),
                 pl.ds(pl.multiple_of((n_blocks - 1 - i) * rows, rows), rows))
            return carry

        lax.fori_loop(0, n_blocks, body, 0)
    o = of_ref[...] + ob_ref[...]
    o = o * lax.rsqrt(jnp.mean(o * o, axis=-1, keepdims=True) + EPS) * hn_ref[...]
    rec_ref[...] = o * hg_ref[...]
    if not has_init:
        sf_ref[...] = stf_ref[...].T
        sb_ref[...] = stb_ref[...].T


def _hgrn(l, h5, row_off, n_seq, seq_len, lb_fwd, lb_bwd, hnorm_w, tables, init=None):
    blocks_off = row_off // seq_len
    part = lambda k: pl.BlockSpec((seq_len, HGRN_K),
                                  lambda b, h: (blocks_off + b, k * HGRN_HEADS + h))
    lb_spec = pl.BlockSpec((DEPTH, HGRN_K), lambda b, h: (0, h))
    sum_spec = pl.BlockSpec((8 * CHUNK, 2 * CHUNK), lambda b, h: (0, 0))
    msk_spec = pl.BlockSpec((len(HGRN_LEVELS) + 1, CHUNK, CHUNK), lambda b, h: (0, 0, 0))
    in_specs = [part(0), part(1), part(2), part(3), part(4), lb_spec, lb_spec,
                pl.BlockSpec((None, 1, HGRN_V), lambda b, h: (l, 0, 0)),
                sum_spec, sum_spec, msk_spec, msk_spec]
    args = [h5, h5, h5, h5, h5, lb_fwd, lb_bwd, hnorm_w, *tables]
    rec_spec = pl.BlockSpec((seq_len, HGRN_V), lambda b, h: (b, h))
    rec_shape = jax.ShapeDtypeStruct((n_seq * seq_len, HGRN_WIDTH), F32)
    if init is not None:
        st_spec = pl.BlockSpec((None, None, None, HGRN_K, HGRN_V), lambda b, h: (b, l, h, 0, 0))
        in_specs += [st_spec, st_spec]
        args += list(init)
        out_specs, out_shape = rec_spec, rec_shape
    else:
        st_spec = pl.BlockSpec((None, None, HGRN_K, HGRN_V), lambda b, h: (b, h, 0, 0))
        st_shape = jax.ShapeDtypeStruct((n_seq, HGRN_HEADS, HGRN_K, HGRN_V), F32)
        out_specs, out_shape = [rec_spec, st_spec, st_spec], [rec_shape, st_shape, st_shape]
    return pl.pallas_call(
        functools.partial(_hgrn_kernel, l, seq_len, init is not None),
        grid=(n_seq, HGRN_HEADS),
        in_specs=in_specs,
        out_specs=out_specs,
        out_shape=out_shape,
        scratch_shapes=[pltpu.VMEM((seq_len, HGRN_V), F32), pltpu.VMEM((seq_len, HGRN_V), F32),
                        pltpu.VMEM((HGRN_V, HGRN_K), F32), pltpu.VMEM((HGRN_V, HGRN_K), F32)],
        compiler_params=_params("arbitrary", "arbitrary"),
        name="hgrn_lat" if init is not None else "hgrn_ctx",
    )(*args)


def _route(scores, bias, sel_member, sel_group):
    sel = scores + bias
    members = [_dot_exact(sel, sel_member[k]) for k in range(EXPERTS_PER_GROUP)]
    pair = None
    for i in range(EXPERTS_PER_GROUP):
        for j in range(i + 1, EXPERTS_PER_GROUP):
            s = members[i] + members[j]
            pair = s if pair is None else jnp.maximum(pair, s)
    lane = lax.broadcasted_iota(jnp.int32, sel.shape, 1)
    best = _dot_exact(pair, sel_group[0])
    best_g = jnp.zeros(sel.shape, jnp.int32)
    for m in range(1, N_GROUPS):
        cand = _dot_exact(pair, sel_group[m])
        upd = cand > best
        best_g = jnp.where(upd, m, best_g)
        best = jnp.where(upd, cand, best)
    in_group = (lane >> 2) == best_g
    pos = lane & (EXPERTS_PER_GROUP - 1)
    rank = jnp.zeros(sel.shape, jnp.int32)
    for k in range(EXPERTS_PER_GROUP):
        ahead = (members[k] > sel) | ((members[k] == sel) & (k < pos))
        rank = rank + jnp.where(ahead, 1, 0)
    w = jnp.where(in_group & (rank < 2), scores, 0.0)
    return w / w.sum(axis=-1, keepdims=True)


def _merge_kernel(attn_c_ref, attn_s_ref, rec_c_ref, rec_s_ref, g_ref, x_ref, mod_ref, n2_ref,
                  wa_ref, wh_ref, wo_ref, wr_ref, br_ref, selm_ref, selg_ref, pad_ref,
                  x1_ref, slab_ref, route_ref):
    is_ctx = pl.program_id(0) < _CTX_TILES
    attn = jnp.where(is_ctx, attn_c_ref[...], attn_s_ref[...])
    rec = jnp.where(is_ctx, rec_c_ref[...], rec_s_ref[...])
    ya = _dot(attn, wa_ref[...])
    yh = _dot(rec.astype(BF16), wh_ref[...])
    merged = g_ref[:, 0:D_MODEL] * ya + g_ref[:, D_MODEL:2 * D_MODEL] * yh
    out = _dot(merged.astype(BF16), wo_ref[...])
    x1 = x_ref[...] + mod_ref[2:3, :] * out
    x1_ref[...] = x1
    xn = x1 * lax.rsqrt(jnp.mean(x1 * x1, axis=-1, keepdims=True) + EPS) * n2_ref[...]
    h2 = xn * (1.0 + mod_ref[4:5, :]) + mod_ref[3:4, :]
    h_hi, h_lo = _split_bf16(h2)
    w_hi, w_lo = _split_bf16(wr_ref[...])
    scores = _sigmoid(_dot(jnp.concatenate([h_hi, h_lo, h_hi], axis=1),
                           jnp.concatenate([w_hi, w_hi, w_lo], axis=0)))
    gates = _route(scores, br_ref[...], [selm_ref[k] for k in range(EXPERTS_PER_GROUP)],
                   [selg_ref[m] for m in range(N_GROUPS)])
    gates = _dot_exact(gates, pad_ref[...])
    route_ref[...] = gates
    for s in range(FEAT_ROWS):
        slab_ref[:, s, :] = h2[:, s * LANES:(s + 1) * LANES]
    slab_ref[:, FEAT_ROWS, :] = gates
    for s in range(FEAT_ROWS + 1, SLAB_ROWS):
        slab_ref[:, s, :] = jnp.zeros_like(gates)


def _merge(l, attn, rec, gates, x, mod, norm2_w, w_br_attn, w_br_hgrn, w_out, w_router, b_router,
           sel_member, sel_group, lane_pad):
    row = lambda w: pl.BlockSpec((TOK_TILE, w), lambda i: (i, 0))
    ctx_row = lambda w: pl.BlockSpec((TOK_TILE, w), lambda i: (jnp.minimum(i, _CTX_TILES - 1), 0))
    lat_row = lambda w: pl.BlockSpec((TOK_TILE, w), lambda i: (jnp.maximum(i - _CTX_TILES, 0), 0))
    layer = lambda r, c: pl.BlockSpec((None, r, c), lambda i: (l, 0, 0))
    full = lambda *s: pl.BlockSpec(s, lambda i: (0,) * len(s))
    return pl.pallas_call(
        _merge_kernel,
        grid=(N_TOK // TOK_TILE,),
        in_specs=[
            ctx_row(ATTN_WIDTH), lat_row(ATTN_WIDTH), ctx_row(HGRN_WIDTH), lat_row(HGRN_WIDTH),
            row(2 * D_MODEL), row(D_MODEL),
            pl.BlockSpec((None, None, N_MOD, D_MODEL), lambda i: (l, _tile_cond(i), 0, 0)),
            layer(1, D_MODEL),
            layer(ATTN_WIDTH, D_MODEL), layer(HGRN_WIDTH, D_MODEL), layer(D_MODEL, D_MODEL),
            full(D_MODEL, N_EXPERTS), full(1, N_EXPERTS),
            full(EXPERTS_PER_GROUP, N_EXPERTS, N_EXPERTS), full(N_GROUPS, N_EXPERTS, N_EXPERTS),
            full(N_EXPERTS, LANES),
        ],
        out_specs=[row(D_MODEL),
                   pl.BlockSpec((TOK_TILE, SLAB_ROWS, LANES), lambda i: (i, 0, 0)),
                   row(LANES)],
        out_shape=[
            jax.ShapeDtypeStruct((N_TOK, D_MODEL), F32),
            jax.ShapeDtypeStruct((N_TOK, SLAB_ROWS, LANES), F32),
            jax.ShapeDtypeStruct((N_TOK, LANES), F32),
        ],
        compiler_params=_params("arbitrary"),
        name="merge_router",
    )(*attn, *rec, gates, x, mod, norm2_w, w_br_attn, w_br_hgrn, w_out, w_router, b_router,
      sel_member, sel_group, lane_pad)


def _plan_kernel(route_ref, dst_ref, meta_ref, tot_ref, run_ref):
    p, i = pl.program_id(0), pl.program_id(1)

    @pl.when(jnp.logical_and(p == 0, i == 0))
    def _():
        tot_ref[...] = jnp.zeros_like(tot_ref)

    @pl.when(i == 0)
    def _():
        run_ref[...] = jnp.zeros_like(run_ref)

    grp = lax.broadcasted_iota(jnp.int32, (SUBLANES, LANES), 0)
    lane = lax.broadcasted_iota(jnp.int32, (SUBLANES, LANES), 1)
    member = jnp.where(((lane >> 2) == grp) & (lane < N_EXPERTS), 1.0, 0.0)
    one_hot = jnp.where(_dot_nt(member, route_ref[...], precision=_HI) > 0.0, 1.0, 0.0)
    src = lax.broadcasted_iota(jnp.int32, (PLAN_TILE, PLAN_TILE), 0)
    tgt = lax.broadcasted_iota(jnp.int32, (PLAN_TILE, PLAN_TILE), 1)
    before = jnp.where(src < tgt, 1.0, 0.0).astype(BF16)
    rank = _dot(one_hot.astype(BF16), before) + run_ref[:, 0:1]
    tile_tot = one_hot.sum(axis=1, keepdims=True)

    @pl.when(p == 0)
    def _():
        tot_ref[...] += tile_tot

    @pl.when(p == 1)
    def _():
        run_ref[...] += tile_tot
        tot = tot_ref[:, 0:1]
        n_tiles = jnp.floor((tot + (SORT_TILE - 1)) * (1.0 / SORT_TILE))
        offs, acc = [], jnp.zeros((1, 1), F32)
        for m in range(N_GROUPS):
            offs.append(acc * SORT_TILE)
            acc = acc + n_tiles[m:m + 1]
        dst = sum(one_hot[m:m + 1] * (offs[m] + rank[m:m + 1]) for m in range(N_GROUPS))
        for r in range(PLAN_TILE // LANES):
            dst_ref[r:r + 1, :] = dst[:, r * LANES:(r + 1) * LANES].astype(jnp.int32)
        tile_row = (lax.broadcasted_iota(jnp.int32, (1, LANES), 1) * SORT_TILE).astype(F32)
        tile_group = sum(jnp.where(tile_row >= offs[m], 1.0, 0.0) for m in range(1, N_GROUPS))
        rows = [tile_group, jnp.broadcast_to(acc, (1, LANES))]
        rows += [jnp.zeros((1, LANES), F32)] * (SUBLANES - len(rows))
        meta_ref[...] = jnp.concatenate(rows, axis=0).astype(jnp.int32)


def _plan(route):
    n_blocks = N_TOK // PLAN_TILE
    dst, meta = pl.pallas_call(
        _plan_kernel,
        grid=(2, n_blocks),
        in_specs=[pl.BlockSpec((PLAN_TILE, LANES), lambda p, i: (i, 0))],
        out_specs=[pl.BlockSpec((PLAN_TILE // LANES, LANES), lambda p, i: (i * p, 0)),
                   pl.BlockSpec((SUBLANES, LANES), lambda p, i: (0, 0))],
        out_shape=[jax.ShapeDtypeStruct((N_TOK // LANES, LANES), jnp.int32),
                   jax.ShapeDtypeStruct((SUBLANES, LANES), jnp.int32)],
        scratch_shapes=[pltpu.VMEM((SUBLANES, LANES), F32), pltpu.VMEM((SUBLANES, LANES), F32)],
        compiler_params=_params("arbitrary", "arbitrary"),
        name="route_plan",
    )(route)
    return dst, meta[0, :N_SORT_TILES], meta[1, 0:1]


def _start_rows(make_copy):
    def start(c, carry):
        make_copy(c).start()
        return carry

    lax.fori_loop(0, LANES, start, 0, unroll=8)


def _invert_kernel(dst_ref, zero_ref, src_ref, sem):
    i = pl.program_id(0)

    @pl.when(i == 0)
    def _():
        clear = pltpu.make_async_copy(zero_ref, src_ref, sem)
        clear.start()
        clear.wait()

    for r in range(PLAN_TILE // LANES):
        def put(c, carry, r=r):
            d = dst_ref[r, c]
            src_ref[lax.shift_right_logical(d, 7), d & (LANES - 1)] = i * PLAN_TILE + r * LANES + c
            return carry

        lax.fori_loop(0, LANES, put, 0, unroll=8)


def _invert(dst):
    shape = (N_SORT_TILES * SORT_TILE // LANES, LANES)
    return pl.pallas_call(
        _invert_kernel,
        grid=(N_TOK // PLAN_TILE,),
        in_specs=[pl.BlockSpec((PLAN_TILE // LANES, LANES), lambda i: (i, 0), memory_space=pltpu.SMEM),
                  pl.BlockSpec(memory_space=pl.ANY)],
        out_specs=pl.BlockSpec(shape, lambda i: (0, 0), memory_space=pltpu.SMEM),
        out_shape=jax.ShapeDtypeStruct(shape, jnp.int32),
        scratch_shapes=[pltpu.SemaphoreType.DMA(())],
        compiler_params=_params("arbitrary"),
        name="invert_plan",
    )(dst, jnp.zeros(shape, jnp.int32))


_SRC_BLOCK_TILES = SUBLANES * LANES // SORT_TILE


def _experts_kernel(tg_ref, nv_ref, src_ref, nxt_ref, slab_ref, wg_ref, wu_ref, wd_ref, y_ref,
                    x_ref, sem):
    j = pl.program_id(0)
    n_used = nv_ref[0]
    groups = SORT_TILE // LANES

    def gather(idx_ref, tile, slot, unrolled):
        first_row = (tile % _SRC_BLOCK_TILES) * groups
        for r in range(groups):
            make = lambda c, r=r: pltpu.make_async_copy(
                slab_ref.at[idx_ref[first_row + r, c]], x_ref.at[slot, r * LANES + c], sem.at[slot])
            if unrolled:
                for c in range(LANES):
                    make(c).start()
            else:
                _start_rows(make)

    def wait_rows(slot):
        for r in range(groups):
            pltpu.make_async_copy(slab_ref.at[pl.ds(0, LANES)],
                                  x_ref.at[slot, pl.ds(r * LANES, LANES)], sem.at[slot]).wait()

    @pl.when(j == 0)
    def _():
        gather(src_ref, j, 0, False)

    @pl.when(j < n_used)
    def _():
        slot = j % 2
        wait_rows(slot)
        first_expert = tg_ref[j] * EXPERTS_PER_GROUP
        h = jnp.concatenate([x_ref[slot, :, s, :] for s in range(FEAT_ROWS)], axis=1).astype(BF16)
        gates = x_ref[slot, :, FEAT_ROWS, :]
        gather(nxt_ref, jnp.minimum(j + 1, n_used - 1), 1 - slot, True)
        lane = lax.broadcasted_iota(jnp.int32, gates.shape, 1)
        acc = None
        for k in range(EXPERTS_PER_GROUP):
            a = _dot(h, wg_ref[k])
            u = _dot(h, wu_ref[k])
            ge = jnp.sum(jnp.where(lane == first_expert + k, gates, 0.0), axis=-1, keepdims=True)
            y = _dot((a * _sigmoid(a) * u * ge).astype(BF16), wd_ref[k])
            acc = y if acc is None else acc + y
        for s in range(FEAT_ROWS):
            y_ref[:, s, :] = acc[:, s * LANES:(s + 1) * LANES]

        @pl.when(j + 1 >= n_used)
        def _():
            wait_rows(1 - slot)

    @pl.when(j >= n_used)
    def _():
        y_ref[...] = jnp.zeros_like(y_ref)


def _experts(l, tile_group, n_used, src, slab, w_gate, w_up, w_down):
    tile = lambda j, tg, nv: jnp.minimum(j, nv[0] - 1)
    w_spec = lambda r, c: pl.BlockSpec((None, None, EXPERTS_PER_GROUP, r, c),
                                       lambda j, tg, nv: (l, tg[tile(j, tg, nv)], 0, 0, 0))
    return pl.pallas_call(
        _experts_kernel,
        grid_spec=pltpu.PrefetchScalarGridSpec(
            num_scalar_prefetch=2,
            grid=(N_SORT_TILES,),
            in_specs=[
                pl.BlockSpec((SUBLANES, LANES), lambda j, tg, nv: (j // _SRC_BLOCK_TILES, 0),
                             memory_space=pltpu.SMEM),
                pl.BlockSpec((SUBLANES, LANES),
                             lambda j, tg, nv: (tile(j + 1, tg, nv) // _SRC_BLOCK_TILES, 0),
                             memory_space=pltpu.SMEM),
                pl.BlockSpec(memory_space=pl.ANY),
                w_spec(D_MODEL, D_FF), w_spec(D_MODEL, D_FF), w_spec(D_FF, D_MODEL),
            ],
            out_specs=pl.BlockSpec((SORT_TILE, FEAT_ROWS, LANES), lambda j, tg, nv: (j, 0, 0)),
            scratch_shapes=[pltpu.VMEM((2, SORT_TILE, SLAB_ROWS, LANES), F32),
                            pltpu.SemaphoreType.DMA((2,))],
        ),
        out_shape=jax.ShapeDtypeStruct((N_SORT_TILES * SORT_TILE, FEAT_ROWS, LANES), F32),
        compiler_params=_params("arbitrary"),
        name="experts",
    )(tile_group, n_used, src, src, slab, w_gate, w_up, w_down)


def _combine_kernel(dst_ref, y_ref, x1_ref, mod_ref, o_ref, buf_ref, sem):
    groups = range(PLAN_TILE // LANES)
    for r in groups:
        _start_rows(lambda c, r=r: pltpu.make_async_copy(
            y_ref.at[dst_ref[r, c]], buf_ref.at[r * LANES + c], sem))
    for r in groups:
        pltpu.make_async_copy(y_ref.at[pl.ds(0, LANES)],
                              buf_ref.at[pl.ds(r * LANES, LANES)], sem).wait()
    for s in range(FEAT_ROWS):
        cols = slice(s * LANES, (s + 1) * LANES)
        o_ref[:, cols] = x1_ref[:, cols] + mod_ref[5:6, cols] * buf_ref[:, s, :]


_PLAN_CTX_TILES = N_CTX // PLAN_TILE
_PLAN_LAT_TILES_PER_SEQ = DEC_SEQ // PLAN_TILE


def _plan_cond(i):
    return jnp.where(i < _PLAN_CTX_TILES, 0, 1 + (i - _PLAN_CTX_TILES) // _PLAN_LAT_TILES_PER_SEQ)


def _combine(l, dst, expert_out, x1, mod):
    return pl.pallas_call(
        _combine_kernel,
        grid=(N_TOK // PLAN_TILE,),
        in_specs=[
            pl.BlockSpec((PLAN_TILE // LANES, LANES), lambda i: (i, 0), memory_space=pltpu.SMEM),
            pl.BlockSpec(memory_space=pl.ANY),
            pl.BlockSpec((PLAN_TILE, D_MODEL), lambda i: (i, 0)),
            pl.BlockSpec((None, None, N_MOD, D_MODEL), lambda i: (l, _plan_cond(i), 0, 0)),
        ],
        out_specs=pl.BlockSpec((PLAN_TILE, D_MODEL), lambda i: (i, 0)),
        out_shape=jax.ShapeDtypeStruct((N_TOK, D_MODEL), F32),
        scratch_shapes=[pltpu.VMEM((PLAN_TILE, FEAT_ROWS, LANES), F32), pltpu.SemaphoreType.DMA(())],
        compiler_params=_params("arbitrary"),
        name="combine",
    )(dst, expert_out, x1, mod)


def _rope_tables():
    pos = np.arange(DEC_SEQ)
    inv_freq = ROPE_THETA ** (-np.arange(0, AXIS_DIM, 2, dtype=np.float32) / AXIS_DIM)
    ang_r = (pos // GRID_W).astype(np.float32)[:, None] * inv_freq[None, :]
    ang_c = (pos % GRID_W).astype(np.float32)[:, None] * inv_freq[None, :]
    ang = jnp.asarray(np.concatenate([ang_r, ang_r, ang_c, ang_c], axis=-1).astype(np.float32))
    sign = np.where(np.arange(HEAD_DIM) % AXIS_DIM < AXIS_DIM // 2, -1.0, 1.0).astype(np.float32)
    cos = jnp.concatenate([jnp.cos(ang), jnp.ones((TOK_TILE, HEAD_DIM), F32)], axis=0)
    sin = jnp.concatenate([jnp.sin(ang) * sign, jnp.zeros((TOK_TILE, HEAD_DIM), F32)], axis=0)
    return jnp.tile(cos, (1, N_Q_HEADS)), jnp.tile(sin, (1, N_Q_HEADS))


def _selectors():
    lane = np.arange(N_EXPERTS)
    member = np.stack([(lane[:, None] == (lane[None, :] // EXPERTS_PER_GROUP) * EXPERTS_PER_GROUP + k)
                       for k in range(EXPERTS_PER_GROUP)]).astype(np.float32)
    group = np.stack([np.broadcast_to(lane[:, None] == m * EXPERTS_PER_GROUP, (N_EXPERTS, N_EXPERTS))
                      for m in range(N_GROUPS)]).astype(np.float32)
    head = (np.arange(ATTN_WIDTH)[:, None] // HEAD_DIM == np.arange(ATTN_WIDTH)[None, :] // HEAD_DIM)
    return jnp.asarray(member), jnp.asarray(group), jnp.asarray(head.astype(np.float32), dtype=BF16)


def kernel(x_prompt, x_sample, cache_k, cache_v, state_fwd, state_bwd, c, c_ctx, norm1_w, norm2_w, w_mod, b_mod, w_in, q_norm_w, k_norm_w, hgrn_lb_fwd, hgrn_lb_bwd, hgrn_norm_w, w_br_attn, w_br_hgrn, w_out, w_router, b_router, w_exp_gate, w_exp_up, w_exp_down):
    cos_t, sin_t = _rope_tables()
    sel_member, sel_group, head_ones = _selectors()
    sums_f, masks_f = _hgrn_tables(False)
    sums_b, masks_b = _hgrn_tables(True)
    hgrn_tables = (sums_f, sums_b, masks_f, masks_b)
    cond = jnp.concatenate([c_ctx[None, :], c, jnp.zeros((N_COND - 1 - DEC_BATCH, D_MODEL), F32)], axis=0)
    mod = _modulation(cond, w_mod, b_mod)

    w_in_b = w_in.astype(BF16)
    w_ba_b = w_br_attn.astype(BF16)
    w_bh_b = w_br_hgrn.astype(BF16)
    w_out_b = w_out.astype(BF16)
    by_group = lambda w: w.astype(BF16).reshape(DEPTH, N_GROUPS, EXPERTS_PER_GROUP, *w.shape[2:])
    w_eg_b, w_eu_b, w_ed_b = by_group(w_exp_gate), by_group(w_exp_up), by_group(w_exp_down)
    lane_pad = jnp.eye(N_EXPERTS, LANES, dtype=F32)
    b_router_r = b_router.reshape(1, N_EXPERTS)
    norm1_r = norm1_w.reshape(DEPTH, 1, D_MODEL)
    norm2_r = norm2_w.reshape(DEPTH, 1, D_MODEL)
    hnorm_r = hgrn_norm_w.reshape(DEPTH, 1, HGRN_V)
    qn_r = jnp.tile(q_norm_w, (1, N_Q_HEADS)).reshape(DEPTH, 1, ATTN_WIDTH)
    kn_r = jnp.tile(k_norm_w, (1, N_KV_HEADS)).reshape(DEPTH, 1, KV_WIDTH)

    x = jnp.concatenate([x_prompt.reshape(N_CTX, D_MODEL), x_sample.reshape(N_LAT, D_MODEL)], axis=0)
    ks_out, vs_out, sf_out, sb_out = [], [], [], []
    for l in range(DEPTH):
        q, k, v, h5, gates = _input_projection(l, x, mod, norm1_r, w_in_b, cos_t, sin_t, qn_r, kn_r,
                                               head_ones)
        attn = (_attention_ctx(q, k, v), _attention_lat(l, q, k, v, cache_k, cache_v))
        rec_c, sf, sb = _hgrn(l, h5, 0, BATCH, SEQ, hgrn_lb_fwd, hgrn_lb_bwd, hnorm_r, hgrn_tables)
        rec_s = _hgrn(l, h5, N_CTX, DEC_BATCH, DEC_SEQ, hgrn_lb_fwd, hgrn_lb_bwd, hnorm_r,
                      hgrn_tables, init=(state_fwd, state_bwd))
        rec = (rec_c, rec_s)
        x1, slab, route = _merge(l, attn, rec, gates, x, mod, norm2_r, w_ba_b, w_bh_b, w_out_b,
                                 w_router, b_router_r, sel_member, sel_group, lane_pad)
        dst, tile_group, n_used = _plan(route)
        expert_out = _experts(l, tile_group, n_used, _invert(dst), slab, w_eg_b, w_eu_b, w_ed_b)
        x = _combine(l, dst, expert_out, x1, mod)
        ks_out.append(k[:N_CTX].reshape(BATCH, SEQ, N_KV_HEADS, HEAD_DIM).transpose(0, 2, 1, 3))
        vs_out.append(v[:N_CTX].reshape(BATCH, SEQ, N_KV_HEADS, HEAD_DIM).transpose(0, 2, 1, 3))
        sf_out.append(sf)
        sb_out.append(sb)

    y_prompt = x[:N_CTX].reshape(BATCH, SEQ, D_MODEL)
    y_sample = x[N_CTX:].reshape(DEC_BATCH, DEC_SEQ, D_MODEL)
    return (y_prompt, y_sample, jnp.stack(ks_out, axis=1), jnp.stack(vs_out, axis=1),
            jnp.stack(sf_out, axis=1), jnp.stack(sb_out, axis=1))
```

```python
import functools

import numpy as np
import jax
import jax.numpy as jnp
from jax import lax
from jax.experimental import pallas as pl
from jax.experimental.pallas import tpu as pltpu

F32 = jnp.float32
BF16 = jnp.bfloat16

D_MODEL = 1024
BATCH = 16
SEQ = 256
DEPTH = 4
DEC_BATCH = 4
DEC_SEQ = 2048
PAST_LEN = 512
GRID_W = 64
N_Q_HEADS = 8
N_KV_HEADS = 2
GQA_GROUP = N_Q_HEADS // N_KV_HEADS
HEAD_DIM = 64
AXIS_DIM = HEAD_DIM // 2
ATTN_WIDTH = N_Q_HEADS * HEAD_DIM
KV_WIDTH = N_KV_HEADS * HEAD_DIM
ROPE_THETA = 10000.0
HGRN_HEADS = 4
HGRN_K = 128
HGRN_V = 128
HGRN_WIDTH = HGRN_HEADS * HGRN_K
N_EXPERTS = 16
N_GROUPS = 4
EXPERTS_PER_GROUP = N_EXPERTS // N_GROUPS
D_FF = 512
EPS = 1e-6
IN_COLS = ATTN_WIDTH + 2 * KV_WIDTH + 5 * HGRN_WIDTH + 2 * D_MODEL

N_CTX = BATCH * SEQ
N_LAT = DEC_BATCH * DEC_SEQ
N_TOK = N_CTX + N_LAT
N_COND = 8
N_MOD = 6

C_Q = 0
C_K = C_Q + ATTN_WIDTH
C_V = C_K + KV_WIDTH
C_HQ = C_V + KV_WIDTH
C_FF = C_HQ + HGRN_WIDTH
C_FB = C_FF + HGRN_WIDTH
C_HI = C_FB + HGRN_WIDTH
C_HG = C_HI + HGRN_WIDTH
C_GA = C_HG + HGRN_WIDTH
C_GH = C_GA + D_MODEL

LANES = 128
SUBLANES = 8
FEAT_ROWS = D_MODEL // LANES
SLAB_ROWS = 2 * FEAT_ROWS

TOK_TILE = 512
PLAN_TILE = 1024
SORT_TILE = 512
N_SORT_TILES = N_TOK // SORT_TILE + N_GROUPS
Q_TILE = 512
CHUNK = 64
HGRN_LEVELS = (32, 16, 8, 4, 2, 1)
BLOCK_CHUNKS = 4
LOG2E = 1.4426950408889634
VMEM_LIMIT = 56 * 1024 * 1024

_HI = lax.Precision.HIGHEST


def _sigmoid(x):
    return 1.0 / (1.0 + jnp.exp(-x))


def _dot(a, b):
    return jnp.dot(a, b, preferred_element_type=F32)


def _dot_nt(a, b, precision=None):
    return lax.dot_general(a, b, (((1,), (1,)), ((), ())), precision=precision,
                           preferred_element_type=F32)


def _dot_tn(a, b, precision=None):
    return lax.dot_general(a, b, (((0,), (0,)), ((), ())), precision=precision,
                           preferred_element_type=F32)


def _dot_exact(a, b):
    return jnp.dot(a, b, precision=_HI, preferred_element_type=F32)


def _split_bf16(a):
    hi = a.astype(BF16)
    return hi, (a - hi.astype(F32)).astype(BF16)


def _params(*sem):
    return pltpu.CompilerParams(dimension_semantics=sem, vmem_limit_bytes=VMEM_LIMIT)


def _mod_kernel(cond_ref, w_ref, b_ref, out_ref):
    c = cond_ref[...]
    sc = c * _sigmoid(c)
    out_ref[...] = _dot(sc.astype(BF16), w_ref[...].astype(BF16)) + b_ref[...]


def _modulation(cond, w_mod, b_mod):
    out = pl.pallas_call(
        _mod_kernel,
        grid=(DEPTH, N_MOD),
        in_specs=[
            pl.BlockSpec((N_COND, D_MODEL), lambda l, j: (0, 0)),
            pl.BlockSpec((None, D_MODEL, D_MODEL), lambda l, j: (l, 0, j)),
            pl.BlockSpec((None, 1, D_MODEL), lambda l, j: (l, 0, j)),
        ],
        out_specs=pl.BlockSpec((None, None, N_COND, D_MODEL), lambda l, j: (l, j, 0, 0)),
        out_shape=jax.ShapeDtypeStruct((DEPTH, N_MOD, N_COND, D_MODEL), F32),
        compiler_params=_params("arbitrary", "arbitrary"),
        name="modulation",
    )(cond, w_mod, b_mod.reshape(DEPTH, 1, N_MOD * D_MODEL))
    return out.transpose(0, 2, 1, 3)


def _head_mean_sq(a, bd):
    hi, lo = _split_bf16(a * a)
    return (_dot(hi, bd) + _dot(lo, bd)) * (1.0 / HEAD_DIM)


def _rope(x, cos, sin_signed):
    width = x.shape[-1]
    lane = lax.broadcasted_iota(jnp.int32, x.shape, 1)
    first = (lane & (AXIS_DIM - 1)) < (AXIS_DIM // 2)
    rot = jnp.where(first, pltpu.roll(x, width - AXIS_DIM // 2, 1), pltpu.roll(x, AXIS_DIM // 2, 1))
    return x * cos + rot * sin_signed


def _inproj_kernel(x_ref, mod_ref, n1_ref, w_ref, cos_ref, sin_ref, qn_ref, kn_ref, bd_ref,
                   q_ref, k_ref, v_ref, h_ref, g_ref):
    x = x_ref[...]
    xn = x * lax.rsqrt(jnp.mean(x * x, axis=-1, keepdims=True) + EPS) * n1_ref[...]
    xb = (xn * (1.0 + mod_ref[1:2, :]) + mod_ref[0:1, :]).astype(BF16)

    def proj(c0, width):
        return _dot(xb, w_ref[:, c0:c0 + width])

    a = proj(C_Q, ATTN_WIDTH)
    qn = a * lax.rsqrt(_head_mean_sq(a, bd_ref[...]) + EPS) * qn_ref[...]
    q_ref[...] = (_rope(qn, cos_ref[...], sin_ref[...]) * (HEAD_DIM ** -0.5)).astype(BF16)

    a = proj(C_K, KV_WIDTH)
    kn = a * lax.rsqrt(_head_mean_sq(a, bd_ref[:KV_WIDTH, :KV_WIDTH]) + EPS) * kn_ref[...]
    k_ref[...] = _rope(kn, cos_ref[:, :KV_WIDTH], sin_ref[:, :KV_WIDTH])
    v_ref[...] = proj(C_V, KV_WIDTH)

    a = proj(C_HQ, HGRN_WIDTH)
    h_ref[:, 0:HGRN_WIDTH] = a * _sigmoid(a) * (HGRN_K ** -0.5)
    h_ref[:, HGRN_WIDTH:2 * HGRN_WIDTH] = proj(C_FF, HGRN_WIDTH)
    h_ref[:, 2 * HGRN_WIDTH:3 * HGRN_WIDTH] = proj(C_FB, HGRN_WIDTH)
    h_ref[:, 3 * HGRN_WIDTH:4 * HGRN_WIDTH] = proj(C_HI, HGRN_WIDTH)
    a = proj(C_HG, HGRN_WIDTH)
    h_ref[:, 4 * HGRN_WIDTH:5 * HGRN_WIDTH] = a * _sigmoid(a)

    g_ref[:, 0:D_MODEL] = _sigmoid(proj(C_GA, D_MODEL))
    g_ref[:, D_MODEL:2 * D_MODEL] = _sigmoid(proj(C_GH, D_MODEL))


_CTX_TILES = N_CTX // TOK_TILE
_LAT_TILES_PER_SEQ = DEC_SEQ // TOK_TILE


def _tile_cond(i):
    return jnp.where(i < _CTX_TILES, 0, 1 + (i - _CTX_TILES) // _LAT_TILES_PER_SEQ)


def _tile_rope_block(i):
    return jnp.where(i < _CTX_TILES, _LAT_TILES_PER_SEQ, (i - _CTX_TILES) % _LAT_TILES_PER_SEQ)


def _input_projection(l, x, mod, norm1_w, w_in, cos_t, sin_t, qn_w, kn_w, bd):
    row = lambda w: pl.BlockSpec((TOK_TILE, w), lambda i: (i, 0))
    return pl.pallas_call(
        _inproj_kernel,
        grid=(N_TOK // TOK_TILE,),
        in_specs=[
            row(D_MODEL),
            pl.BlockSpec((None, None, N_MOD, D_MODEL), lambda i: (l, _tile_cond(i), 0, 0)),
            pl.BlockSpec((None, 1, D_MODEL), lambda i: (l, 0, 0)),
            pl.BlockSpec((None, D_MODEL, IN_COLS), lambda i: (l, 0, 0),
                         pipeline_mode=pl.Buffered(1)),
            pl.BlockSpec((TOK_TILE, ATTN_WIDTH), lambda i: (_tile_rope_block(i), 0)),
            pl.BlockSpec((TOK_TILE, ATTN_WIDTH), lambda i: (_tile_rope_block(i), 0)),
            pl.BlockSpec((None, 1, ATTN_WIDTH), lambda i: (l, 0, 0)),
            pl.BlockSpec((None, 1, KV_WIDTH), lambda i: (l, 0, 0)),
            pl.BlockSpec((ATTN_WIDTH, ATTN_WIDTH), lambda i: (0, 0)),
        ],
        out_specs=[row(ATTN_WIDTH), row(KV_WIDTH), row(KV_WIDTH), row(5 * HGRN_WIDTH),
                   row(2 * D_MODEL)],
        out_shape=[
            jax.ShapeDtypeStruct((N_TOK, ATTN_WIDTH), BF16),
            jax.ShapeDtypeStruct((N_TOK, KV_WIDTH), F32),
            jax.ShapeDtypeStruct((N_TOK, KV_WIDTH), F32),
            jax.ShapeDtypeStruct((N_TOK, 5 * HGRN_WIDTH), F32),
            jax.ShapeDtypeStruct((N_TOK, 2 * D_MODEL), F32),
        ],
        compiler_params=_params("arbitrary"),
        name="input_projection",
    )(x, mod, norm1_w, w_in, cos_t, sin_t, qn_w, kn_w, bd)


def _softmax_pv(scores, values):
    m = scores[0].max(axis=-1, keepdims=True)
    for s in scores[1:]:
        m = jnp.maximum(m, s.max(axis=-1, keepdims=True))
    num, den = None, None
    for s, v in zip(scores, values):
        p = jnp.exp(s - m)
        d = p.sum(axis=-1, keepdims=True)
        o = _dot(p.astype(BF16), v)
        num = o if num is None else num + o
        den = d if den is None else den + d
    return num / den


def _attn_ctx_kernel(q_ref, k_ref, v_ref, o_ref):
    for g in range(N_KV_HEADS):
        cols = slice(g * HEAD_DIM, (g + 1) * HEAD_DIM)
        kg = k_ref[:, cols].astype(BF16)
        vg = v_ref[:, cols].astype(BF16)
        for hh in range(GQA_GROUP):
            h = g * GQA_GROUP + hh
            hc = slice(h * HEAD_DIM, (h + 1) * HEAD_DIM)
            o = _softmax_pv([_dot_nt(q_ref[:, hc], kg)], [vg])
            o_ref[:, hc] = o.astype(BF16)


def _attention_ctx(q, k, v):
    return pl.pallas_call(
        _attn_ctx_kernel,
        grid=(BATCH,),
        in_specs=[
            pl.BlockSpec((SEQ, ATTN_WIDTH), lambda b: (b, 0)),
            pl.BlockSpec((SEQ, KV_WIDTH), lambda b: (b, 0)),
            pl.BlockSpec((SEQ, KV_WIDTH), lambda b: (b, 0)),
        ],
        out_specs=pl.BlockSpec((SEQ, ATTN_WIDTH), lambda b: (b, 0)),
        out_shape=jax.ShapeDtypeStruct((N_CTX, ATTN_WIDTH), BF16),
        compiler_params=_params("arbitrary"),
        name="attention_ctx",
    )(q, k, v)


def _attn_lat_kernel(q_ref, k_ref, v_ref, ck_ref, cv_ref, o_ref):
    for g in range(N_KV_HEADS):
        cols = slice(g * HEAD_DIM, (g + 1) * HEAD_DIM)
        kg = k_ref[:, cols].astype(BF16)
        vg = v_ref[:, cols].astype(BF16)
        ckg = ck_ref[g].astype(BF16)
        cvg = cv_ref[g].astype(BF16)
        for hh in range(GQA_GROUP):
            h = g * GQA_GROUP + hh
            hc = slice(h * HEAD_DIM, (h + 1) * HEAD_DIM)
            qh = q_ref[:, hc]
            o = _softmax_pv([_dot_nt(qh, kg), _dot_nt(qh, ckg)], [vg, cvg])
            o_ref[:, hc] = o.astype(BF16)


def _attention_lat(l, q, k, v, cache_k, cache_v):
    q_blocks = DEC_SEQ // Q_TILE
    q_off = N_CTX // Q_TILE
    kv_off = N_CTX // DEC_SEQ
    cache_spec = pl.BlockSpec((None, None, N_KV_HEADS, PAST_LEN, HEAD_DIM),
                              lambda b, i: (b, l, 0, 0, 0))
    return pl.pallas_call(
        _attn_lat_kernel,
        grid=(DEC_BATCH, q_blocks),
        in_specs=[
            pl.BlockSpec((Q_TILE, ATTN_WIDTH), lambda b, i: (q_off + b * q_blocks + i, 0)),
            pl.BlockSpec((DEC_SEQ, KV_WIDTH), lambda b, i: (kv_off + b, 0)),
            pl.BlockSpec((DEC_SEQ, KV_WIDTH), lambda b, i: (kv_off + b, 0)),
            cache_spec,
            cache_spec,
        ],
        out_specs=pl.BlockSpec((Q_TILE, ATTN_WIDTH), lambda b, i: (b * q_blocks + i, 0)),
        out_shape=jax.ShapeDtypeStruct((N_LAT, ATTN_WIDTH), BF16),
        compiler_params=_params("arbitrary", "arbitrary"),
        name="attention_lat",
    )(q, k, v, cache_k, cache_v)


def _lower_bound(lb_ref, l):
    z = lb_ref[...]
    e = jnp.exp(z - z.max(axis=0, keepdims=True))
    p = e / e.sum(axis=0, keepdims=True)
    acc = p[0:1] * 0.0
    for r in range(1, l + 1):
        acc = acc + p[r:r + 1]
    return acc


def _hgrn_tables(rev):
    t = np.arange(CHUNK)
    tt, uu = t[:, None], t[None, :]
    groups = []
    masks = []
    for h in HGRN_LEVELS:
        right = (t & h) != 0
        edge = ((t // (2 * h)) * 2 * h + h - 1)[:, None]
        groups.append(np.where(right[:, None], (uu > edge) & (uu <= tt), (uu > tt) & (uu <= edge)))
        same = (tt // (2 * h)) == (uu // (2 * h))
        masks.append(same & right[:, None] & ~right[None, :])
    masks.append(tt == uu)
    groups += [uu <= tt, uu > tt]
    sums = np.stack(groups).astype(np.float32)
    masks = np.stack(masks).astype(np.float32)
    if rev:
        sums = sums[:, ::-1, ::-1]
        masks = masks[:, ::-1, ::-1]
    sums = sums.reshape(-1, CHUNK)
    return (jnp.asarray(np.concatenate([sums, sums], axis=1), dtype=BF16),
            jnp.asarray(np.ascontiguousarray(masks)))


def _hgrn_block(q, x, v, lb, sums_ref, masks_ref, st_ref, rev):
    n = q.shape[0] // CHUNK
    f = lb + (1.0 - lb) * _sigmoid(x)
    kk = 1.0 - f
    g = jnp.log(f) * LOG2E
    g_hi, g_lo = _split_bf16(g)
    vb = v.astype(BF16)
    rid = lax.broadcasted_iota(jnp.int32, (CHUNK, HGRN_K), 0)
    edge = 0 if rev else CHUNK - 1
    n_lev = len(HGRN_LEVELS)
    outs = [None] * n
    for c in (reversed(range(n)) if rev else range(n)):
        rs = slice(c * CHUNK, (c + 1) * CHUNK)
        e = jnp.exp2(_dot(sums_ref[...], jnp.concatenate([g_hi[rs], g_lo[rs]], axis=0)))
        q_c, k_c, v_c = q[rs], kk[rs], vb[rs]
        a = _dot_nt(q_c.astype(BF16), k_c.astype(BF16)) * masks_ref[n_lev]
        for i, h in enumerate(HGRN_LEVELS):
            later = ((rid & h) == 0) if rev else ((rid & h) != 0)
            z = (jnp.where(later, q_c, k_c) * e[i * CHUNK:(i + 1) * CHUNK]).astype(BF16)
            a = a + _dot_nt(z, z) * masks_ref[i]
        e_in = e[n_lev * CHUNK:(n_lev + 1) * CHUNK]
        e_out = e[(n_lev + 1) * CHUNK:(n_lev + 2) * CHUNK]
        st = st_ref[...]
        outs[c] = (_dot(a.astype(BF16), v_c)
                   + _dot_nt((q_c * e_in).astype(BF16), st.astype(BF16)))
        k_dec = (k_c * e_out).astype(BF16)
        st_ref[...] = st * e_in[edge:edge + 1] + _dot_tn(v_c, k_dec)
    return jnp.concatenate(outs, axis=0)


def _hgrn_kernel(l, seq_len, has_init, *refs):
    (hq_ref, ff_ref, fb_ref, hi_ref, hg_ref, lbf_ref, lbb_ref, hn_ref,
     sumf_ref, sumb_ref, mskf_ref, mskb_ref) = refs[:12]
    refs = refs[12:]
    if has_init:
        s0f_ref, s0b_ref, rec_ref, of_ref, ob_ref, stf_ref, stb_ref = refs
    else:
        rec_ref, sf_ref, sb_ref, of_ref, ob_ref, stf_ref, stb_ref = refs
    lb_f = _lower_bound(lbf_ref, l)
    lb_b = _lower_bound(lbb_ref, l)
    if has_init:
        stf_ref[...] = s0f_ref[...].T
        stb_ref[...] = s0b_ref[...].T
    else:
        stf_ref[...] = jnp.zeros((HGRN_V, HGRN_K), F32)
        stb_ref[...] = jnp.zeros((HGRN_V, HGRN_K), F32)
    rows = BLOCK_CHUNKS * CHUNK
    n_blocks = seq_len // rows

    def step(rf, rb):
        of_ref[rf, :] = _hgrn_block(hq_ref[rf, :], ff_ref[rf, :], hi_ref[rf, :], lb_f,
                                    sumf_ref, mskf_ref, stf_ref, False)
        ob_ref[rb, :] = _hgrn_block(hq_ref[rb, :], fb_ref[rb, :], hi_ref[rb, :], lb_b,
                                    sumb_ref, mskb_ref, stb_ref, True)

    if n_blocks == 1:
        step(pl.ds(0, rows), pl.ds(0, rows))
    else:
        def body(i, carry):
            step(pl.ds(pl.multiple_of(i * rows, rows), rows),
                 pl.ds(pl.multiple_of((n_blocks - 1 - i) * rows, rows), rows))
            return carry

        lax.fori_loop(0, n_blocks, body, 0)
    o = of_ref[...] + ob_ref[...]
    o = o * lax.rsqrt(jnp.mean(o * o, axis=-1, keepdims=True) + EPS) * hn_ref[...]
    rec_ref[...] = o * hg_ref[...]
    if not has_init:
        sf_ref[...] = stf_ref[...].T
        sb_ref[...] = stb_ref[...].T


def _hgrn(l, h5, row_off, n_seq, seq_len, lb_fwd, lb_bwd, hnorm_w, tables, init=None):
    blocks_off = row_off // seq_len
    part = lambda k: pl.BlockSpec((seq_len, HGRN_K),
                                  lambda b, h: (blocks_off + b, k * HGRN_HEADS + h))
    lb_spec = pl.BlockSpec((DEPTH, HGRN_K), lambda b, h: (0, h))
    sum_spec = pl.BlockSpec((8 * CHUNK, 2 * CHUNK), lambda b, h: (0, 0))
    msk_spec = pl.BlockSpec((len(HGRN_LEVELS) + 1, CHUNK, CHUNK), lambda b, h: (0, 0, 0))
    in_specs = [part(0), part(1), part(2), part(3), part(4), lb_spec, lb_spec,
                pl.BlockSpec((None, 1, HGRN_V), lambda b, h: (l, 0, 0)),
                sum_spec, sum_spec, msk_spec, msk_spec]
    args = [h5, h5, h5, h5, h5, lb_fwd, lb_bwd, hnorm_w, *tables]
    rec_spec = pl.BlockSpec((seq_len, HGRN_V), lambda b, h: (b, h))
    rec_shape = jax.ShapeDtypeStruct((n_seq * seq_len, HGRN_WIDTH), F32)
    if init is not None:
        st_spec = pl.BlockSpec((None, None, None, HGRN_K, HGRN_V), lambda b, h: (b, l, h, 0, 0))
        in_specs += [st_spec, st_spec]
        args += list(init)
        out_specs, out_shape = rec_spec, rec_shape
    else:
        st_spec = pl.BlockSpec((None, None, HGRN_K, HGRN_V), lambda b, h: (b, h, 0, 0))
        st_shape = jax.ShapeDtypeStruct((n_seq, HGRN_HEADS, HGRN_K, HGRN_V), F32)
        out_specs, out_shape = [rec_spec, st_spec, st_spec], [rec_shape, st_shape, st_shape]
    return pl.pallas_call(
        functools.partial(_hgrn_kernel, l, seq_len, init is not None),
        grid=(n_seq, HGRN_HEADS),
        in_specs=in_specs,
        out_specs=out_specs,
        out_shape=out_shape,
        scratch_shapes=[pltpu.VMEM((seq_len, HGRN_V), F32), pltpu.VMEM((seq_len, HGRN_V), F32),
                        pltpu.VMEM((HGRN_V, HGRN_K), F32), pltpu.VMEM((HGRN_V, HGRN_K), F32)],
        compiler_params=_params("arbitrary", "arbitrary"),
        name="hgrn_lat" if init is not None else "hgrn_ctx",
    )(*args)


def _route(scores, bias, sel_member, sel_group):
    sel = scores + bias
    members = [_dot_exact(sel, sel_member[k]) for k in range(EXPERTS_PER_GROUP)]
    pair = None
    for i in range(EXPERTS_PER_GROUP):
        for j in range(i + 1, EXPERTS_PER_GROUP):
            s = members[i] + members[j]
            pair = s if pair is None else jnp.maximum(pair, s)
    lane = lax.broadcasted_iota(jnp.int32, sel.shape, 1)
    best = _dot_exact(pair, sel_group[0])
    best_g = jnp.zeros(sel.shape, jnp.int32)
    for m in range(1, N_GROUPS):
        cand = _dot_exact(pair, sel_group[m])
        upd = cand > best
        best_g = jnp.where(upd, m, best_g)
        best = jnp.where(upd, cand, best)
    in_group = (lane >> 2) == best_g
    pos = lane & (EXPERTS_PER_GROUP - 1)
    rank = jnp.zeros(sel.shape, jnp.int32)
    for k in range(EXPERTS_PER_GROUP):
        ahead = (members[k] > sel) | ((members[k] == sel) & (k < pos))
        rank = rank + jnp.where(ahead, 1, 0)
    w = jnp.where(in_group & (rank < 2), scores, 0.0)
    return w / w.sum(axis=-1, keepdims=True)


def _merge_kernel(attn_c_ref, attn_s_ref, rec_c_ref, rec_s_ref, g_ref, x_ref, mod_ref, n2_ref,
                  wa_ref, wh_ref, wo_ref, wr_ref, br_ref, selm_ref, selg_ref, pad_ref,
                  x1_ref, slab_ref, route_ref):
    is_ctx = pl.program_id(0) < _CTX_TILES
    attn = jnp.where(is_ctx, attn_c_ref[...], attn_s_ref[...])
    rec = jnp.where(is_ctx, rec_c_ref[...], rec_s_ref[...])
    ya = _dot(attn, wa_ref[...])
    yh = _dot(rec.astype(BF16), wh_ref[...])
    merged = g_ref[:, 0:D_MODEL] * ya + g_ref[:, D_MODEL:2 * D_MODEL] * yh
    out = _dot(merged.astype(BF16), wo_ref[...])
    x1 = x_ref[...] + mod_ref[2:3, :] * out
    x1_ref[...] = x1
    xn = x1 * lax.rsqrt(jnp.mean(x1 * x1, axis=-1, keepdims=True) + EPS) * n2_ref[...]
    h2 = xn * (1.0 + mod_ref[4:5, :]) + mod_ref[3:4, :]
    h_hi, h_lo = _split_bf16(h2)
    w_hi, w_lo = _split_bf16(wr_ref[...])
    scores = _sigmoid(_dot(jnp.concatenate([h_hi, h_lo, h_hi], axis=1),
                           jnp.concatenate([w_hi, w_hi, w_lo], axis=0)))
    gates = _route(scores, br_ref[...], [selm_ref[k] for k in range(EXPERTS_PER_GROUP)],
                   [selg_ref[m] for m in range(N_GROUPS)])
    gates = _dot_exact(gates, pad_ref[...])
    route_ref[...] = gates
    for s in range(FEAT_ROWS):
        slab_ref[:, s, :] = h2[:, s * LANES:(s + 1) * LANES]
    slab_ref[:, FEAT_ROWS, :] = gates
    for s in range(FEAT_ROWS + 1, SLAB_ROWS):
        slab_ref[:, s, :] = jnp.zeros_like(gates)


def _merge(l, attn, rec, gates, x, mod, norm2_w, w_br_attn, w_br_hgrn, w_out, w_router, b_router,
           sel_member, sel_group, lane_pad):
    row = lambda w: pl.BlockSpec((TOK_TILE, w), lambda i: (i, 0))
    ctx_row = lambda w: pl.BlockSpec((TOK_TILE, w), lambda i: (jnp.minimum(i, _CTX_TILES - 1), 0))
    lat_row = lambda w: pl.BlockSpec((TOK_TILE, w), lambda i: (jnp.maximum(i - _CTX_TILES, 0), 0))
    layer = lambda r, c: pl.BlockSpec((None, r, c), lambda i: (l, 0, 0))
    full = lambda *s: pl.BlockSpec(s, lambda i: (0,) * len(s))
    return pl.pallas_call(
        _merge_kernel,
        grid=(N_TOK // TOK_TILE,),
        in_specs=[
            ctx_row(ATTN_WIDTH), lat_row(ATTN_WIDTH), ctx_row(HGRN_WIDTH), lat_row(HGRN_WIDTH),
            row(2 * D_MODEL), row(D_MODEL),
            pl.BlockSpec((None, None, N_MOD, D_MODEL), lambda i: (l, _tile_cond(i), 0, 0)),
            layer(1, D_MODEL),
            layer(ATTN_WIDTH, D_MODEL), layer(HGRN_WIDTH, D_MODEL), layer(D_MODEL, D_MODEL),
            full(D_MODEL, N_EXPERTS), full(1, N_EXPERTS),
            full(EXPERTS_PER_GROUP, N_EXPERTS, N_EXPERTS), full(N_GROUPS, N_EXPERTS, N_EXPERTS),
            full(N_EXPERTS, LANES),
        ],
        out_specs=[row(D_MODEL),
                   pl.BlockSpec((TOK_TILE, SLAB_ROWS, LANES), lambda i: (i, 0, 0)),
                   row(LANES)],
        out_shape=[
            jax.ShapeDtypeStruct((N_TOK, D_MODEL), F32),
            jax.ShapeDtypeStruct((N_TOK, SLAB_ROWS, LANES), F32),
            jax.ShapeDtypeStruct((N_TOK, LANES), F32),
        ],
        compiler_params=_params("arbitrary"),
        name="merge_router",
    )(*attn, *rec, gates, x, mod, norm2_w, w_br_attn, w_br_hgrn, w_out, w_router, b_router,
      sel_member, sel_group, lane_pad)


def _plan_kernel(route_ref, dst_ref, meta_ref, tot_ref, run_ref):
    p, i = pl.program_id(0), pl.program_id(1)

    @pl.when(jnp.logical_and(p == 0, i == 0))
    def _():
        tot_ref[...] = jnp.zeros_like(tot_ref)

    @pl.when(i == 0)
    def _():
        run_ref[...] = jnp.zeros_like(run_ref)

    grp = lax.broadcasted_iota(jnp.int32, (SUBLANES, LANES), 0)
    lane = lax.broadcasted_iota(jnp.int32, (SUBLANES, LANES), 1)
    member = jnp.where(((lane >> 2) == grp) & (lane < N_EXPERTS), 1.0, 0.0)
    one_hot = jnp.where(_dot_nt(member, route_ref[...], precision=_HI) > 0.0, 1.0, 0.0)
    src = lax.broadcasted_iota(jnp.int32, (PLAN_TILE, PLAN_TILE), 0)
    tgt = lax.broadcasted_iota(jnp.int32, (PLAN_TILE, PLAN_TILE), 1)
    before = jnp.where(src < tgt, 1.0, 0.0).astype(BF16)
    rank = _dot(one_hot.astype(BF16), before) + run_ref[:, 0:1]
    tile_tot = one_hot.sum(axis=1, keepdims=True)

    @pl.when(p == 0)
    def _():
        tot_ref[...] += tile_tot

    @pl.when(p == 1)
    def _():
        run_ref[...] += tile_tot
        tot = tot_ref[:, 0:1]
        n_tiles = jnp.floor((tot + (SORT_TILE - 1)) * (1.0 / SORT_TILE))
        offs, acc = [], jnp.zeros((1, 1), F32)
        for m in range(N_GROUPS):
            offs.append(acc * SORT_TILE)
            acc = acc + n_tiles[m:m + 1]
        dst = sum(one_hot[m:m + 1] * (offs[m] + rank[m:m + 1]) for m in range(N_GROUPS))
        for r in range(PLAN_TILE // LANES):
            dst_ref[r:r + 1, :] = dst[:, r * LANES:(r + 1) * LANES].astype(jnp.int32)
        tile_row = (lax.broadcasted_iota(jnp.int32, (1, LANES), 1) * SORT_TILE).astype(F32)
        tile_group = sum(jnp.where(tile_row >= offs[m], 1.0, 0.0) for m in range(1, N_GROUPS))
        rows = [tile_group, jnp.broadcast_to(acc, (1, LANES))]
        rows += [jnp.zeros((1, LANES), F32)] * (SUBLANES - len(rows))
        meta_ref[...] = jnp.concatenate(rows, axis=0).astype(jnp.int32)


def _plan(route):
    n_blocks = N_TOK // PLAN_TILE
    dst, meta = pl.pallas_call(
        _plan_kernel,
        grid=(2, n_blocks),
        in_specs=[pl.BlockSpec((PLAN_TILE, LANES), lambda p, i: (i, 0))],
        out_specs=[pl.BlockSpec((PLAN_TILE // LANES, LANES), lambda p, i: (i * p, 0)),
                   pl.BlockSpec((SUBLANES, LANES), lambda p, i: (0, 0))],
        out_shape=[jax.ShapeDtypeStruct((N_TOK // LANES, LANES), jnp.int32),
                   jax.ShapeDtypeStruct((SUBLANES, LANES), jnp.int32)],
        scratch_shapes=[pltpu.VMEM((SUBLANES, LANES), F32), pltpu.VMEM((SUBLANES, LANES), F32)],
        compiler_params=_params("arbitrary", "arbitrary"),
        name="route_plan",
    )(route)
    return dst, meta[0, :N_SORT_TILES], meta[1, 0:1]


def _start_rows(make_copy):
    def start(i, carry):
        make_copy(2 * i).start(priority=0)
        make_copy(2 * i + 1).start(priority=1)
        return carry

    lax.fori_loop(0, LANES // 2, start, 0, unroll=4)


def _invert_kernel(dst_ref, zero_ref, src_ref, sem):
    i = pl.program_id(0)

    @pl.when(i == 0)
    def _():
        clear = pltpu.make_async_copy(zero_ref, src_ref, sem)
        clear.start()
        clear.wait()

    for r in range(PLAN_TILE // LANES):
        def put(c, carry, r=r):
            d = dst_ref[r, c]
            src_ref[lax.shift_right_logical(d, 7), d & (LANES - 1)] = i * PLAN_TILE + r * LANES + c
            return carry

        lax.fori_loop(0, LANES, put, 0, unroll=8)


def _invert(dst):
    shape = (N_SORT_TILES * SORT_TILE // LANES, LANES)
    return pl.pallas_call(
        _invert_kernel,
        grid=(N_TOK // PLAN_TILE,),
        in_specs=[pl.BlockSpec((PLAN_TILE // LANES, LANES), lambda i: (i, 0), memory_space=pltpu.SMEM),
                  pl.BlockSpec(memory_space=pl.ANY)],
        out_specs=pl.BlockSpec(shape, lambda i: (0, 0), memory_space=pltpu.SMEM),
        out_shape=jax.ShapeDtypeStruct(shape, jnp.int32),
        scratch_shapes=[pltpu.SemaphoreType.DMA(())],
        compiler_params=_params("arbitrary"),
        name="invert_plan",
    )(dst, jnp.zeros(shape, jnp.int32))


_SRC_BLOCK_TILES = SUBLANES * LANES // SORT_TILE


def _experts_kernel(tg_ref, nv_ref, src_ref, nxt_ref, slab_ref, wg_ref, wu_ref, wd_ref, y_ref,
                    x_ref, h_ref, acc_ref, sem):
    j, k = pl.program_id(0), pl.program_id(1)
    n_used = nv_ref[0]
    groups = SORT_TILE // LANES

    def gather_group(idx_ref, tile, slot, r):
        first_row = (tile % _SRC_BLOCK_TILES) * groups
        _start_rows(lambda c: pltpu.make_async_copy(
            slab_ref.at[idx_ref[first_row + r, c]], x_ref.at[slot, r * LANES + c], sem.at[slot]))

    def wait_rows(slot):
        for r in range(groups):
            pltpu.make_async_copy(slab_ref.at[pl.ds(0, LANES)],
                                  x_ref.at[slot, pl.ds(r * LANES, LANES)], sem.at[slot]).wait()

    @pl.when(jnp.logical_and(j == 0, k == 0))
    def _():
        for r in range(groups):
            gather_group(src_ref, j, 0, r)

    @pl.when(j < n_used)
    def _():
        slot = j % 2

        @pl.when(k == 0)
        def _():
            wait_rows(slot)
            h_ref[...] = jnp.concatenate([x_ref[slot, :, s, :] for s in range(FEAT_ROWS)],
                                         axis=1).astype(BF16)

        gather_group(nxt_ref, jnp.minimum(j + 1, n_used - 1), 1 - slot, k)
        h = h_ref[...]
        gates = x_ref[slot, :, FEAT_ROWS, :]
        lane = lax.broadcasted_iota(jnp.int32, gates.shape, 1)
        a = _dot(h, wg_ref[k])
        u = _dot(h, wu_ref[k])
        expert = tg_ref[j] * EXPERTS_PER_GROUP + k
        ge = jnp.sum(jnp.where(lane == expert, gates, 0.0), axis=-1, keepdims=True)
        y = _dot((a * _sigmoid(a) * u * ge).astype(BF16), wd_ref[k])

        @pl.when(k == 0)
        def _():
            acc_ref[...] = y

        @pl.when(k > 0)
        def _():
            acc_ref[...] += y

        @pl.when(k == EXPERTS_PER_GROUP - 1)
        def _():
            for s in range(FEAT_ROWS):
                y_ref[:, s, :] = acc_ref[:, s * LANES:(s + 1) * LANES]

            @pl.when(j + 1 >= n_used)
            def _():
                wait_rows(1 - slot)

    @pl.when(j >= n_used)
    def _():
        y_ref[...] = jnp.zeros_like(y_ref)


def _experts(l, tile_group, n_used, src, slab, w_gate, w_up, w_down):
    tile = lambda j, tg, nv: jnp.minimum(j, nv[0] - 1)
    w_spec = lambda r, c: pl.BlockSpec((None, None, EXPERTS_PER_GROUP, r, c),
                                       lambda j, k, tg, nv: (l, tg[tile(j, tg, nv)], 0, 0, 0))
    assert SORT_TILE // LANES == EXPERTS_PER_GROUP
    return pl.pallas_call(
        _experts_kernel,
        grid_spec=pltpu.PrefetchScalarGridSpec(
            num_scalar_prefetch=2,
            grid=(N_SORT_TILES, EXPERTS_PER_GROUP),
            in_specs=[
                pl.BlockSpec((SUBLANES, LANES), lambda j, k, tg, nv: (j // _SRC_BLOCK_TILES, 0),
                             memory_space=pltpu.SMEM),
                pl.BlockSpec((SUBLANES, LANES),
                             lambda j, k, tg, nv: (tile(j + 1, tg, nv) // _SRC_BLOCK_TILES, 0),
                             memory_space=pltpu.SMEM),
                pl.BlockSpec(memory_space=pl.ANY),
                w_spec(D_MODEL, D_FF), w_spec(D_MODEL, D_FF), w_spec(D_FF, D_MODEL),
            ],
            out_specs=pl.BlockSpec((SORT_TILE, FEAT_ROWS, LANES), lambda j, k, tg, nv: (j, 0, 0)),
            scratch_shapes=[pltpu.VMEM((2, SORT_TILE, SLAB_ROWS, LANES), F32),
                            pltpu.VMEM((SORT_TILE, D_MODEL), BF16),
                            pltpu.VMEM((SORT_TILE, D_MODEL), F32),
                            pltpu.SemaphoreType.DMA((2,))],
        ),
        out_shape=jax.ShapeDtypeStruct((N_SORT_TILES * SORT_TILE, FEAT_ROWS, LANES), F32),
        compiler_params=_params("arbitrary", "arbitrary"),
        name="experts",
    )(tile_group, n_used, src, src, slab, w_gate, w_up, w_down)


def _combine_kernel(dst_ref, y_ref, x1_ref, mod_ref, o_ref, buf_ref, sem):
    groups = range(PLAN_TILE // LANES)
    for r in groups:
        _start_rows(lambda c, r=r: pltpu.make_async_copy(
            y_ref.at[dst_ref[r, c]], buf_ref.at[r * LANES + c], sem))
    for r in groups:
        pltpu.make_async_copy(y_ref.at[pl.ds(0, LANES)],
                              buf_ref.at[pl.ds(r * LANES, LANES)], sem).wait()
    for s in range(FEAT_ROWS):
        cols = slice(s * LANES, (s + 1) * LANES)
        o_ref[:, cols] = x1_ref[:, cols] + mod_ref[5:6, cols] * buf_ref[:, s, :]


_PLAN_CTX_TILES = N_CTX // PLAN_TILE
_PLAN_LAT_TILES_PER_SEQ = DEC_SEQ // PLAN_TILE


def _plan_cond(i):
    return jnp.where(i < _PLAN_CTX_TILES, 0, 1 + (i - _PLAN_CTX_TILES) // _PLAN_LAT_TILES_PER_SEQ)


def _combine(l, dst, expert_out, x1, mod):
    return pl.pallas_call(
        _combine_kernel,
        grid=(N_TOK // PLAN_TILE,),
        in_specs=[
            pl.BlockSpec((PLAN_TILE // LANES, LANES), lambda i: (i, 0), memory_space=pltpu.SMEM),
            pl.BlockSpec(memory_space=pl.ANY),
            pl.BlockSpec((PLAN_TILE, D_MODEL), lambda i: (i, 0)),
            pl.BlockSpec((None, None, N_MOD, D_MODEL), lambda i: (l, _plan_cond(i), 0, 0)),
        ],
        out_specs=pl.BlockSpec((PLAN_TILE, D_MODEL), lambda i: (i, 0)),
        out_shape=jax.ShapeDtypeStruct((N_TOK, D_MODEL), F32),
        scratch_shapes=[pltpu.VMEM((PLAN_TILE, FEAT_ROWS, LANES), F32), pltpu.SemaphoreType.DMA(())],
        compiler_params=_params("arbitrary"),
        name="combine",
    )(dst, expert_out, x1, mod)


def _rope_tables():
    pos = np.arange(DEC_SEQ)
    inv_freq = ROPE_THETA ** (-np.arange(0, AXIS_DIM, 2, dtype=np.float32) / AXIS_DIM)
    ang_r = (pos // GRID_W).astype(np.float32)[:, None] * inv_freq[None, :]
    ang_c = (pos % GRID_W).astype(np.float32)[:, None] * inv_freq[None, :]
    ang = jnp.asarray(np.concatenate([ang_r, ang_r, ang_c, ang_c], axis=-1).astype(np.float32))
    sign = np.where(np.arange(HEAD_DIM) % AXIS_DIM < AXIS_DIM // 2, -1.0, 1.0).astype(np.float32)
    cos = jnp.concatenate([jnp.cos(ang), jnp.ones((TOK_TILE, HEAD_DIM), F32)], axis=0)
    sin = jnp.concatenate([jnp.sin(ang) * sign, jnp.zeros((TOK_TILE, HEAD_DIM), F32)], axis=0)
    return jnp.tile(cos, (1, N_Q_HEADS)), jnp.tile(sin, (1, N_Q_HEADS))


def _selectors():
    lane = np.arange(N_EXPERTS)
    member = np.stack([(lane[:, None] == (lane[None, :] // EXPERTS_PER_GROUP) * EXPERTS_PER_GROUP + k)
                       for k in range(EXPERTS_PER_GROUP)]).astype(np.float32)
    group = np.stack([np.broadcast_to(lane[:, None] == m * EXPERTS_PER_GROUP, (N_EXPERTS, N_EXPERTS))
                      for m in range(N_GROUPS)]).astype(np.float32)
    head = (np.arange(ATTN_WIDTH)[:, None] // HEAD_DIM == np.arange(ATTN_WIDTH)[None, :] // HEAD_DIM)
    return jnp.asarray(member), jnp.asarray(group), jnp.asarray(head.astype(np.float32), dtype=BF16)


def kernel(x_prompt, x_sample, cache_k, cache_v, state_fwd, state_bwd, c, c_ctx, norm1_w, norm2_w, w_mod, b_mod, w_in, q_norm_w, k_norm_w, hgrn_lb_fwd, hgrn_lb_bwd, hgrn_norm_w, w_br_attn, w_br_hgrn, w_out, w_router, b_router, w_exp_gate, w_exp_up, w_exp_down):
    cos_t, sin_t = _rope_tables()
    sel_member, sel_group, head_ones = _selectors()
    sums_f, masks_f = _hgrn_tables(False)
    sums_b, masks_b = _hgrn_tables(True)
    hgrn_tables = (sums_f, sums_b, masks_f, masks_b)
    cond = jnp.concatenate([c_ctx[None, :], c, jnp.zeros((N_COND - 1 - DEC_BATCH, D_MODEL), F32)], axis=0)
    mod = _modulation(cond, w_mod, b_mod)

    w_in_b = w_in.astype(BF16)
    w_ba_b = w_br_attn.astype(BF16)
    w_bh_b = w_br_hgrn.astype(BF16)
    w_out_b = w_out.astype(BF16)
    by_group = lambda w: w.astype(BF16).reshape(DEPTH, N_GROUPS, EXPERTS_PER_GROUP, *w.shape[2:])
    w_eg_b, w_eu_b, w_ed_b = by_group(w_exp_gate), by_group(w_exp_up), by_group(w_exp_down)
    lane_pad = jnp.eye(N_EXPERTS, LANES, dtype=F32)
    b_router_r = b_router.reshape(1, N_EXPERTS)
    norm1_r = norm1_w.reshape(DEPTH, 1, D_MODEL)
    norm2_r = norm2_w.reshape(DEPTH, 1, D_MODEL)
    hnorm_r = hgrn_norm_w.reshape(DEPTH, 1, HGRN_V)
    qn_r = jnp.tile(q_norm_w, (1, N_Q_HEADS)).reshape(DEPTH, 1, ATTN_WIDTH)
    kn_r = jnp.tile(k_norm_w, (1, N_KV_HEADS)).reshape(DEPTH, 1, KV_WIDTH)

    x = jnp.concatenate([x_prompt.reshape(N_CTX, D_MODEL), x_sample.reshape(N_LAT, D_MODEL)], axis=0)
    ks_out, vs_out, sf_out, sb_out = [], [], [], []
    for l in range(DEPTH):
        q, k, v, h5, gates = _input_projection(l, x, mod, norm1_r, w_in_b, cos_t, sin_t, qn_r, kn_r,
                                               head_ones)
        attn = (_attention_ctx(q, k, v), _attention_lat(l, q, k, v, cache_k, cache_v))
        rec_c, sf, sb = _hgrn(l, h5, 0, BATCH, SEQ, hgrn_lb_fwd, hgrn_lb_bwd, hnorm_r, hgrn_tables)
        rec_s = _hgrn(l, h5, N_CTX, DEC_BATCH, DEC_SEQ, hgrn_lb_fwd, hgrn_lb_bwd, hnorm_r,
                      hgrn_tables, init=(state_fwd, state_bwd))
        rec = (rec_c, rec_s)
        x1, slab, route = _merge(l, attn, rec, gates, x, mod, norm2_r, w_ba_b, w_bh_b, w_out_b,
                                 w_router, b_router_r, sel_member, sel_group, lane_pad)
        dst, tile_group, n_used = _plan(route)
        expert_out = _experts(l, tile_group, n_used, _invert(dst), slab, w_eg_b, w_eu_b, w_ed_b)
        x = _combine(l, dst, expert_out, x1, mod)
        ks_out.append(k[:N_CTX].reshape(BATCH, SEQ, N_KV_HEADS, HEAD_DIM).transpose(0, 2, 1, 3))
        vs_out.append(v[:N_CTX].reshape(BATCH, SEQ, N_KV_HEADS, HEAD_DIM).transpose(0, 2, 1, 3))
        sf_out.append(sf)
        sb_out.append(sb)

    y_prompt = x[:N_CTX].reshape(BATCH, SEQ, D_MODEL)
    y_sample = x[N_CTX:].reshape(DEC_BATCH, DEC_SEQ, D_MODEL)
    return (y_prompt, y_sample, jnp.stack(ks_out, axis=1), jnp.stack(vs_out, axis=1),
            jnp.stack(sf_out, axis=1), jnp.stack(sb_out, axis=1))
```

```python
import functools

import numpy as np
import jax
import jax.numpy as jnp
from jax import lax
from jax.experimental import pallas as pl
from jax.experimental.pallas import tpu as pltpu

F32 = jnp.float32
BF16 = jnp.bfloat16

D_MODEL = 1024
BATCH = 16
SEQ = 256
DEPTH = 4
DEC_BATCH = 4
DEC_SEQ = 2048
PAST_LEN = 512
GRID_W = 64
N_Q_HEADS = 8
N_KV_HEADS = 2
GQA_GROUP = N_Q_HEADS // N_KV_HEADS
HEAD_DIM = 64
AXIS_DIM = HEAD_DIM // 2
ATTN_WIDTH = N_Q_HEADS * HEAD_DIM
KV_WIDTH = N_KV_HEADS * HEAD_DIM
ROPE_THETA = 10000.0
HGRN_HEADS = 4
HGRN_K = 128
HGRN_V = 128
HGRN_WIDTH = HGRN_HEADS * HGRN_K
N_EXPERTS = 16
N_GROUPS = 4
EXPERTS_PER_GROUP = N_EXPERTS // N_GROUPS
D_FF = 512
EPS = 1e-6
IN_COLS = ATTN_WIDTH + 2 * KV_WIDTH + 5 * HGRN_WIDTH + 2 * D_MODEL

N_CTX = BATCH * SEQ
N_LAT = DEC_BATCH * DEC_SEQ
N_TOK = N_CTX + N_LAT
N_COND = 8
N_MOD = 6

C_Q = 0
C_K = C_Q + ATTN_WIDTH
C_V = C_K + KV_WIDTH
C_HQ = C_V + KV_WIDTH
C_FF = C_HQ + HGRN_WIDTH
C_FB = C_FF + HGRN_WIDTH
C_HI = C_FB + HGRN_WIDTH
C_HG = C_HI + HGRN_WIDTH
C_GA = C_HG + HGRN_WIDTH
C_GH = C_GA + D_MODEL

LANES = 128
SUBLANES = 8
FEAT_ROWS = D_MODEL // LANES
PACK_ROWS = FEAT_ROWS // 2
SLAB_ROWS = SUBLANES

TOK_TILE = 512
PLAN_TILE = 1024
SORT_TILE = 512
N_SORT_TILES = N_TOK // SORT_TILE + N_GROUPS
Q_TILE = 512
CHUNK = 64
HGRN_LEVELS = (32, 16, 8, 4, 2, 1)
BLOCK_CHUNKS = 8
LOG2E = 1.4426950408889634
VMEM_LIMIT = 56 * 1024 * 1024

_HI = lax.Precision.HIGHEST


def _sigmoid(x):
    return 1.0 / (1.0 + jnp.exp(-x))


def _dot(a, b):
    return jnp.dot(a, b, preferred_element_type=F32)


def _dot_nt(a, b, precision=None):
    return lax.dot_general(a, b, (((1,), (1,)), ((), ())), precision=precision,
                           preferred_element_type=F32)


def _dot_tn(a, b, precision=None):
    return lax.dot_general(a, b, (((0,), (0,)), ((), ())), precision=precision,
                           preferred_element_type=F32)


def _dot_exact(a, b):
    return jnp.dot(a, b, precision=_HI, preferred_element_type=F32)


def _split_bf16(a):
    hi = a.astype(BF16)
    return hi, (a - hi.astype(F32)).astype(BF16)


def _params(*sem):
    return pltpu.CompilerParams(dimension_semantics=sem, vmem_limit_bytes=VMEM_LIMIT)


def _mod_kernel(cond_ref, w_ref, b_ref, out_ref):
    c = cond_ref[...]
    a_hi, a_lo = _split_bf16(c * _sigmoid(c))
    w_hi, w_lo = _split_bf16(w_ref[...])
    p = _dot(jnp.concatenate([a_hi, a_lo], axis=0), w_hi)
    out_ref[...] = p[:N_COND] + p[N_COND:] + _dot(a_hi, w_lo) + b_ref[...]


def _modulation(cond, w_mod, b_mod):
    out = pl.pallas_call(
        _mod_kernel,
        grid=(DEPTH, N_MOD),
        in_specs=[
            pl.BlockSpec((N_COND, D_MODEL), lambda l, j: (0, 0)),
            pl.BlockSpec((None, D_MODEL, D_MODEL), lambda l, j: (l, 0, j)),
            pl.BlockSpec((None, 1, D_MODEL), lambda l, j: (l, 0, j)),
        ],
        out_specs=pl.BlockSpec((None, None, N_COND, D_MODEL), lambda l, j: (l, j, 0, 0)),
        out_shape=jax.ShapeDtypeStruct((DEPTH, N_MOD, N_COND, D_MODEL), F32),
        compiler_params=_params("arbitrary", "arbitrary"),
        name="modulation",
    )(cond, w_mod, b_mod.reshape(DEPTH, 1, N_MOD * D_MODEL))
    return out.transpose(0, 2, 1, 3)


def _head_mean_sq(a, bd):
    hi, lo = _split_bf16(a * a)
    return (_dot(hi, bd) + _dot(lo, bd)) * (1.0 / HEAD_DIM)


def _rope(x, cos, sin_signed):
    width = x.shape[-1]
    lane = lax.broadcasted_iota(jnp.int32, x.shape, 1)
    first = (lane & (AXIS_DIM - 1)) < (AXIS_DIM // 2)
    rot = jnp.where(first, pltpu.roll(x, width - AXIS_DIM // 2, 1), pltpu.roll(x, AXIS_DIM // 2, 1))
    return x * cos + rot * sin_signed


def _inproj_kernel(x_ref, mod_ref, n1_ref, w_ref, cos_ref, sin_ref, qn_ref, kn_ref, bd_ref,
                   q_ref, k_ref, v_ref, h_ref, g_ref):
    x = x_ref[...]
    xn = x * lax.rsqrt(jnp.mean(x * x, axis=-1, keepdims=True) + EPS) * n1_ref[...]
    xb = (xn * (1.0 + mod_ref[1:2, :]) + mod_ref[0:1, :]).astype(BF16)

    def proj(c0, width):
        return _dot(xb, w_ref[:, c0:c0 + width])

    a = proj(C_Q, ATTN_WIDTH)
    qn = a * lax.rsqrt(_head_mean_sq(a, bd_ref[...]) + EPS) * qn_ref[...]
    q_ref[...] = (_rope(qn, cos_ref[...], sin_ref[...]) * (HEAD_DIM ** -0.5)).astype(BF16)

    a = proj(C_K, KV_WIDTH)
    kn = a * lax.rsqrt(_head_mean_sq(a, bd_ref[:KV_WIDTH, :KV_WIDTH]) + EPS) * kn_ref[...]
    k_ref[...] = _rope(kn, cos_ref[:, :KV_WIDTH], sin_ref[:, :KV_WIDTH])
    v_ref[...] = proj(C_V, KV_WIDTH)

    a = proj(C_HQ, HGRN_WIDTH)
    h_ref[:, 0:HGRN_WIDTH] = a * _sigmoid(a) * (HGRN_K ** -0.5)
    h_ref[:, HGRN_WIDTH:2 * HGRN_WIDTH] = proj(C_FF, HGRN_WIDTH)
    h_ref[:, 2 * HGRN_WIDTH:3 * HGRN_WIDTH] = proj(C_FB, HGRN_WIDTH)
    h_ref[:, 3 * HGRN_WIDTH:4 * HGRN_WIDTH] = proj(C_HI, HGRN_WIDTH)
    a = proj(C_HG, HGRN_WIDTH)
    h_ref[:, 4 * HGRN_WIDTH:5 * HGRN_WIDTH] = a * _sigmoid(a)

    g_ref[:, 0:D_MODEL] = _sigmoid(proj(C_GA, D_MODEL))
    g_ref[:, D_MODEL:2 * D_MODEL] = _sigmoid(proj(C_GH, D_MODEL))


_CTX_TILES = N_CTX // TOK_TILE
_LAT_TILES_PER_SEQ = DEC_SEQ // TOK_TILE


def _tile_cond(i):
    return jnp.where(i < _CTX_TILES, 0, 1 + (i - _CTX_TILES) // _LAT_TILES_PER_SEQ)


def _tile_rope_block(i):
    return jnp.where(i < _CTX_TILES, _LAT_TILES_PER_SEQ, (i - _CTX_TILES) % _LAT_TILES_PER_SEQ)


def _input_projection(l, x, mod, norm1_w, w_in, cos_t, sin_t, qn_w, kn_w, bd):
    row = lambda w: pl.BlockSpec((TOK_TILE, w), lambda i: (i, 0))
    return pl.pallas_call(
        _inproj_kernel,
        grid=(N_TOK // TOK_TILE,),
        in_specs=[
            row(D_MODEL),
            pl.BlockSpec((None, None, N_MOD, D_MODEL), lambda i: (l, _tile_cond(i), 0, 0)),
            pl.BlockSpec((None, 1, D_MODEL), lambda i: (l, 0, 0)),
            pl.BlockSpec((None, D_MODEL, IN_COLS), lambda i: (l, 0, 0),
                         pipeline_mode=pl.Buffered(1)),
            pl.BlockSpec((TOK_TILE, ATTN_WIDTH), lambda i: (_tile_rope_block(i), 0)),
            pl.BlockSpec((TOK_TILE, ATTN_WIDTH), lambda i: (_tile_rope_block(i), 0)),
            pl.BlockSpec((None, 1, ATTN_WIDTH), lambda i: (l, 0, 0)),
            pl.BlockSpec((None, 1, KV_WIDTH), lambda i: (l, 0, 0)),
            pl.BlockSpec((ATTN_WIDTH, ATTN_WIDTH), lambda i: (0, 0)),
        ],
        out_specs=[row(ATTN_WIDTH), row(KV_WIDTH), row(KV_WIDTH), row(5 * HGRN_WIDTH),
                   row(2 * D_MODEL)],
        out_shape=[
            jax.ShapeDtypeStruct((N_TOK, ATTN_WIDTH), BF16),
            jax.ShapeDtypeStruct((N_TOK, KV_WIDTH), F32),
            jax.ShapeDtypeStruct((N_TOK, KV_WIDTH), F32),
            jax.ShapeDtypeStruct((N_TOK, 5 * HGRN_WIDTH), F32),
            jax.ShapeDtypeStruct((N_TOK, 2 * D_MODEL), F32),
        ],
        compiler_params=_params("arbitrary"),
        name="input_projection",
    )(x, mod, norm1_w, w_in, cos_t, sin_t, qn_w, kn_w, bd)


def _softmax_pv(scores, values):
    m = scores[0].max(axis=-1, keepdims=True)
    for s in scores[1:]:
        m = jnp.maximum(m, s.max(axis=-1, keepdims=True))
    num, den = None, None
    for s, v in zip(scores, values):
        p = jnp.exp(s - m)
        d = p.sum(axis=-1, keepdims=True)
        o = _dot(p.astype(BF16), v)
        num = o if num is None else num + o
        den = d if den is None else den + d
    return num / den


def _attn_ctx_kernel(q_ref, k_ref, v_ref, o_ref):
    for g in range(N_KV_HEADS):
        cols = slice(g * HEAD_DIM, (g + 1) * HEAD_DIM)
        kg = k_ref[:, cols].astype(BF16)
        vg = v_ref[:, cols].astype(BF16)
        for hh in range(GQA_GROUP):
            h = g * GQA_GROUP + hh
            hc = slice(h * HEAD_DIM, (h + 1) * HEAD_DIM)
            o = _softmax_pv([_dot_nt(q_ref[:, hc], kg)], [vg])
            o_ref[:, hc] = o.astype(BF16)


def _attention_ctx(q, k, v):
    return pl.pallas_call(
        _attn_ctx_kernel,
        grid=(BATCH,),
        in_specs=[
            pl.BlockSpec((SEQ, ATTN_WIDTH), lambda b: (b, 0)),
            pl.BlockSpec((SEQ, KV_WIDTH), lambda b: (b, 0)),
            pl.BlockSpec((SEQ, KV_WIDTH), lambda b: (b, 0)),
        ],
        out_specs=pl.BlockSpec((SEQ, ATTN_WIDTH), lambda b: (b, 0)),
        out_shape=jax.ShapeDtypeStruct((N_CTX, ATTN_WIDTH), BF16),
        compiler_params=_params("arbitrary"),
        name="attention_ctx",
    )(q, k, v)


def _attn_lat_kernel(q_ref, k_ref, v_ref, ck_ref, cv_ref, o_ref):
    for g in range(N_KV_HEADS):
        cols = slice(g * HEAD_DIM, (g + 1) * HEAD_DIM)
        kg = k_ref[:, cols].astype(BF16)
        vg = v_ref[:, cols].astype(BF16)
        ckg = ck_ref[g].astype(BF16)
        cvg = cv_ref[g].astype(BF16)
        for hh in range(GQA_GROUP):
            h = g * GQA_GROUP + hh
            hc = slice(h * HEAD_DIM, (h + 1) * HEAD_DIM)
            qh = q_ref[:, hc]
            o = _softmax_pv([_dot_nt(qh, kg), _dot_nt(qh, ckg)], [vg, cvg])
            o_ref[:, hc] = o.astype(BF16)


def _attention_lat(l, q, k, v, cache_k, cache_v):
    q_blocks = DEC_SEQ // Q_TILE
    q_off = N_CTX // Q_TILE
    kv_off = N_CTX // DEC_SEQ
    cache_spec = pl.BlockSpec((None, None, N_KV_HEADS, PAST_LEN, HEAD_DIM),
                              lambda b, i: (b, l, 0, 0, 0))
    return pl.pallas_call(
        _attn_lat_kernel,
        grid=(DEC_BATCH, q_blocks),
        in_specs=[
            pl.BlockSpec((Q_TILE, ATTN_WIDTH), lambda b, i: (q_off + b * q_blocks + i, 0)),
            pl.BlockSpec((DEC_SEQ, KV_WIDTH), lambda b, i: (kv_off + b, 0)),
            pl.BlockSpec((DEC_SEQ, KV_WIDTH), lambda b, i: (kv_off + b, 0)),
            cache_spec,
            cache_spec,
        ],
        out_specs=pl.BlockSpec((Q_TILE, ATTN_WIDTH), lambda b, i: (b * q_blocks + i, 0)),
        out_shape=jax.ShapeDtypeStruct((N_LAT, ATTN_WIDTH), BF16),
        compiler_params=_params("arbitrary", "arbitrary"),
        name="attention_lat",
    )(q, k, v, cache_k, cache_v)


def _lower_bound(lb_ref, l):
    z = lb_ref[...]
    e = jnp.exp(z - z.max(axis=0, keepdims=True))
    p = e / e.sum(axis=0, keepdims=True)
    acc = p[0:1] * 0.0
    for r in range(1, l + 1):
        acc = acc + p[r:r + 1]
    return acc


def _hgrn_tables(rev):
    t = np.arange(CHUNK)
    tt, uu = t[:, None], t[None, :]
    groups = []
    masks = []
    for h in HGRN_LEVELS:
        right = (t & h) != 0
        edge = ((t // (2 * h)) * 2 * h + h - 1)[:, None]
        groups.append(np.where(right[:, None], (uu > edge) & (uu <= tt), (uu > tt) & (uu <= edge)))
        same = (tt // (2 * h)) == (uu // (2 * h))
        masks.append(same & right[:, None] & ~right[None, :])
    masks.append(tt == uu)
    groups += [uu <= tt, uu > tt]
    sums = np.stack(groups).astype(np.float32)
    masks = np.stack(masks).astype(np.float32)
    if rev:
        sums = sums[:, ::-1, ::-1]
        masks = masks[:, ::-1, ::-1]
    sums = sums.reshape(-1, CHUNK)
    return (jnp.asarray(np.concatenate([sums, sums], axis=1), dtype=BF16),
            jnp.asarray(np.ascontiguousarray(masks)))


def _hgrn_block(q, x, v, lb, sums_ref, masks_ref, st_ref, rev):
    n = q.shape[0] // CHUNK
    f = lb + (1.0 - lb) * _sigmoid(x)
    kk = 1.0 - f
    g = jnp.log(f) * LOG2E
    g_hi, g_lo = _split_bf16(g)
    vb = v.astype(BF16)
    rid = lax.broadcasted_iota(jnp.int32, (CHUNK, HGRN_K), 0)
    edge = 0 if rev else CHUNK - 1
    n_lev = len(HGRN_LEVELS)
    outs = [None] * n
    for c in (reversed(range(n)) if rev else range(n)):
        rs = slice(c * CHUNK, (c + 1) * CHUNK)
        e = jnp.exp2(_dot(sums_ref[...], jnp.concatenate([g_hi[rs], g_lo[rs]], axis=0)))
        q_c, k_c, v_c = q[rs], kk[rs], vb[rs]
        a = _dot_nt(q_c.astype(BF16), k_c.astype(BF16)) * masks_ref[n_lev]
        for i, h in enumerate(HGRN_LEVELS):
            later = ((rid & h) == 0) if rev else ((rid & h) != 0)
            z = (jnp.where(later, q_c, k_c) * e[i * CHUNK:(i + 1) * CHUNK]).astype(BF16)
            a = a + _dot_nt(z, z) * masks_ref[i]
        e_in = e[n_lev * CHUNK:(n_lev + 1) * CHUNK]
        e_out = e[(n_lev + 1) * CHUNK:(n_lev + 2) * CHUNK]
        st = st_ref[...]
        outs[c] = (_dot(a.astype(BF16), v_c)
                   + _dot_nt((q_c * e_in).astype(BF16), st.astype(BF16)))
        k_dec = (k_c * e_out).astype(BF16)
        st_ref[...] = st * e_in[edge:edge + 1] + _dot_tn(v_c, k_dec)
    return jnp.concatenate(outs, axis=0)


def _hgrn_kernel(l, seq_len, has_init, *refs):
    (hq_ref, ff_ref, fb_ref, hi_ref, hg_ref, lbf_ref, lbb_ref, hn_ref,
     sumf_ref, sumb_ref, mskf_ref, mskb_ref) = refs[:12]
    refs = refs[12:]
    if has_init:
        s0f_ref, s0b_ref, rec_ref, of_ref, ob_ref, stf_ref, stb_ref = refs
    else:
        rec_ref, sf_ref, sb_ref, of_ref, ob_ref, stf_ref, stb_ref = refs
    lb_f = _lower_bound(lbf_ref, l)
    lb_b = _lower_bound(lbb_ref, l)
    if has_init:
        stf_ref[...] = s0f_ref[...].T
        stb_ref[...] = s0b_ref[...].T
    else:
        stf_ref[...] = jnp.zeros((HGRN_V, HGRN_K), F32)
        stb_ref[...] = jnp.zeros((HGRN_V, HGRN_K), F32)
    rows = min(BLOCK_CHUNKS * CHUNK, seq_len)
    n_blocks = seq_len // rows

    def step(rf, rb):
        of_ref[rf, :] = _hgrn_block(hq_ref[rf, :], ff_ref[rf, :], hi_ref[rf, :], lb_f,
                                    sumf_ref, mskf_ref, stf_ref, False)
        ob_ref[rb, :] = _hgrn_block(hq_ref[rb, :], fb_ref[rb, :], hi_ref[rb, :], lb_b,
                                    sumb_ref, mskb_ref, stb_ref, True)

    if n_blocks == 1:
        step(pl.ds(0, rows), pl.ds(0, rows))
    else:
        def body(i, carry):
            step(pl.ds(pl.multiple_of(i * rows, rows), rows),
                 pl.ds(pl.multiple_of((n_blocks - 1 - i) * rows, rows), rows))
            return carry

        lax.fori_loop(0, n_blocks, body, 0)
    o = of_ref[...] + ob_ref[...]
    o = o * lax.rsqrt(jnp.mean(o * o, axis=-1, keepdims=True) + EPS) * hn_ref[...]
    rec_ref[...] = o * hg_ref[...]
    if not has_init:
        sf_ref[...] = stf_ref[...].T
        sb_ref[...] = stb_ref[...].T


def _hgrn(l, h5, row_off, n_seq, seq_len, lb_fwd, lb_bwd, hnorm_w, tables, init=None):
    blocks_off = row_off // seq_len
    part = lambda k: pl.BlockSpec((seq_len, HGRN_K),
                                  lambda b, h: (blocks_off + b, k * HGRN_HEADS + h))
    lb_spec = pl.BlockSpec((DEPTH, HGRN_K), lambda b, h: (0, h))
    sum_spec = pl.BlockSpec((8 * CHUNK, 2 * CHUNK), lambda b, h: (0, 0))
    msk_spec = pl.BlockSpec((len(HGRN_LEVELS) + 1, CHUNK, CHUNK), lambda b, h: (0, 0, 0))
    in_specs = [part(0), part(1), part(2), part(3), part(4), lb_spec, lb_spec,
                pl.BlockSpec((None, 1, HGRN_V), lambda b, h: (l, 0, 0)),
                sum_spec, sum_spec, msk_spec, msk_spec]
    args = [h5, h5, h5, h5, h5, lb_fwd, lb_bwd, hnorm_w, *tables]
    rec_spec = pl.BlockSpec((seq_len, HGRN_V), lambda b, h: (b, h))
    rec_shape = jax.ShapeDtypeStruct((n_seq * seq_len, HGRN_WIDTH), F32)
    if init is not None:
        st_spec = pl.BlockSpec((None, None, None, HGRN_K, HGRN_V), lambda b, h: (b, l, h, 0, 0))
        in_specs += [st_spec, st_spec]
        args += list(init)
        out_specs, out_shape = rec_spec, rec_shape
    else:
        st_spec = pl.BlockSpec((None, None, HGRN_K, HGRN_V), lambda b, h: (b, h, 0, 0))
        st_shape = jax.ShapeDtypeStruct((n_seq, HGRN_HEADS, HGRN_K, HGRN_V), F32)
        out_specs, out_shape = [rec_spec, st_spec, st_spec], [rec_shape, st_shape, st_shape]
    return pl.pallas_call(
        functools.partial(_hgrn_kernel, l, seq_len, init is not None),
        grid=(n_seq, HGRN_HEADS),
        in_specs=in_specs,
        out_specs=out_specs,
        out_shape=out_shape,
        scratch_shapes=[pltpu.VMEM((seq_len, HGRN_V), F32), pltpu.VMEM((seq_len, HGRN_V), F32),
                        pltpu.VMEM((HGRN_V, HGRN_K), F32), pltpu.VMEM((HGRN_V, HGRN_K), F32)],
        compiler_params=_params("arbitrary", "arbitrary"),
        name="hgrn_lat" if init is not None else "hgrn_ctx",
    )(*args)


def _route(scores, bias, sel_member, sel_group):
    sel = scores + bias
    members = [_dot_exact(sel, sel_member[k]) for k in range(EXPERTS_PER_GROUP)]
    pair = None
    for i in range(EXPERTS_PER_GROUP):
        for j in range(i + 1, EXPERTS_PER_GROUP):
            s = members[i] + members[j]
            pair = s if pair is None else jnp.maximum(pair, s)
    lane = lax.broadcasted_iota(jnp.int32, sel.shape, 1)
    best = _dot_exact(pair, sel_group[0])
    best_g = jnp.zeros(sel.shape, jnp.int32)
    for m in range(1, N_GROUPS):
        cand = _dot_exact(pair, sel_group[m])
        upd = cand > best
        best_g = jnp.where(upd, m, best_g)
        best = jnp.where(upd, cand, best)
    in_group = (lane >> 2) == best_g
    pos = lane & (EXPERTS_PER_GROUP - 1)
    rank = jnp.zeros(sel.shape, jnp.int32)
    for k in range(EXPERTS_PER_GROUP):
        ahead = (members[k] > sel) | ((members[k] == sel) & (k < pos))
        rank = rank + jnp.where(ahead, 1, 0)
    w = jnp.where(in_group & (rank < 2), scores, 0.0)
    return w / w.sum(axis=-1, keepdims=True)


def _merge_kernel(attn_c_ref, attn_s_ref, rec_c_ref, rec_s_ref, g_ref, x_ref, mod_ref, n2_ref,
                  wa_ref, wh_ref, wo_ref, wr_ref, br_ref, selm_ref, selg_ref, pad_ref,
                  x1_ref, slab_ref, route_ref):
    is_ctx = pl.program_id(0) < _CTX_TILES
    attn = jnp.where(is_ctx, attn_c_ref[...], attn_s_ref[...])
    rec = jnp.where(is_ctx, rec_c_ref[...], rec_s_ref[...])
    ya = _dot(attn, wa_ref[...])
    yh = _dot(rec.astype(BF16), wh_ref[...])
    merged = g_ref[:, 0:D_MODEL] * ya + g_ref[:, D_MODEL:2 * D_MODEL] * yh
    out = _dot(merged.astype(BF16), wo_ref[...])
    x1 = x_ref[...] + mod_ref[2:3, :] * out
    x1_ref[...] = x1
    xn = x1 * lax.rsqrt(jnp.mean(x1 * x1, axis=-1, keepdims=True) + EPS) * n2_ref[...]
    h2 = xn * (1.0 + mod_ref[4:5, :]) + mod_ref[3:4, :]
    h_hi, h_lo = _split_bf16(h2)
    w_hi, w_lo = _split_bf16(wr_ref[...])
    scores = _sigmoid(_dot(jnp.concatenate([h_hi, h_lo, h_hi], axis=1),
                           jnp.concatenate([w_hi, w_hi, w_lo], axis=0)))
    gates = _route(scores, br_ref[...], [selm_ref[k] for k in range(EXPERTS_PER_GROUP)],
                   [selg_ref[m] for m in range(N_GROUPS)])
    gates = _dot_exact(gates, pad_ref[...])
    route_ref[...] = gates
    half = D_MODEL // 2
    for s in range(PACK_ROWS):
        hi = pltpu.bitcast(h2[:, s * LANES:(s + 1) * LANES].astype(BF16).astype(F32), jnp.int32)
        lo = pltpu.bitcast(h2[:, half + s * LANES:half + (s + 1) * LANES].astype(BF16).astype(F32),
                           jnp.int32)
        slab_ref[:, s, :] = pltpu.bitcast(hi | lax.shift_right_logical(lo, 16), F32)
    slab_ref[:, PACK_ROWS, :] = gates
    for s in range(PACK_ROWS + 1, SLAB_ROWS):
        slab_ref[:, s, :] = jnp.zeros_like(gates)


def _merge(l, attn, rec, gates, x, mod, norm2_w, w_br_attn, w_br_hgrn, w_out, w_router, b_router,
           sel_member, sel_group, lane_pad):
    row = lambda w: pl.BlockSpec((TOK_TILE, w), lambda i: (i, 0))
    ctx_row = lambda w: pl.BlockSpec((TOK_TILE, w), lambda i: (jnp.minimum(i, _CTX_TILES - 1), 0))
    lat_row = lambda w: pl.BlockSpec((TOK_TILE, w), lambda i: (jnp.maximum(i - _CTX_TILES, 0), 0))
    layer = lambda r, c: pl.BlockSpec((None, r, c), lambda i: (l, 0, 0))
    full = lambda *s: pl.BlockSpec(s, lambda i: (0,) * len(s))
    return pl.pallas_call(
        _merge_kernel,
        grid=(N_TOK // TOK_TILE,),
        in_specs=[
            ctx_row(ATTN_WIDTH), lat_row(ATTN_WIDTH), ctx_row(HGRN_WIDTH), lat_row(HGRN_WIDTH),
            row(2 * D_MODEL), row(D_MODEL),
            pl.BlockSpec((None, None, N_MOD, D_MODEL), lambda i: (l, _tile_cond(i), 0, 0)),
            layer(1, D_MODEL),
            layer(ATTN_WIDTH, D_MODEL), layer(HGRN_WIDTH, D_MODEL), layer(D_MODEL, D_MODEL),
            full(D_MODEL, N_EXPERTS), full(1, N_EXPERTS),
            full(EXPERTS_PER_GROUP, N_EXPERTS, N_EXPERTS), full(N_GROUPS, N_EXPERTS, N_EXPERTS),
            full(N_EXPERTS, LANES),
        ],
        out_specs=[row(D_MODEL),
                   pl.BlockSpec((TOK_TILE, SLAB_ROWS, LANES), lambda i: (i, 0, 0)),
                   row(LANES)],
        out_shape=[
            jax.ShapeDtypeStruct((N_TOK, D_MODEL), F32),
            jax.ShapeDtypeStruct((N_TOK, SLAB_ROWS, LANES), F32),
            jax.ShapeDtypeStruct((N_TOK, LANES), F32),
        ],
        compiler_params=_params("arbitrary"),
        name="merge_router",
    )(*attn, *rec, gates, x, mod, norm2_w, w_br_attn, w_br_hgrn, w_out, w_router, b_router,
      sel_member, sel_group, lane_pad)


def _plan_kernel(route_ref, dst_ref, meta_ref, tot_ref, run_ref):
    p, i = pl.program_id(0), pl.program_id(1)

    @pl.when(jnp.logical_and(p == 0, i == 0))
    def _():
        tot_ref[...] = jnp.zeros_like(tot_ref)

    @pl.when(i == 0)
    def _():
        run_ref[...] = jnp.zeros_like(run_ref)

    grp = lax.broadcasted_iota(jnp.int32, (SUBLANES, LANES), 0)
    lane = lax.broadcasted_iota(jnp.int32, (SUBLANES, LANES), 1)
    member = jnp.where(((lane >> 2) == grp) & (lane < N_EXPERTS), 1.0, 0.0)
    one_hot = jnp.where(_dot_nt(member, route_ref[...], precision=_HI) > 0.0, 1.0, 0.0)
    src = lax.broadcasted_iota(jnp.int32, (PLAN_TILE, PLAN_TILE), 0)
    tgt = lax.broadcasted_iota(jnp.int32, (PLAN_TILE, PLAN_TILE), 1)
    before = jnp.where(src < tgt, 1.0, 0.0).astype(BF16)
    rank = _dot(one_hot.astype(BF16), before) + run_ref[:, 0:1]
    tile_tot = one_hot.sum(axis=1, keepdims=True)

    @pl.when(p == 0)
    def _():
        tot_ref[...] += tile_tot

    @pl.when(p == 1)
    def _():
        run_ref[...] += tile_tot
        tot = tot_ref[:, 0:1]
        n_tiles = jnp.floor((tot + (SORT_TILE - 1)) * (1.0 / SORT_TILE))
        offs, acc = [], jnp.zeros((1, 1), F32)
        for m in range(N_GROUPS):
            offs.append(acc * SORT_TILE)
            acc = acc + n_tiles[m:m + 1]
        dst = sum(one_hot[m:m + 1] * (offs[m] + rank[m:m + 1]) for m in range(N_GROUPS))
        for r in range(PLAN_TILE // LANES):
            dst_ref[r:r + 1, :] = dst[:, r * LANES:(r + 1) * LANES].astype(jnp.int32)
        tile_row = (lax.broadcasted_iota(jnp.int32, (1, LANES), 1) * SORT_TILE).astype(F32)
        tile_group = sum(jnp.where(tile_row >= offs[m], 1.0, 0.0) for m in range(1, N_GROUPS))
        rows = [tile_group, jnp.broadcast_to(acc, (1, LANES))]
        rows += [jnp.zeros((1, LANES), F32)] * (SUBLANES - len(rows))
        meta_ref[...] = jnp.concatenate(rows, axis=0).astype(jnp.int32)


def _plan(route):
    n_blocks = N_TOK // PLAN_TILE
    dst, meta = pl.pallas_call(
        _plan_kernel,
        grid=(2, n_blocks),
        in_specs=[pl.BlockSpec((PLAN_TILE, LANES), lambda p, i: (i, 0))],
        out_specs=[pl.BlockSpec((PLAN_TILE // LANES, LANES), lambda p, i: (i * p, 0)),
                   pl.BlockSpec((SUBLANES, LANES), lambda p, i: (0, 0))],
        out_shape=[jax.ShapeDtypeStruct((N_TOK // LANES, LANES), jnp.int32),
                   jax.ShapeDtypeStruct((SUBLANES, LANES), jnp.int32)],
        scratch_shapes=[pltpu.VMEM((SUBLANES, LANES), F32), pltpu.VMEM((SUBLANES, LANES), F32)],
        compiler_params=_params("arbitrary", "arbitrary"),
        name="route_plan",
    )(route)
    return dst, meta[0, :N_SORT_TILES], meta[1, 0:1]


def _start_rows(make_copy):
    def start(i, carry):
        make_copy(2 * i).start(priority=0)
        make_copy(2 * i + 1).start(priority=1)
        return carry

    lax.fori_loop(0, LANES // 2, start, 0, unroll=4)


def _invert_kernel(dst_ref, zero_ref, src_ref, sem):
    i = pl.program_id(0)

    @pl.when(i == 0)
    def _():
        clear = pltpu.make_async_copy(zero_ref, src_ref, sem)
        clear.start()
        clear.wait()

    for r in range(PLAN_TILE // LANES):
        def put(c, carry, r=r):
            d = dst_ref[r, c]
            src_ref[lax.shift_right_logical(d, 7), d & (LANES - 1)] = i * PLAN_TILE + r * LANES + c
            return carry

        lax.fori_loop(0, LANES, put, 0, unroll=8)


def _invert(dst):
    shape = (N_SORT_TILES * SORT_TILE // LANES, LANES)
    return pl.pallas_call(
        _invert_kernel,
        grid=(N_TOK // PLAN_TILE,),
        in_specs=[pl.BlockSpec((PLAN_TILE // LANES, LANES), lambda i: (i, 0), memory_space=pltpu.SMEM),
                  pl.BlockSpec(memory_space=pl.ANY)],
        out_specs=pl.BlockSpec(shape, lambda i: (0, 0), memory_space=pltpu.SMEM),
        out_shape=jax.ShapeDtypeStruct(shape, jnp.int32),
        scratch_shapes=[pltpu.SemaphoreType.DMA(())],
        compiler_params=_params("arbitrary"),
        name="invert_plan",
    )(dst, jnp.zeros(shape, jnp.int32))


_SRC_BLOCK_TILES = SUBLANES * LANES // SORT_TILE


def _experts_kernel(tg_ref, nv_ref, src_ref, nxt_ref, slab_ref, wg_ref, wu_ref, wd_ref, y_ref,
                    x_ref, sem):
    j = pl.program_id(0)
    n_used = nv_ref[0]
    groups = SORT_TILE // LANES

    def gather(idx_ref, tile, slot, unrolled):
        first_row = (tile % _SRC_BLOCK_TILES) * groups
        for r in range(groups):
            make = lambda c, r=r: pltpu.make_async_copy(
                slab_ref.at[idx_ref[first_row + r, c]], x_ref.at[slot, r * LANES + c], sem.at[slot])
            if unrolled:
                for c in range(LANES):
                    make(c).start(priority=c % 2)
            else:
                _start_rows(make)

    def wait_rows(slot):
        for r in range(groups):
            pltpu.make_async_copy(slab_ref.at[pl.ds(0, LANES)],
                                  x_ref.at[slot, pl.ds(r * LANES, LANES)], sem.at[slot]).wait()

    @pl.when(j == 0)
    def _():
        gather(src_ref, j, 0, False)

    @pl.when(j < n_used)
    def _():
        slot = j % 2
        wait_rows(slot)
        first_expert = tg_ref[j] * EXPERTS_PER_GROUP
        words = [pltpu.bitcast(x_ref[slot, :, s, :], jnp.int32) for s in range(PACK_ROWS)]
        h = jnp.concatenate([pltpu.bitcast(w & -65536, F32) for w in words]
                            + [pltpu.bitcast(w << 16, F32) for w in words], axis=1).astype(BF16)
        gates = x_ref[slot, :, PACK_ROWS, :]
        gather(nxt_ref, jnp.minimum(j + 1, n_used - 1), 1 - slot, True)
        lane = lax.broadcasted_iota(jnp.int32, gates.shape, 1)
        acc = None
        for k in range(EXPERTS_PER_GROUP):
            a = _dot(h, wg_ref[k])
            u = _dot(h, wu_ref[k])
            ge = jnp.sum(jnp.where(lane == first_expert + k, gates, 0.0), axis=-1, keepdims=True)
            y = _dot((a * _sigmoid(a) * u * ge).astype(BF16), wd_ref[k])
            acc = y if acc is None else acc + y
        for s in range(FEAT_ROWS):
            y_ref[:, s, :] = acc[:, s * LANES:(s + 1) * LANES]

        @pl.when(j + 1 >= n_used)
        def _():
            wait_rows(1 - slot)

    @pl.when(j >= n_used)
    def _():
        y_ref[...] = jnp.zeros_like(y_ref)


def _experts(l, tile_group, n_used, src, slab, w_gate, w_up, w_down):
    tile = lambda j, tg, nv: jnp.minimum(j, nv[0] - 1)
    w_spec = lambda r, c: pl.BlockSpec((None, None, EXPERTS_PER_GROUP, r, c),
                                       lambda j, tg, nv: (l, tg[tile(j, tg, nv)], 0, 0, 0))
    return pl.pallas_call(
        _experts_kernel,
        grid_spec=pltpu.PrefetchScalarGridSpec(
            num_scalar_prefetch=2,
            grid=(N_SORT_TILES,),
            in_specs=[
                pl.BlockSpec((SUBLANES, LANES), lambda j, tg, nv: (j // _SRC_BLOCK_TILES, 0),
                             memory_space=pltpu.SMEM),
                pl.BlockSpec((SUBLANES, LANES),
                             lambda j, tg, nv: (tile(j + 1, tg, nv) // _SRC_BLOCK_TILES, 0),
                             memory_space=pltpu.SMEM),
                pl.BlockSpec(memory_space=pl.ANY),
                w_spec(D_MODEL, D_FF), w_spec(D_MODEL, D_FF), w_spec(D_FF, D_MODEL),
            ],
            out_specs=pl.BlockSpec((SORT_TILE, FEAT_ROWS, LANES), lambda j, tg, nv: (j, 0, 0)),
            scratch_shapes=[pltpu.VMEM((2, SORT_TILE, SLAB_ROWS, LANES), F32),
                            pltpu.SemaphoreType.DMA((2,))],
        ),
        out_shape=jax.ShapeDtypeStruct((N_SORT_TILES * SORT_TILE, FEAT_ROWS, LANES), F32),
        compiler_params=_params("arbitrary"),
        name="experts",
    )(tile_group, n_used, src, src, slab, w_gate, w_up, w_down)


def _combine_kernel(dst_ref, y_ref, x1_ref, mod_ref, o_ref, buf_ref, sem):
    groups = range(PLAN_TILE // LANES)
    for r in groups:
        _start_rows(lambda c, r=r: pltpu.make_async_copy(
            y_ref.at[dst_ref[r, c]], buf_ref.at[r * LANES + c], sem))
    for r in groups:
        pltpu.make_async_copy(y_ref.at[pl.ds(0, LANES)],
                              buf_ref.at[pl.ds(r * LANES, LANES)], sem).wait()
    for s in range(FEAT_ROWS):
        cols = slice(s * LANES, (s + 1) * LANES)
        o_ref[:, cols] = x1_ref[:, cols] + mod_ref[5:6, cols] * buf_ref[:, s, :]


_PLAN_CTX_TILES = N_CTX // PLAN_TILE
_PLAN_LAT_TILES_PER_SEQ = DEC_SEQ // PLAN_TILE


def _plan_cond(i):
    return jnp.where(i < _PLAN_CTX_TILES, 0, 1 + (i - _PLAN_CTX_TILES) // _PLAN_LAT_TILES_PER_SEQ)


def _combine(l, dst, expert_out, x1, mod):
    return pl.pallas_call(
        _combine_kernel,
        grid=(N_TOK // PLAN_TILE,),
        in_specs=[
            pl.BlockSpec((PLAN_TILE // LANES, LANES), lambda i: (i, 0), memory_space=pltpu.SMEM),
            pl.BlockSpec(memory_space=pl.ANY),
            pl.BlockSpec((PLAN_TILE, D_MODEL), lambda i: (i, 0)),
            pl.BlockSpec((None, None, N_MOD, D_MODEL), lambda i: (l, _plan_cond(i), 0, 0)),
        ],
        out_specs=pl.BlockSpec((PLAN_TILE, D_MODEL), lambda i: (i, 0)),
        out_shape=jax.ShapeDtypeStruct((N_TOK, D_MODEL), F32),
        scratch_shapes=[pltpu.VMEM((PLAN_TILE, FEAT_ROWS, LANES), F32), pltpu.SemaphoreType.DMA(())],
        compiler_params=_params("arbitrary"),
        name="combine",
    )(dst, expert_out, x1, mod)


def _rope_tables():
    pos = np.arange(DEC_SEQ)
    inv_freq = ROPE_THETA ** (-np.arange(0, AXIS_DIM, 2, dtype=np.float32) / AXIS_DIM)
    ang_r = (pos // GRID_W).astype(np.float32)[:, None] * inv_freq[None, :]
    ang_c = (pos % GRID_W).astype(np.float32)[:, None] * inv_freq[None, :]
    ang = jnp.asarray(np.concatenate([ang_r, ang_r, ang_c, ang_c], axis=-1).astype(np.float32))
    sign = np.where(np.arange(HEAD_DIM) % AXIS_DIM < AXIS_DIM // 2, -1.0, 1.0).astype(np.float32)
    cos = jnp.concatenate([jnp.cos(ang), jnp.ones((TOK_TILE, HEAD_DIM), F32)], axis=0)
    sin = jnp.concatenate([jnp.sin(ang) * sign, jnp.zeros((TOK_TILE, HEAD_DIM), F32)], axis=0)
    return jnp.tile(cos, (1, N_Q_HEADS)), jnp.tile(sin, (1, N_Q_HEADS))


def _selectors():
    lane = np.arange(N_EXPERTS)
    member = np.stack([(lane[:, None] == (lane[None, :] // EXPERTS_PER_GROUP) * EXPERTS_PER_GROUP + k)
                       for k in range(EXPERTS_PER_GROUP)]).astype(np.float32)
    group = np.stack([np.broadcast_to(lane[:, None] == m * EXPERTS_PER_GROUP, (N_EXPERTS, N_EXPERTS))
                      for m in range(N_GROUPS)]).astype(np.float32)
    head = (np.arange(ATTN_WIDTH)[:, None] // HEAD_DIM == np.arange(ATTN_WIDTH)[None, :] // HEAD_DIM)
    return jnp.asarray(member), jnp.asarray(group), jnp.asarray(head.astype(np.float32), dtype=BF16)


def kernel(x_prompt, x_sample, cache_k, cache_v, state_fwd, state_bwd, c, c_ctx, norm1_w, norm2_w, w_mod, b_mod, w_in, q_norm_w, k_norm_w, hgrn_lb_fwd, hgrn_lb_bwd, hgrn_norm_w, w_br_attn, w_br_hgrn, w_out, w_router, b_router, w_exp_gate, w_exp_up, w_exp_down):
    cos_t, sin_t = _rope_tables()
    sel_member, sel_group, head_ones = _selectors()
    sums_f, masks_f = _hgrn_tables(False)
    sums_b, masks_b = _hgrn_tables(True)
    hgrn_tables = (sums_f, sums_b, masks_f, masks_b)
    cond = jnp.concatenate([c_ctx[None, :], c, jnp.zeros((N_COND - 1 - DEC_BATCH, D_MODEL), F32)], axis=0)
    mod = _modulation(cond, w_mod, b_mod)

    w_in_b = w_in.astype(BF16)
    w_ba_b = w_br_attn.astype(BF16)
    w_bh_b = w_br_hgrn.astype(BF16)
    w_out_b = w_out.astype(BF16)
    by_group = lambda w: w.astype(BF16).reshape(DEPTH, N_GROUPS, EXPERTS_PER_GROUP, *w.shape[2:])
    w_eg_b, w_eu_b, w_ed_b = by_group(w_exp_gate), by_group(w_exp_up), by_group(w_exp_down)
    lane_pad = jnp.eye(N_EXPERTS, LANES, dtype=F32)
    b_router_r = b_router.reshape(1, N_EXPERTS)
    norm1_r = norm1_w.reshape(DEPTH, 1, D_MODEL)
    norm2_r = norm2_w.reshape(DEPTH, 1, D_MODEL)
    hnorm_r = hgrn_norm_w.reshape(DEPTH, 1, HGRN_V)
    qn_r = jnp.tile(q_norm_w, (1, N_Q_HEADS)).reshape(DEPTH, 1, ATTN_WIDTH)
    kn_r = jnp.tile(k_norm_w, (1, N_KV_HEADS)).reshape(DEPTH, 1, KV_WIDTH)

    x = jnp.concatenate([x_prompt.reshape(N_CTX, D_MODEL), x_sample.reshape(N_LAT, D_MODEL)], axis=0)
    ks_out, vs_out, sf_out, sb_out = [], [], [], []
    for l in range(DEPTH):
        q, k, v, h5, gates = _input_projection(l, x, mod, norm1_r, w_in_b, cos_t, sin_t, qn_r, kn_r,
                                               head_ones)
        attn = (_attention_ctx(q, k, v), _attention_lat(l, q, k, v, cache_k, cache_v))
        rec_c, sf, sb = _hgrn(l, h5, 0, BATCH, SEQ, hgrn_lb_fwd, hgrn_lb_bwd, hnorm_r, hgrn_tables)
        rec_s = _hgrn(l, h5, N_CTX, DEC_BATCH, DEC_SEQ, hgrn_lb_fwd, hgrn_lb_bwd, hnorm_r,
                      hgrn_tables, init=(state_fwd, state_bwd))
        rec = (rec_c, rec_s)
        x1, slab, route = _merge(l, attn, rec, gates, x, mod, norm2_r, w_ba_b, w_bh_b, w_out_b,
                                 w_router, b_router_r, sel_member, sel_group, lane_pad)
        dst, tile_group, n_used = _plan(route)
        expert_out = _experts(l, tile_group, n_used, _invert(dst), slab, w_eg_b, w_eu_b, w_ed_b)
        x = _combine(l, dst, expert_out, x1, mod)
        ks_out.append(k[:N_CTX].reshape(BATCH, SEQ, N_KV_HEADS, HEAD_DIM).transpose(0, 2, 1, 3))
        vs_out.append(v[:N_CTX].reshape(BATCH, SEQ, N_KV_HEADS, HEAD_DIM).transpose(0, 2, 1, 3))
        sf_out.append(sf)
        sb_out.append(sb)

    y_prompt = x[:N_CTX].reshape(BATCH, SEQ, D_MODEL)
    y_sample = x[N_CTX:].reshape(DEC_BATCH, DEC_SEQ, D_MODEL)
    return (y_prompt, y_sample, jnp.stack(ks_out, axis=1), jnp.stack(vs_out, axis=1),
            jnp.stack(sf_out, axis=1), jnp.stack(sb_out, axis=1))
```

```python
import functools

import numpy as np
import jax
import jax.numpy as jnp
from jax import lax
from jax.experimental import pallas as pl
from jax.experimental.pallas import tpu as pltpu

F32 = jnp.float32
BF16 = jnp.bfloat16

D_MODEL = 1024
BATCH = 16
SEQ = 256
DEPTH = 4
DEC_BATCH = 4
DEC_SEQ = 2048
PAST_LEN = 512
GRID_W = 64
N_Q_HEADS = 8
N_KV_HEADS = 2
GQA_GROUP = N_Q_HEADS // N_KV_HEADS
HEAD_DIM = 64
AXIS_DIM = HEAD_DIM // 2
ATTN_WIDTH = N_Q_HEADS * HEAD_DIM
KV_WIDTH = N_KV_HEADS * HEAD_DIM
ROPE_THETA = 10000.0
HGRN_HEADS = 4
HGRN_K = 128
HGRN_V = 128
HGRN_WIDTH = HGRN_HEADS * HGRN_K
N_EXPERTS = 16
N_GROUPS = 4
EXPERTS_PER_GROUP = N_EXPERTS // N_GROUPS
D_FF = 512
EPS = 1e-6
IN_COLS = ATTN_WIDTH + 2 * KV_WIDTH + 5 * HGRN_WIDTH + 2 * D_MODEL

N_CTX = BATCH * SEQ
N_LAT = DEC_BATCH * DEC_SEQ
N_TOK = N_CTX + N_LAT
N_COND = 8
N_MOD = 6

C_Q = 0
C_K = C_Q + ATTN_WIDTH
C_V = C_K + KV_WIDTH
C_HQ = C_V + KV_WIDTH
C_FF = C_HQ + HGRN_WIDTH
C_FB = C_FF + HGRN_WIDTH
C_HI = C_FB + HGRN_WIDTH
C_HG = C_HI + HGRN_WIDTH
C_GA = C_HG + HGRN_WIDTH
C_GH = C_GA + D_MODEL

LANES = 128
SUBLANES = 8
FEAT_ROWS = D_MODEL // LANES
SLAB_ROWS = 2 * FEAT_ROWS

TOK_TILE = 512
PLAN_TILE = 1024
SORT_TILE = 512
N_SORT_TILES = N_TOK // SORT_TILE + N_GROUPS
Q_TILE = 512
CHUNK = 64
HGRN_LEVELS = (32, 16, 8, 4, 2, 1)
BLOCK_CHUNKS = 8
LOG2E = 1.4426950408889634
VMEM_LIMIT = 56 * 1024 * 1024

_HI = lax.Precision.HIGHEST


def _sigmoid(x):
    return 1.0 / (1.0 + jnp.exp(-x))


def _dot(a, b):
    return jnp.dot(a, b, preferred_element_type=F32)


def _dot_nt(a, b, precision=None):
    return lax.dot_general(a, b, (((1,), (1,)), ((), ())), precision=precision,
                           preferred_element_type=F32)


def _dot_tn(a, b, precision=None):
    return lax.dot_general(a, b, (((0,), (0,)), ((), ())), precision=precision,
                           preferred_element_type=F32)


def _dot_exact(a, b):
    return jnp.dot(a, b, precision=_HI, preferred_element_type=F32)


def _split_bf16(a):
    hi = a.astype(BF16)
    return hi, (a - hi.astype(F32)).astype(BF16)


def _params(*sem):
    return pltpu.CompilerParams(dimension_semantics=sem, vmem_limit_bytes=VMEM_LIMIT)


def _mod_kernel(cond_ref, w_ref, b_ref, out_ref):
    c = cond_ref[...]
    a_hi, a_lo = _split_bf16(c * _sigmoid(c))
    w_hi, w_lo = _split_bf16(w_ref[...])
    p = _dot(jnp.concatenate([a_hi, a_lo], axis=0), w_hi)
    out_ref[...] = p[:N_COND] + p[N_COND:] + _dot(a_hi, w_lo) + b_ref[...]


def _modulation(cond, w_mod, b_mod):
    out = pl.pallas_call(
        _mod_kernel,
        grid=(DEPTH, N_MOD),
        in_specs=[
            pl.BlockSpec((N_COND, D_MODEL), lambda l, j: (0, 0)),
            pl.BlockSpec((None, D_MODEL, D_MODEL), lambda l, j: (l, 0, j)),
            pl.BlockSpec((None, 1, D_MODEL), lambda l, j: (l, 0, j)),
        ],
        out_specs=pl.BlockSpec((None, None, N_COND, D_MODEL), lambda l, j: (l, j, 0, 0)),
        out_shape=jax.ShapeDtypeStruct((DEPTH, N_MOD, N_COND, D_MODEL), F32),
        compiler_params=_params("arbitrary", "arbitrary"),
        name="modulation",
    )(cond, w_mod, b_mod.reshape(DEPTH, 1, N_MOD * D_MODEL))
    return out.transpose(0, 2, 1, 3)


def _head_mean_sq(a, bd):
    hi, lo = _split_bf16(a * a)
    return (_dot(hi, bd) + _dot(lo, bd)) * (1.0 / HEAD_DIM)


def _rope(x, cos, sin_signed):
    width = x.shape[-1]
    lane = lax.broadcasted_iota(jnp.int32, x.shape, 1)
    first = (lane & (AXIS_DIM - 1)) < (AXIS_DIM // 2)
    rot = jnp.where(first, pltpu.roll(x, width - AXIS_DIM // 2, 1), pltpu.roll(x, AXIS_DIM // 2, 1))
    return x * cos + rot * sin_signed


def _inproj_kernel(x_ref, mod_ref, n1_ref, w_ref, cos_ref, sin_ref, qn_ref, kn_ref, bd_ref,
                   q_ref, k_ref, v_ref, h_ref, g_ref):
    x = x_ref[...]
    xn = x * lax.rsqrt(jnp.mean(x * x, axis=-1, keepdims=True) + EPS) * n1_ref[...]
    xb = (xn * (1.0 + mod_ref[1:2, :]) + mod_ref[0:1, :]).astype(BF16)

    def proj(c0, width):
        return _dot(xb, w_ref[:, c0:c0 + width])

    a = proj(C_Q, ATTN_WIDTH)
    qn = a * lax.rsqrt(_head_mean_sq(a, bd_ref[...]) + EPS) * qn_ref[...]
    q_ref[...] = (_rope(qn, cos_ref[...], sin_ref[...]) * (HEAD_DIM ** -0.5)).astype(BF16)

    a = proj(C_K, KV_WIDTH)
    kn = a * lax.rsqrt(_head_mean_sq(a, bd_ref[:KV_WIDTH, :KV_WIDTH]) + EPS) * kn_ref[...]
    k_ref[...] = _rope(kn, cos_ref[:, :KV_WIDTH], sin_ref[:, :KV_WIDTH])
    v_ref[...] = proj(C_V, KV_WIDTH)

    a = proj(C_HQ, HGRN_WIDTH)
    h_ref[:, 0:HGRN_WIDTH] = a * _sigmoid(a) * (HGRN_K ** -0.5)
    h_ref[:, HGRN_WIDTH:2 * HGRN_WIDTH] = proj(C_FF, HGRN_WIDTH)
    h_ref[:, 2 * HGRN_WIDTH:3 * HGRN_WIDTH] = proj(C_FB, HGRN_WIDTH)
    h_ref[:, 3 * HGRN_WIDTH:4 * HGRN_WIDTH] = proj(C_HI, HGRN_WIDTH)
    a = proj(C_HG, HGRN_WIDTH)
    h_ref[:, 4 * HGRN_WIDTH:5 * HGRN_WIDTH] = a * _sigmoid(a)

    g_ref[:, 0:D_MODEL] = _sigmoid(proj(C_GA, D_MODEL))
    g_ref[:, D_MODEL:2 * D_MODEL] = _sigmoid(proj(C_GH, D_MODEL))


_CTX_TILES = N_CTX // TOK_TILE
_LAT_TILES_PER_SEQ = DEC_SEQ // TOK_TILE


def _tile_cond(i):
    return jnp.where(i < _CTX_TILES, 0, 1 + (i - _CTX_TILES) // _LAT_TILES_PER_SEQ)


def _tile_rope_block(i):
    return jnp.where(i < _CTX_TILES, _LAT_TILES_PER_SEQ, (i - _CTX_TILES) % _LAT_TILES_PER_SEQ)


def _input_projection(l, x, mod, norm1_w, w_in, cos_t, sin_t, qn_w, kn_w, bd):
    row = lambda w: pl.BlockSpec((TOK_TILE, w), lambda i: (i, 0))
    return pl.pallas_call(
        _inproj_kernel,
        grid=(N_TOK // TOK_TILE,),
        in_specs=[
            row(D_MODEL),
            pl.BlockSpec((None, None, N_MOD, D_MODEL), lambda i: (l, _tile_cond(i), 0, 0)),
            pl.BlockSpec((None, 1, D_MODEL), lambda i: (l, 0, 0)),
            pl.BlockSpec((None, D_MODEL, IN_COLS), lambda i: (l, 0, 0),
                         pipeline_mode=pl.Buffered(1)),
            pl.BlockSpec((TOK_TILE, ATTN_WIDTH), lambda i: (_tile_rope_block(i), 0)),
            pl.BlockSpec((TOK_TILE, ATTN_WIDTH), lambda i: (_tile_rope_block(i), 0)),
            pl.BlockSpec((None, 1, ATTN_WIDTH), lambda i: (l, 0, 0)),
            pl.BlockSpec((None, 1, KV_WIDTH), lambda i: (l, 0, 0)),
            pl.BlockSpec((ATTN_WIDTH, ATTN_WIDTH), lambda i: (0, 0)),
        ],
        out_specs=[row(ATTN_WIDTH), row(KV_WIDTH), row(KV_WIDTH), row(5 * HGRN_WIDTH),
                   row(2 * D_MODEL)],
        out_shape=[
            jax.ShapeDtypeStruct((N_TOK, ATTN_WIDTH), BF16),
            jax.ShapeDtypeStruct((N_TOK, KV_WIDTH), F32),
            jax.ShapeDtypeStruct((N_TOK, KV_WIDTH), F32),
            jax.ShapeDtypeStruct((N_TOK, 5 * HGRN_WIDTH), F32),
            jax.ShapeDtypeStruct((N_TOK, 2 * D_MODEL), F32),
        ],
        compiler_params=_params("arbitrary"),
        name="input_projection",
    )(x, mod, norm1_w, w_in, cos_t, sin_t, qn_w, kn_w, bd)


def _softmax_pv(scores, values):
    m = scores[0].max(axis=-1, keepdims=True)
    for s in scores[1:]:
        m = jnp.maximum(m, s.max(axis=-1, keepdims=True))
    num, den = None, None
    for s, v in zip(scores, values):
        p = jnp.exp(s - m)
        d = p.sum(axis=-1, keepdims=True)
        o = _dot(p.astype(BF16), v)
        num = o if num is None else num + o
        den = d if den is None else den + d
    return num / den


def _attn_ctx_kernel(q_ref, k_ref, v_ref, o_ref):
    for g in range(N_KV_HEADS):
        cols = slice(g * HEAD_DIM, (g + 1) * HEAD_DIM)
        kg = k_ref[:, cols].astype(BF16)
        vg = v_ref[:, cols].astype(BF16)
        for hh in range(GQA_GROUP):
            h = g * GQA_GROUP + hh
            hc = slice(h * HEAD_DIM, (h + 1) * HEAD_DIM)
            o = _softmax_pv([_dot_nt(q_ref[:, hc], kg)], [vg])
            o_ref[:, hc] = o.astype(BF16)


def _attention_ctx(q, k, v):
    return pl.pallas_call(
        _attn_ctx_kernel,
        grid=(BATCH,),
        in_specs=[
            pl.BlockSpec((SEQ, ATTN_WIDTH), lambda b: (b, 0)),
            pl.BlockSpec((SEQ, KV_WIDTH), lambda b: (b, 0)),
            pl.BlockSpec((SEQ, KV_WIDTH), lambda b: (b, 0)),
        ],
        out_specs=pl.BlockSpec((SEQ, ATTN_WIDTH), lambda b: (b, 0)),
        out_shape=jax.ShapeDtypeStruct((N_CTX, ATTN_WIDTH), BF16),
        compiler_params=_params("arbitrary"),
        name="attention_ctx",
    )(q, k, v)


def _attn_lat_kernel(q_ref, k_ref, v_ref, ck_ref, cv_ref, o_ref):
    for g in range(N_KV_HEADS):
        cols = slice(g * HEAD_DIM, (g + 1) * HEAD_DIM)
        kg = k_ref[:, cols].astype(BF16)
        vg = v_ref[:, cols].astype(BF16)
        ckg = ck_ref[g].astype(BF16)
        cvg = cv_ref[g].astype(BF16)
        for hh in range(GQA_GROUP):
            h = g * GQA_GROUP + hh
            hc = slice(h * HEAD_DIM, (h + 1) * HEAD_DIM)
            qh = q_ref[:, hc]
            o = _softmax_pv([_dot_nt(qh, kg), _dot_nt(qh, ckg)], [vg, cvg])
            o_ref[:, hc] = o.astype(BF16)


def _attention_lat(l, q, k, v, cache_k, cache_v):
    q_blocks = DEC_SEQ // Q_TILE
    q_off = N_CTX // Q_TILE
    kv_off = N_CTX // DEC_SEQ
    cache_spec = pl.BlockSpec((None, None, N_KV_HEADS, PAST_LEN, HEAD_DIM),
                              lambda b, i: (b, l, 0, 0, 0))
    return pl.pallas_call(
        _attn_lat_kernel,
        grid=(DEC_BATCH, q_blocks),
        in_specs=[
            pl.BlockSpec((Q_TILE, ATTN_WIDTH), lambda b, i: (q_off + b * q_blocks + i, 0)),
            pl.BlockSpec((DEC_SEQ, KV_WIDTH), lambda b, i: (kv_off + b, 0)),
            pl.BlockSpec((DEC_SEQ, KV_WIDTH), lambda b, i: (kv_off + b, 0)),
            cache_spec,
            cache_spec,
        ],
        out_specs=pl.BlockSpec((Q_TILE, ATTN_WIDTH), lambda b, i: (b * q_blocks + i, 0)),
        out_shape=jax.ShapeDtypeStruct((N_LAT, ATTN_WIDTH), BF16),
        compiler_params=_params("arbitrary", "arbitrary"),
        name="attention_lat",
    )(q, k, v, cache_k, cache_v)


def _lower_bound(lb_ref, l):
    z = lb_ref[...]
    e = jnp.exp(z - z.max(axis=0, keepdims=True))
    p = e / e.sum(axis=0, keepdims=True)
    acc = p[0:1] * 0.0
    for r in range(1, l + 1):
        acc = acc + p[r:r + 1]
    return acc


def _hgrn_tables(rev):
    t = np.arange(CHUNK)
    tt, uu = t[:, None], t[None, :]
    groups = []
    masks = []
    for h in HGRN_LEVELS:
        right = (t & h) != 0
        edge = ((t // (2 * h)) * 2 * h + h - 1)[:, None]
        groups.append(np.where(right[:, None], (uu > edge) & (uu <= tt), (uu > tt) & (uu <= edge)))
        same = (tt // (2 * h)) == (uu // (2 * h))
        masks.append(same & right[:, None] & ~right[None, :])
    masks.append(tt == uu)
    groups += [uu <= tt, uu > tt]
    sums = np.stack(groups).astype(np.float32)
    masks = np.stack(masks).astype(np.float32)
    if rev:
        sums = sums[:, ::-1, ::-1]
        masks = masks[:, ::-1, ::-1]
    sums = sums.reshape(-1, CHUNK)
    return (jnp.asarray(np.concatenate([sums, sums], axis=1), dtype=BF16),
            jnp.asarray(np.ascontiguousarray(masks)))


def _hgrn_block(q, x, v, lb, sums_ref, masks_ref, st_ref, rev):
    n = q.shape[0] // CHUNK
    f = lb + (1.0 - lb) * _sigmoid(x)
    kk = 1.0 - f
    g = jnp.log(f) * LOG2E
    g_hi, g_lo = _split_bf16(g)
    vb = v.astype(BF16)
    rid = lax.broadcasted_iota(jnp.int32, (CHUNK, HGRN_K), 0)
    edge = 0 if rev else CHUNK - 1
    n_lev = len(HGRN_LEVELS)
    outs = [None] * n
    for c in (reversed(range(n)) if rev else range(n)):
        rs = slice(c * CHUNK, (c + 1) * CHUNK)
        e = jnp.exp2(_dot(sums_ref[...], jnp.concatenate([g_hi[rs], g_lo[rs]], axis=0)))
        q_c, k_c, v_c = q[rs], kk[rs], vb[rs]
        a = _dot_nt(q_c.astype(BF16), k_c.astype(BF16)) * masks_ref[n_lev]
        for i, h in enumerate(HGRN_LEVELS):
            later = ((rid & h) == 0) if rev else ((rid & h) != 0)
            z = (jnp.where(later, q_c, k_c) * e[i * CHUNK:(i + 1) * CHUNK]).astype(BF16)
            a = a + _dot_nt(z, z) * masks_ref[i]
        e_in = e[n_lev * CHUNK:(n_lev + 1) * CHUNK]
        e_out = e[(n_lev + 1) * CHUNK:(n_lev + 2) * CHUNK]
        st = st_ref[...]
        outs[c] = (_dot(a.astype(BF16), v_c)
                   + _dot_nt((q_c * e_in).astype(BF16), st.astype(BF16)))
        k_dec = (k_c * e_out).astype(BF16)
        st_ref[...] = st * e_in[edge:edge + 1] + _dot_tn(v_c, k_dec)
    return jnp.concatenate(outs, axis=0)


def _hgrn_kernel(l, seq_len, has_init, *refs):
    (hq_ref, ff_ref, fb_ref, hi_ref, hg_ref, lbf_ref, lbb_ref, hn_ref,
     sumf_ref, sumb_ref, mskf_ref, mskb_ref) = refs[:12]
    refs = refs[12:]
    if has_init:
        s0f_ref, s0b_ref, rec_ref, of_ref, ob_ref, stf_ref, stb_ref = refs
    else:
        rec_ref, sf_ref, sb_ref, of_ref, ob_ref, stf_ref, stb_ref = refs
    lb_f = _lower_bound(lbf_ref, l)
    lb_b = _lower_bound(lbb_ref, l)
    if has_init:
        stf_ref[...] = s0f_ref[...].T
        stb_ref[...] = s0b_ref[...].T
    else:
        stf_ref[...] = jnp.zeros((HGRN_V, HGRN_K), F32)
        stb_ref[...] = jnp.zeros((HGRN_V, HGRN_K), F32)
    rows = min(BLOCK_CHUNKS * CHUNK, seq_len)
    n_blocks = seq_len // rows

    def step(rf, rb):
        of_ref[rf, :] = _hgrn_block(hq_ref[rf, :], ff_ref[rf, :], hi_ref[rf, :], lb_f,
                                    sumf_ref, mskf_ref, stf_ref, False)
        ob_ref[rb, :] = _hgrn_block(hq_ref[rb, :], fb_ref[rb, :], hi_ref[rb, :], lb_b,
                                    sumb_ref, mskb_ref, stb_ref, True)

    if n_blocks == 1:
        step(pl.ds(0, rows), pl.ds(0, rows))
    else:
        def body(i, carry):
            step(pl.ds(pl.multiple_of(i * rows, rows), rows),
                 pl.ds(pl.multiple_of((n_blocks - 1 - i) * rows, rows), rows))
            return carry

        lax.fori_loop(0, n_blocks, body, 0)
    o = of_ref[...] + ob_ref[...]
    o = o * lax.rsqrt(jnp.mean(o * o, axis=-1, keepdims=True) + EPS) * hn_ref[...]
    rec_ref[...] = o * hg_ref[...]
    if not has_init:
        sf_ref[...] = stf_ref[...].T
        sb_ref[...] = stb_ref[...].T


def _hgrn(l, h5, row_off, n_seq, seq_len, lb_fwd, lb_bwd, hnorm_w, tables, init=None):
    blocks_off = row_off // seq_len
    part = lambda k: pl.BlockSpec((seq_len, HGRN_K),
                                  lambda b, h: (blocks_off + b, k * HGRN_HEADS + h))
    lb_spec = pl.BlockSpec((DEPTH, HGRN_K), lambda b, h: (0, h))
    sum_spec = pl.BlockSpec((8 * CHUNK, 2 * CHUNK), lambda b, h: (0, 0))
    msk_spec = pl.BlockSpec((len(HGRN_LEVELS) + 1, CHUNK, CHUNK), lambda b, h: (0, 0, 0))
    in_specs = [part(0), part(1), part(2), part(3), part(4), lb_spec, lb_spec,
                pl.BlockSpec((None, 1, HGRN_V), lambda b, h: (l, 0, 0)),
                sum_spec, sum_spec, msk_spec, msk_spec]
    args = [h5, h5, h5, h5, h5, lb_fwd, lb_bwd, hnorm_w, *tables]
    rec_spec = pl.BlockSpec((seq_len, HGRN_V), lambda b, h: (b, h))
    rec_shape = jax.ShapeDtypeStruct((n_seq * seq_len, HGRN_WIDTH), F32)
    if init is not None:
        st_spec = pl.BlockSpec((None, None, None, HGRN_K, HGRN_V), lambda b, h: (b, l, h, 0, 0))
        in_specs += [st_spec, st_spec]
        args += list(init)
        out_specs, out_shape = rec_spec, rec_shape
    else:
        st_spec = pl.BlockSpec((None, None, HGRN_K, HGRN_V), lambda b, h: (b, h, 0, 0))
        st_shape = jax.ShapeDtypeStruct((n_seq, HGRN_HEADS, HGRN_K, HGRN_V), F32)
        out_specs, out_shape = [rec_spec, st_spec, st_spec], [rec_shape, st_shape, st_shape]
    return pl.pallas_call(
        functools.partial(_hgrn_kernel, l, seq_len, init is not None),
        grid=(n_seq, HGRN_HEADS),
        in_specs=in_specs,
        out_specs=out_specs,
        out_shape=out_shape,
        scratch_shapes=[pltpu.VMEM((seq_len, HGRN_V), F32), pltpu.VMEM((seq_len, HGRN_V), F32),
                        pltpu.VMEM((HGRN_V, HGRN_K), F32), pltpu.VMEM((HGRN_V, HGRN_K), F32)],
        compiler_params=_params("arbitrary", "arbitrary"),
        name="hgrn_lat" if init is not None else "hgrn_ctx",
    )(*args)


def _route(scores, bias, sel_member, sel_group):
    sel = scores + bias
    members = [_dot_exact(sel, sel_member[k]) for k in range(EXPERTS_PER_GROUP)]
    pair = None
    for i in range(EXPERTS_PER_GROUP):
        for j in range(i + 1, EXPERTS_PER_GROUP):
            s = members[i] + members[j]
            pair = s if pair is None else jnp.maximum(pair, s)
    lane = lax.broadcasted_iota(jnp.int32, sel.shape, 1)
    best = _dot_exact(pair, sel_group[0])
    best_g = jnp.zeros(sel.shape, jnp.int32)
    for m in range(1, N_GROUPS):
        cand = _dot_exact(pair, sel_group[m])
        upd = cand > best
        best_g = jnp.where(upd, m, best_g)
        best = jnp.where(upd, cand, best)
    in_group = (lane >> 2) == best_g
    pos = lane & (EXPERTS_PER_GROUP - 1)
    rank = jnp.zeros(sel.shape, jnp.int32)
    for k in range(EXPERTS_PER_GROUP):
        ahead = (members[k] > sel) | ((members[k] == sel) & (k < pos))
        rank = rank + jnp.where(ahead, 1, 0)
    w = jnp.where(in_group & (rank < 2), scores, 0.0)
    return w / w.sum(axis=-1, keepdims=True)


def _merge_kernel(attn_c_ref, attn_s_ref, rec_c_ref, rec_s_ref, g_ref, x_ref, mod_ref, n2_ref,
                  wa_ref, wh_ref, wo_ref, wr_ref, br_ref, selm_ref, selg_ref, pad_ref,
                  x1_ref, slab_ref, route_ref):
    is_ctx = pl.program_id(0) < _CTX_TILES
    attn = jnp.where(is_ctx, attn_c_ref[...], attn_s_ref[...])
    rec = jnp.where(is_ctx, rec_c_ref[...], rec_s_ref[...])
    ya = _dot(attn, wa_ref[...])
    yh = _dot(rec.astype(BF16), wh_ref[...])
    merged = g_ref[:, 0:D_MODEL] * ya + g_ref[:, D_MODEL:2 * D_MODEL] * yh
    out = _dot(merged.astype(BF16), wo_ref[...])
    x1 = x_ref[...] + mod_ref[2:3, :] * out
    x1_ref[...] = x1
    xn = x1 * lax.rsqrt(jnp.mean(x1 * x1, axis=-1, keepdims=True) + EPS) * n2_ref[...]
    h2 = xn * (1.0 + mod_ref[4:5, :]) + mod_ref[3:4, :]
    h_hi, h_lo = _split_bf16(h2)
    w_hi, w_lo = _split_bf16(wr_ref[...])
    scores = _sigmoid(_dot(jnp.concatenate([h_hi, h_lo, h_hi], axis=1),
                           jnp.concatenate([w_hi, w_hi, w_lo], axis=0)))
    gates = _route(scores, br_ref[...], [selm_ref[k] for k in range(EXPERTS_PER_GROUP)],
                   [selg_ref[m] for m in range(N_GROUPS)])
    gates = _dot_exact(gates, pad_ref[...])
    route_ref[...] = gates
    for s in range(FEAT_ROWS):
        slab_ref[:, s, :] = h2[:, s * LANES:(s + 1) * LANES]
    slab_ref[:, FEAT_ROWS, :] = gates
    for s in range(FEAT_ROWS + 1, SLAB_ROWS):
        slab_ref[:, s, :] = jnp.zeros_like(gates)


def _merge(l, attn, rec, gates, x, mod, norm2_w, w_br_attn, w_br_hgrn, w_out, w_router, b_router,
           sel_member, sel_group, lane_pad):
    row = lambda w: pl.BlockSpec((TOK_TILE, w), lambda i: (i, 0))
    ctx_row = lambda w: pl.BlockSpec((TOK_TILE, w), lambda i: (jnp.minimum(i, _CTX_TILES - 1), 0))
    lat_row = lambda w: pl.BlockSpec((TOK_TILE, w), lambda i: (jnp.maximum(i - _CTX_TILES, 0), 0))
    layer = lambda r, c: pl.BlockSpec((None, r, c), lambda i: (l, 0, 0))
    full = lambda *s: pl.BlockSpec(s, lambda i: (0,) * len(s))
    return pl.pallas_call(
        _merge_kernel,
        grid=(N_TOK // TOK_TILE,),
        in_specs=[
            ctx_row(ATTN_WIDTH), lat_row(ATTN_WIDTH), ctx_row(HGRN_WIDTH), lat_row(HGRN_WIDTH),
            row(2 * D_MODEL), row(D_MODEL),
            pl.BlockSpec((None, None, N_MOD, D_MODEL), lambda i: (l, _tile_cond(i), 0, 0)),
            layer(1, D_MODEL),
            layer(ATTN_WIDTH, D_MODEL), layer(HGRN_WIDTH, D_MODEL), layer(D_MODEL, D_MODEL),
            full(D_MODEL, N_EXPERTS), full(1, N_EXPERTS),
            full(EXPERTS_PER_GROUP, N_EXPERTS, N_EXPERTS), full(N_GROUPS, N_EXPERTS, N_EXPERTS),
            full(N_EXPERTS, LANES),
        ],
        out_specs=[row(D_MODEL),
                   pl.BlockSpec((TOK_TILE, SLAB_ROWS, LANES), lambda i: (i, 0, 0)),
                   row(LANES)],
        out_shape=[
            jax.ShapeDtypeStruct((N_TOK, D_MODEL), F32),
            jax.ShapeDtypeStruct((N_TOK, SLAB_ROWS, LANES), F32),
            jax.ShapeDtypeStruct((N_TOK, LANES), F32),
        ],
        compiler_params=_params("arbitrary"),
        name="merge_router",
    )(*attn, *rec, gates, x, mod, norm2_w, w_br_attn, w_br_hgrn, w_out, w_router, b_router,
      sel_member, sel_group, lane_pad)


def _plan_kernel(route_ref, dst_ref, meta_ref, tot_ref, run_ref):
    p, i = pl.program_id(0), pl.program_id(1)

    @pl.when(jnp.logical_and(p == 0, i == 0))
    def _():
        tot_ref[...] = jnp.zeros_like(tot_ref)

    @pl.when(i == 0)
    def _():
        run_ref[...] = jnp.zeros_like(run_ref)

    grp = lax.broadcasted_iota(jnp.int32, (SUBLANES, LANES), 0)
    lane = lax.broadcasted_iota(jnp.int32, (SUBLANES, LANES), 1)
    member = jnp.where(((lane >> 2) == grp) & (lane < N_EXPERTS), 1.0, 0.0)
    one_hot = jnp.where(_dot_nt(member, route_ref[...], precision=_HI) > 0.0, 1.0, 0.0)
    src = lax.broadcasted_iota(jnp.int32, (PLAN_TILE, PLAN_TILE), 0)
    tgt = lax.broadcasted_iota(jnp.int32, (PLAN_TILE, PLAN_TILE), 1)
    before = jnp.where(src < tgt, 1.0, 0.0).astype(BF16)
    rank = _dot(one_hot.astype(BF16), before) + run_ref[:, 0:1]
    tile_tot = one_hot.sum(axis=1, keepdims=True)

    @pl.when(p == 0)
    def _():
        tot_ref[...] += tile_tot

    @pl.when(p == 1)
    def _():
        run_ref[...] += tile_tot
        tot = tot_ref[:, 0:1]
        n_tiles = jnp.floor((tot + (SORT_TILE - 1)) * (1.0 / SORT_TILE))
        offs, acc = [], jnp.zeros((1, 1), F32)
        for m in range(N_GROUPS):
            offs.append(acc * SORT_TILE)
            acc = acc + n_tiles[m:m + 1]
        dst = sum(one_hot[m:m + 1] * (offs[m] + rank[m:m + 1]) for m in range(N_GROUPS))
        for r in range(PLAN_TILE // LANES):
            dst_ref[r:r + 1, :] = dst[:, r * LANES:(r + 1) * LANES].astype(jnp.int32)
        tile_row = (lax.broadcasted_iota(jnp.int32, (1, LANES), 1) * SORT_TILE).astype(F32)
        tile_group = sum(jnp.where(tile_row >= offs[m], 1.0, 0.0) for m in range(1, N_GROUPS))
        rows = [tile_group, jnp.broadcast_to(acc, (1, LANES))]
        rows += [jnp.zeros((1, LANES), F32)] * (SUBLANES - len(rows))
        meta_ref[...] = jnp.concatenate(rows, axis=0).astype(jnp.int32)


def _plan(route):
    n_blocks = N_TOK // PLAN_TILE
    dst, meta = pl.pallas_call(
        _plan_kernel,
        grid=(2, n_blocks),
        in_specs=[pl.BlockSpec((PLAN_TILE, LANES), lambda p, i: (i, 0))],
        out_specs=[pl.BlockSpec((PLAN_TILE // LANES, LANES), lambda p, i: (i * p, 0)),
                   pl.BlockSpec((SUBLANES, LANES), lambda p, i: (0, 0))],
        out_shape=[jax.ShapeDtypeStruct((N_TOK // LANES, LANES), jnp.int32),
                   jax.ShapeDtypeStruct((SUBLANES, LANES), jnp.int32)],
        scratch_shapes=[pltpu.VMEM((SUBLANES, LANES), F32), pltpu.VMEM((SUBLANES, LANES), F32)],
        compiler_params=_params("arbitrary", "arbitrary"),
        name="route_plan",
    )(route)
    return dst, meta[0, :N_SORT_TILES], meta[1, 0:1]


def _start_rows(make_copy):
    def start(i, carry):
        make_copy(2 * i).start(priority=0)
        make_copy(2 * i + 1).start(priority=1)
        return carry

    lax.fori_loop(0, LANES // 2, start, 0, unroll=4)


def _invert_kernel(dst_ref, zero_ref, src_ref, sem):
    i = pl.program_id(0)

    @pl.when(i == 0)
    def _():
        clear = pltpu.make_async_copy(zero_ref, src_ref, sem)
        clear.start()
        clear.wait()

    for r in range(PLAN_TILE // LANES):
        def put(c, carry, r=r):
            d = dst_ref[r, c]
            src_ref[lax.shift_right_logical(d, 7), d & (LANES - 1)] = i * PLAN_TILE + r * LANES + c
            return carry

        lax.fori_loop(0, LANES, put, 0, unroll=8)


def _invert(dst):
    shape = (N_SORT_TILES * SORT_TILE // LANES, LANES)
    return pl.pallas_call(
        _invert_kernel,
        grid=(N_TOK // PLAN_TILE,),
        in_specs=[pl.BlockSpec((PLAN_TILE // LANES, LANES), lambda i: (i, 0), memory_space=pltpu.SMEM),
                  pl.BlockSpec(memory_space=pl.ANY)],
        out_specs=pl.BlockSpec(shape, lambda i: (0, 0), memory_space=pltpu.SMEM),
        out_shape=jax.ShapeDtypeStruct(shape, jnp.int32),
        scratch_shapes=[pltpu.SemaphoreType.DMA(())],
        compiler_params=_params("arbitrary"),
        name="invert_plan",
    )(dst, jnp.zeros(shape, jnp.int32))


_SRC_BLOCK_TILES = SUBLANES * LANES // SORT_TILE


def _experts_kernel(tg_ref, nv_ref, src_ref, nxt_ref, slab_ref, wg_ref, wu_ref, wd_ref, y_ref,
                    x_ref, sem):
    j = pl.program_id(0)
    n_used = nv_ref[0]
    groups = SORT_TILE // LANES

    def gather(idx_ref, tile, slot, unrolled):
        first_row = (tile % _SRC_BLOCK_TILES) * groups
        for r in range(groups):
            make = lambda c, r=r: pltpu.make_async_copy(
                slab_ref.at[idx_ref[first_row + r, c]], x_ref.at[slot, r * LANES + c], sem.at[slot])
            if unrolled:
                for c in range(LANES):
                    make(c).start(priority=c % 2)
            else:
                _start_rows(make)

    def wait_rows(slot):
        for r in range(groups):
            pltpu.make_async_copy(slab_ref.at[pl.ds(0, LANES)],
                                  x_ref.at[slot, pl.ds(r * LANES, LANES)], sem.at[slot]).wait()

    @pl.when(j == 0)
    def _():
        gather(src_ref, j, 0, False)

    @pl.when(j < n_used)
    def _():
        slot = j % 2
        wait_rows(slot)
        first_expert = tg_ref[j] * EXPERTS_PER_GROUP
        h = jnp.concatenate([x_ref[slot, :, s, :] for s in range(FEAT_ROWS)], axis=1).astype(BF16)
        gates = x_ref[slot, :, FEAT_ROWS, :]
        gather(nxt_ref, jnp.minimum(j + 1, n_used - 1), 1 - slot, True)
        lane = lax.broadcasted_iota(jnp.int32, gates.shape, 1)
        acc = None
        for k in range(EXPERTS_PER_GROUP):
            a = _dot(h, wg_ref[k])
            u = _dot(h, wu_ref[k])
            ge = jnp.sum(jnp.where(lane == first_expert + k, gates, 0.0), axis=-1, keepdims=True)
            y = _dot((a * _sigmoid(a) * u * ge).astype(BF16), wd_ref[k])
            acc = y if acc is None else acc + y
        for s in range(FEAT_ROWS):
            y_ref[:, s, :] = acc[:, s * LANES:(s + 1) * LANES]

        @pl.when(j + 1 >= n_used)
        def _():
            wait_rows(1 - slot)

    @pl.when(j >= n_used)
    def _():
        y_ref[...] = jnp.zeros_like(y_ref)


def _experts(l, tile_group, n_used, src, slab, w_gate, w_up, w_down):
    tile = lambda j, tg, nv: jnp.minimum(j, nv[0] - 1)
    w_spec = lambda r, c: pl.BlockSpec((None, None, EXPERTS_PER_GROUP, r, c),
                                       lambda j, tg, nv: (l, tg[tile(j, tg, nv)], 0, 0, 0))
    return pl.pallas_call(
        _experts_kernel,
        grid_spec=pltpu.PrefetchScalarGridSpec(
            num_scalar_prefetch=2,
            grid=(N_SORT_TILES,),
            in_specs=[
                pl.BlockSpec((SUBLANES, LANES), lambda j, tg, nv: (j // _SRC_BLOCK_TILES, 0),
                             memory_space=pltpu.SMEM),
                pl.BlockSpec((SUBLANES, LANES),
                             lambda j, tg, nv: (tile(j + 1, tg, nv) // _SRC_BLOCK_TILES, 0),
                             memory_space=pltpu.SMEM),
                pl.BlockSpec(memory_space=pl.ANY),
                w_spec(D_MODEL, D_FF), w_spec(D_MODEL, D_FF), w_spec(D_FF, D_MODEL),
            ],
            out_specs=pl.BlockSpec((SORT_TILE, FEAT_ROWS, LANES), lambda j, tg, nv: (j, 0, 0)),
            scratch_shapes=[pltpu.VMEM((2, SORT_TILE, SLAB_ROWS, LANES), F32),
                            pltpu.SemaphoreType.DMA((2,))],
        ),
        out_shape=jax.ShapeDtypeStruct((N_SORT_TILES * SORT_TILE, FEAT_ROWS, LANES), F32),
        compiler_params=_params("arbitrary"),
        name="experts",
    )(tile_group, n_used, src, src, slab, w_gate, w_up, w_down)


def _combine_kernel(dst_ref, y_ref, x1_ref, mod_ref, o_ref, buf_ref, sem):
    groups = range(PLAN_TILE // LANES)
    for r in groups:
        _start_rows(lambda c, r=r: pltpu.make_async_copy(
            y_ref.at[dst_ref[r, c]], buf_ref.at[r * LANES + c], sem))
    for r in groups:
        pltpu.make_async_copy(y_ref.at[pl.ds(0, LANES)],
                              buf_ref.at[pl.ds(r * LANES, LANES)], sem).wait()
    for s in range(FEAT_ROWS):
        cols = slice(s * LANES, (s + 1) * LANES)
        o_ref[:, cols] = x1_ref[:, cols] + mod_ref[5:6, cols] * buf_ref[:, s, :]


_PLAN_CTX_TILES = N_CTX // PLAN_TILE
_PLAN_LAT_TILES_PER_SEQ = DEC_SEQ // PLAN_TILE


def _plan_cond(i):
    return jnp.where(i < _PLAN_CTX_TILES, 0, 1 + (i - _PLAN_CTX_TILES) // _PLAN_LAT_TILES_PER_SEQ)


def _combine(l, dst, expert_out, x1, mod):
    return pl.pallas_call(
        _combine_kernel,
        grid=(N_TOK // PLAN_TILE,),
        in_specs=[
            pl.BlockSpec((PLAN_TILE // LANES, LANES), lambda i: (i, 0), memory_space=pltpu.SMEM),
            pl.BlockSpec(memory_space=pl.ANY),
            pl.BlockSpec((PLAN_TILE, D_MODEL), lambda i: (i, 0)),
            pl.BlockSpec((None, None, N_MOD, D_MODEL), lambda i: (l, _plan_cond(i), 0, 0)),
        ],
        out_specs=pl.BlockSpec((PLAN_TILE, D_MODEL), lambda i: (i, 0)),
        out_shape=jax.ShapeDtypeStruct((N_TOK, D_MODEL), F32),
        scratch_shapes=[pltpu.VMEM((PLAN_TILE, FEAT_ROWS, LANES), F32), pltpu.SemaphoreType.DMA(())],
        compiler_params=_params("arbitrary"),
        name="combine",
    )(dst, expert_out, x1, mod)


def _rope_tables():
    pos = np.arange(DEC_SEQ)
    inv_freq = ROPE_THETA ** (-np.arange(0, AXIS_DIM, 2, dtype=np.float32) / AXIS_DIM)
    ang_r = (pos // GRID_W).astype(np.float32)[:, None] * inv_freq[None, :]
    ang_c = (pos % GRID_W).astype(np.float32)[:, None] * inv_freq[None, :]
    ang = jnp.asarray(np.concatenate([ang_r, ang_r, ang_c, ang_c], axis=-1).astype(np.float32))
    sign = np.where(np.arange(HEAD_DIM) % AXIS_DIM < AXIS_DIM // 2, -1.0, 1.0).astype(np.float32)
    cos = jnp.concatenate([jnp.cos(ang), jnp.ones((TOK_TILE, HEAD_DIM), F32)], axis=0)
    sin = jnp.concatenate([jnp.sin(ang) * sign, jnp.zeros((TOK_TILE, HEAD_DIM), F32)], axis=0)
    return jnp.tile(cos, (1, N_Q_HEADS)), jnp.tile(sin, (1, N_Q_HEADS))


def _selectors():
    lane = np.arange(N_EXPERTS)
    member = np.stack([(lane[:, None] == (lane[None, :] // EXPERTS_PER_GROUP) * EXPERTS_PER_GROUP + k)
                       for k in range(EXPERTS_PER_GROUP)]).astype(np.float32)
    group = np.stack([np.broadcast_to(lane[:, None] == m * EXPERTS_PER_GROUP, (N_EXPERTS, N_EXPERTS))
                      for m in range(N_GROUPS)]).astype(np.float32)
    head = (np.arange(ATTN_WIDTH)[:, None] // HEAD_DIM == np.arange(ATTN_WIDTH)[None, :] // HEAD_DIM)
    return jnp.asarray(member), jnp.asarray(group), jnp.asarray(head.astype(np.float32), dtype=BF16)


def kernel(x_prompt, x_sample, cache_k, cache_v, state_fwd, state_bwd, c, c_ctx, norm1_w, norm2_w, w_mod, b_mod, w_in, q_norm_w, k_norm_w, hgrn_lb_fwd, hgrn_lb_bwd, hgrn_norm_w, w_br_attn, w_br_hgrn, w_out, w_router, b_router, w_exp_gate, w_exp_up, w_exp_down):
    cos_t, sin_t = _rope_tables()
    sel_member, sel_group, head_ones = _selectors()
    sums_f, masks_f = _hgrn_tables(False)
    sums_b, masks_b = _hgrn_tables(True)
    hgrn_tables = (sums_f, sums_b, masks_f, masks_b)
    cond = jnp.concatenate([c_ctx[None, :], c, jnp.zeros((N_COND - 1 - DEC_BATCH, D_MODEL), F32)], axis=0)
    mod = _modulation(cond, w_mod, b_mod)

    w_in_b = w_in.astype(BF16)
    w_ba_b = w_br_attn.astype(BF16)
    w_bh_b = w_br_hgrn.astype(BF16)
    w_out_b = w_out.astype(BF16)
    by_group = lambda w: w.astype(BF16).reshape(DEPTH, N_GROUPS, EXPERTS_PER_GROUP, *w.shape[2:])
    w_eg_b, w_eu_b, w_ed_b = by_group(w_exp_gate), by_group(w_exp_up), by_group(w_exp_down)
    lane_pad = jnp.eye(N_EXPERTS, LANES, dtype=F32)
    b_router_r = b_router.reshape(1, N_EXPERTS)
    norm1_r = norm1_w.reshape(DEPTH, 1, D_MODEL)
    norm2_r = norm2_w.reshape(DEPTH, 1, D_MODEL)
    hnorm_r = hgrn_norm_w.reshape(DEPTH, 1, HGRN_V)
    qn_r = jnp.tile(q_norm_w, (1, N_Q_HEADS)).reshape(DEPTH, 1, ATTN_WIDTH)
    kn_r = jnp.tile(k_norm_w, (1, N_KV_HEADS)).reshape(DEPTH, 1, KV_WIDTH)

    x = jnp.concatenate([x_prompt.reshape(N_CTX, D_MODEL), x_sample.reshape(N_LAT, D_MODEL)], axis=0)
    ks_out, vs_out, sf_out, sb_out = [], [], [], []
    for l in range(DEPTH):
        q, k, v, h5, gates = _input_projection(l, x, mod, norm1_r, w_in_b, cos_t, sin_t, qn_r, kn_r,
                                               head_ones)
        attn = (_attention_ctx(q, k, v), _attention_lat(l, q, k, v, cache_k, cache_v))
        rec_c, sf, sb = _hgrn(l, h5, 0, BATCH, SEQ, hgrn_lb_fwd, hgrn_lb_bwd, hnorm_r, hgrn_tables)
        rec_s = _hgrn(l, h5, N_CTX, DEC_BATCH, DEC_SEQ, hgrn_lb_fwd, hgrn_lb_bwd, hnorm_r,
                      hgrn_tables, init=(state_fwd, state_bwd))
        rec = (rec_c, rec_s)
        x1, slab, route = _merge(l, attn, rec, gates, x, mod, norm2_r, w_ba_b, w_bh_b, w_out_b,
                                 w_router, b_router_r, sel_member, sel_group, lane_pad)
        dst, tile_group, n_used = _plan(route)
        expert_out = _experts(l, tile_group, n_used, _invert(dst), slab, w_eg_b, w_eu_b, w_ed_b)
        x = _combine(l, dst, expert_out, x1, mod)
        ks_out.append(k[:N_CTX].reshape(BATCH, SEQ, N_KV_HEADS, HEAD_DIM).transpose(0, 2, 1, 3))
        vs_out.append(v[:N_CTX].reshape(BATCH, SEQ, N_KV_HEADS, HEAD_DIM).transpose(0, 2, 1, 3))
        sf_out.append(sf)
        sb_out.append(sb)

    y_prompt = x[:N_CTX].reshape(BATCH, SEQ, D_MODEL)
    y_sample = x[N_CTX:].reshape(DEC_BATCH, DEC_SEQ, D_MODEL)
    return (y_prompt, y_sample, jnp.stack(ks_out, axis=1), jnp.stack(vs_out, axis=1),
            jnp.stack(sf_out, axis=1), jnp.stack(sb_out, axis=1))
```

```python
import functools

import numpy as np
import jax
import jax.numpy as jnp
from jax import lax
from jax.experimental import pallas as pl
from jax.experimental.pallas import tpu as pltpu

F32 = jnp.float32
BF16 = jnp.bfloat16

D_MODEL = 1024
BATCH = 16
SEQ = 256
DEPTH = 4
DEC_BATCH = 4
DEC_SEQ = 2048
PAST_LEN = 512
GRID_W = 64
N_Q_HEADS = 8
N_KV_HEADS = 2
GQA_GROUP = N_Q_HEADS // N_KV_HEADS
HEAD_DIM = 64
AXIS_DIM = HEAD_DIM // 2
ATTN_WIDTH = N_Q_HEADS * HEAD_DIM
KV_WIDTH = N_KV_HEADS * HEAD_DIM
ROPE_THETA = 10000.0
HGRN_HEADS = 4
HGRN_K = 128
HGRN_V = 128
HGRN_WIDTH = HGRN_HEADS * HGRN_K
N_EXPERTS = 16
N_GROUPS = 4
EXPERTS_PER_GROUP = N_EXPERTS // N_GROUPS
D_FF = 512
EPS = 1e-6
IN_COLS = ATTN_WIDTH + 2 * KV_WIDTH + 5 * HGRN_WIDTH + 2 * D_MODEL

N_CTX = BATCH * SEQ
N_LAT = DEC_BATCH * DEC_SEQ
N_TOK = N_CTX + N_LAT
N_COND = 8
N_MOD = 6

C_Q = 0
C_K = C_Q + ATTN_WIDTH
C_V = C_K + KV_WIDTH
C_HQ = C_V + KV_WIDTH
C_FF = C_HQ + HGRN_WIDTH
C_FB = C_FF + HGRN_WIDTH
C_HI = C_FB + HGRN_WIDTH
C_HG = C_HI + HGRN_WIDTH
C_GA = C_HG + HGRN_WIDTH
C_GH = C_GA + D_MODEL

LANES = 128
SUBLANES = 8
FEAT_ROWS = D_MODEL // LANES
SLAB_ROWS = 2 * FEAT_ROWS

TOK_TILE = 512
PLAN_TILE = 1024
SORT_TILE = 512
N_SORT_TILES = N_TOK // SORT_TILE + N_GROUPS
Q_TILE = 512
CHUNK = 64
HGRN_LEVELS = (32, 16, 8, 4, 2, 1)
BLOCK_CHUNKS = 8
LOG2E = 1.4426950408889634
VMEM_LIMIT = 56 * 1024 * 1024

_HI = lax.Precision.HIGHEST


def _sigmoid(x):
    return 1.0 / (1.0 + jnp.exp(-x))


def _dot(a, b):
    return jnp.dot(a, b, preferred_element_type=F32)


def _dot_nt(a, b, precision=None):
    return lax.dot_general(a, b, (((1,), (1,)), ((), ())), precision=precision,
                           preferred_element_type=F32)


def _dot_tn(a, b, precision=None):
    return lax.dot_general(a, b, (((0,), (0,)), ((), ())), precision=precision,
                           preferred_element_type=F32)


def _dot_exact(a, b):
    return jnp.dot(a, b, precision=_HI, preferred_element_type=F32)


def _split_bf16(a):
    hi = a.astype(BF16)
    return hi, (a - hi.astype(F32)).astype(BF16)


def _params(*sem):
    return pltpu.CompilerParams(dimension_semantics=sem, vmem_limit_bytes=VMEM_LIMIT)


def _mod_kernel(cond_ref, w_ref, b_ref, out_ref):
    c = cond_ref[...]
    a_hi, a_lo = _split_bf16(c * _sigmoid(c))
    w_hi, w_lo = _split_bf16(w_ref[...])
    p = _dot(jnp.concatenate([a_hi, a_lo], axis=0), w_hi)
    out_ref[...] = p[:N_COND] + p[N_COND:] + _dot(a_hi, w_lo) + b_ref[...]


def _modulation(cond, w_mod, b_mod):
    out = pl.pallas_call(
        _mod_kernel,
        grid=(DEPTH, N_MOD),
        in_specs=[
            pl.BlockSpec((N_COND, D_MODEL), lambda l, j: (0, 0)),
            pl.BlockSpec((None, D_MODEL, D_MODEL), lambda l, j: (l, 0, j)),
            pl.BlockSpec((None, 1, D_MODEL), lambda l, j: (l, 0, j)),
        ],
        out_specs=pl.BlockSpec((None, None, N_COND, D_MODEL), lambda l, j: (l, j, 0, 0)),
        out_shape=jax.ShapeDtypeStruct((DEPTH, N_MOD, N_COND, D_MODEL), F32),
        compiler_params=_params("arbitrary", "arbitrary"),
        name="modulation",
    )(cond, w_mod, b_mod.reshape(DEPTH, 1, N_MOD * D_MODEL))
    return out.transpose(0, 2, 1, 3)


def _head_mean_sq(a, bd):
    hi, lo = _split_bf16(a * a)
    return (_dot(hi, bd) + _dot(lo, bd)) * (1.0 / HEAD_DIM)


def _rope(x, cos, sin_signed):
    width = x.shape[-1]
    lane = lax.broadcasted_iota(jnp.int32, x.shape, 1)
    first = (lane & (AXIS_DIM - 1)) < (AXIS_DIM // 2)
    rot = jnp.where(first, pltpu.roll(x, width - AXIS_DIM // 2, 1), pltpu.roll(x, AXIS_DIM // 2, 1))
    return x * cos + rot * sin_signed


def _inproj_kernel(x_ref, mod_ref, n1_ref, w_ref, cos_ref, sin_ref, qn_ref, kn_ref, bd_ref,
                   q_ref, k_ref, v_ref, h_ref, g_ref):
    x = x_ref[...]
    xn = x * lax.rsqrt(jnp.mean(x * x, axis=-1, keepdims=True) + EPS) * n1_ref[...]
    xb = (xn * (1.0 + mod_ref[1:2, :]) + mod_ref[0:1, :]).astype(BF16)

    def proj(c0, width):
        return _dot(xb, w_ref[:, c0:c0 + width])

    a = proj(C_Q, ATTN_WIDTH)
    qn = a * lax.rsqrt(_head_mean_sq(a, bd_ref[...]) + EPS) * qn_ref[...]
    q_ref[...] = (_rope(qn, cos_ref[...], sin_ref[...]) * (HEAD_DIM ** -0.5)).astype(BF16)

    a = proj(C_K, KV_WIDTH)
    kn = a * lax.rsqrt(_head_mean_sq(a, bd_ref[:KV_WIDTH, :KV_WIDTH]) + EPS) * kn_ref[...]
    k_ref[...] = _rope(kn, cos_ref[:, :KV_WIDTH], sin_ref[:, :KV_WIDTH])
    v_ref[...] = proj(C_V, KV_WIDTH)

    a = proj(C_HQ, HGRN_WIDTH)
    h_ref[:, 0:HGRN_WIDTH] = a * _sigmoid(a) * (HGRN_K ** -0.5)
    h_ref[:, HGRN_WIDTH:2 * HGRN_WIDTH] = proj(C_FF, HGRN_WIDTH)
    h_ref[:, 2 * HGRN_WIDTH:3 * HGRN_WIDTH] = proj(C_FB, HGRN_WIDTH)
    h_ref[:, 3 * HGRN_WIDTH:4 * HGRN_WIDTH] = proj(C_HI, HGRN_WIDTH)
    a = proj(C_HG, HGRN_WIDTH)
    h_ref[:, 4 * HGRN_WIDTH:5 * HGRN_WIDTH] = a * _sigmoid(a)

    g_ref[:, 0:D_MODEL] = _sigmoid(proj(C_GA, D_MODEL))
    g_ref[:, D_MODEL:2 * D_MODEL] = _sigmoid(proj(C_GH, D_MODEL))


_CTX_TILES = N_CTX // TOK_TILE
_LAT_TILES_PER_SEQ = DEC_SEQ // TOK_TILE


def _tile_cond(i):
    return jnp.where(i < _CTX_TILES, 0, 1 + (i - _CTX_TILES) // _LAT_TILES_PER_SEQ)


def _tile_rope_block(i):
    return jnp.where(i < _CTX_TILES, _LAT_TILES_PER_SEQ, (i - _CTX_TILES) % _LAT_TILES_PER_SEQ)


def _input_projection(l, x, mod, norm1_w, w_in, cos_t, sin_t, qn_w, kn_w, bd):
    row = lambda w: pl.BlockSpec((TOK_TILE, w), lambda i: (i, 0))
    return pl.pallas_call(
        _inproj_kernel,
        grid=(N_TOK // TOK_TILE,),
        in_specs=[
            row(D_MODEL),
            pl.BlockSpec((None, None, N_MOD, D_MODEL), lambda i: (l, _tile_cond(i), 0, 0)),
            pl.BlockSpec((None, 1, D_MODEL), lambda i: (l, 0, 0)),
            pl.BlockSpec((None, D_MODEL, IN_COLS), lambda i: (l, 0, 0),
                         pipeline_mode=pl.Buffered(1)),
            pl.BlockSpec((TOK_TILE, ATTN_WIDTH), lambda i: (_tile_rope_block(i), 0)),
            pl.BlockSpec((TOK_TILE, ATTN_WIDTH), lambda i: (_tile_rope_block(i), 0)),
            pl.BlockSpec((None, 1, ATTN_WIDTH), lambda i: (l, 0, 0)),
            pl.BlockSpec((None, 1, KV_WIDTH), lambda i: (l, 0, 0)),
            pl.BlockSpec((ATTN_WIDTH, ATTN_WIDTH), lambda i: (0, 0)),
        ],
        out_specs=[row(ATTN_WIDTH), row(KV_WIDTH), row(KV_WIDTH), row(5 * HGRN_WIDTH),
                   row(2 * D_MODEL)],
        out_shape=[
            jax.ShapeDtypeStruct((N_TOK, ATTN_WIDTH), BF16),
            jax.ShapeDtypeStruct((N_TOK, KV_WIDTH), F32),
            jax.ShapeDtypeStruct((N_TOK, KV_WIDTH), F32),
            jax.ShapeDtypeStruct((N_TOK, 5 * HGRN_WIDTH), F32),
            jax.ShapeDtypeStruct((N_TOK, 2 * D_MODEL), F32),
        ],
        compiler_params=_params("arbitrary"),
        name="input_projection",
    )(x, mod, norm1_w, w_in, cos_t, sin_t, qn_w, kn_w, bd)


def _softmax_pv(scores, values):
    m = scores[0].max(axis=-1, keepdims=True)
    for s in scores[1:]:
        m = jnp.maximum(m, s.max(axis=-1, keepdims=True))
    num, den = None, None
    for s, v in zip(scores, values):
        p = jnp.exp(s - m)
        d = p.sum(axis=-1, keepdims=True)
        o = _dot(p.astype(BF16), v)
        num = o if num is None else num + o
        den = d if den is None else den + d
    return num / den


def _attn_ctx_kernel(q_ref, k_ref, v_ref, o_ref):
    for g in range(N_KV_HEADS):
        cols = slice(g * HEAD_DIM, (g + 1) * HEAD_DIM)
        kg = k_ref[:, cols].astype(BF16)
        vg = v_ref[:, cols].astype(BF16)
        for hh in range(GQA_GROUP):
            h = g * GQA_GROUP + hh
            hc = slice(h * HEAD_DIM, (h + 1) * HEAD_DIM)
            o = _softmax_pv([_dot_nt(q_ref[:, hc], kg)], [vg])
            o_ref[:, hc] = o.astype(BF16)


def _attention_ctx(q, k, v):
    return pl.pallas_call(
        _attn_ctx_kernel,
        grid=(BATCH,),
        in_specs=[
            pl.BlockSpec((SEQ, ATTN_WIDTH), lambda b: (b, 0)),
            pl.BlockSpec((SEQ, KV_WIDTH), lambda b: (b, 0)),
            pl.BlockSpec((SEQ, KV_WIDTH), lambda b: (b, 0)),
        ],
        out_specs=pl.BlockSpec((SEQ, ATTN_WIDTH), lambda b: (b, 0)),
        out_shape=jax.ShapeDtypeStruct((N_CTX, ATTN_WIDTH), BF16),
        compiler_params=_params("arbitrary"),
        name="attention_ctx",
    )(q, k, v)


def _attn_lat_kernel(q_ref, k_ref, v_ref, ck_ref, cv_ref, o_ref):
    for g in range(N_KV_HEADS):
        cols = slice(g * HEAD_DIM, (g + 1) * HEAD_DIM)
        kg = k_ref[:, cols].astype(BF16)
        vg = v_ref[:, cols].astype(BF16)
        ckg = ck_ref[g].astype(BF16)
        cvg = cv_ref[g].astype(BF16)
        for hh in range(GQA_GROUP):
            h = g * GQA_GROUP + hh
            hc = slice(h * HEAD_DIM, (h + 1) * HEAD_DIM)
            qh = q_ref[:, hc]
            o = _softmax_pv([_dot_nt(qh, kg), _dot_nt(qh, ckg)], [vg, cvg])
            o_ref[:, hc] = o.astype(BF16)


def _attention_lat(l, q, k, v, cache_k, cache_v):
    q_blocks = DEC_SEQ // Q_TILE
    q_off = N_CTX // Q_TILE
    kv_off = N_CTX // DEC_SEQ
    cache_spec = pl.BlockSpec((None, None, N_KV_HEADS, PAST_LEN, HEAD_DIM),
                              lambda b, i: (b, l, 0, 0, 0))
    return pl.pallas_call(
        _attn_lat_kernel,
        grid=(DEC_BATCH, q_blocks),
        in_specs=[
            pl.BlockSpec((Q_TILE, ATTN_WIDTH), lambda b, i: (q_off + b * q_blocks + i, 0)),
            pl.BlockSpec((DEC_SEQ, KV_WIDTH), lambda b, i: (kv_off + b, 0)),
            pl.BlockSpec((DEC_SEQ, KV_WIDTH), lambda b, i: (kv_off + b, 0)),
            cache_spec,
            cache_spec,
        ],
        out_specs=pl.BlockSpec((Q_TILE, ATTN_WIDTH), lambda b, i: (b * q_blocks + i, 0)),
        out_shape=jax.ShapeDtypeStruct((N_LAT, ATTN_WIDTH), BF16),
        compiler_params=_params("arbitrary", "arbitrary"),
        name="attention_lat",
    )(q, k, v, cache_k, cache_v)


def _lower_bound(lb_ref, l):
    z = lb_ref[...]
    e = jnp.exp(z - z.max(axis=0, keepdims=True))
    p = e / e.sum(axis=0, keepdims=True)
    acc = p[0:1] * 0.0
    for r in range(1, l + 1):
        acc = acc + p[r:r + 1]
    return acc


def _hgrn_tables(rev):
    t = np.arange(CHUNK)
    tt, uu = t[:, None], t[None, :]
    groups = []
    masks = []
    for h in HGRN_LEVELS:
        right = (t & h) != 0
        edge = ((t // (2 * h)) * 2 * h + h - 1)[:, None]
        groups.append(np.where(right[:, None], (uu > edge) & (uu <= tt), (uu > tt) & (uu <= edge)))
        same = (tt // (2 * h)) == (uu // (2 * h))
        masks.append(same & right[:, None] & ~right[None, :])
    masks.append(tt == uu)
    groups += [uu <= tt, uu > tt]
    sums = np.stack(groups).astype(np.float32)
    masks = np.stack(masks).astype(np.float32)
    if rev:
        sums = sums[:, ::-1, ::-1]
        masks = masks[:, ::-1, ::-1]
    sums = sums.reshape(-1, CHUNK)
    return (jnp.asarray(np.concatenate([sums, sums], axis=1), dtype=BF16),
            jnp.asarray(np.ascontiguousarray(masks)))


def _hgrn_block(q, x, v, lb, sums_ref, masks_ref, st_ref, rev):
    n = q.shape[0] // CHUNK
    f = lb + (1.0 - lb) * _sigmoid(x)
    kk = 1.0 - f
    g = jnp.log(f) * LOG2E
    g_hi, g_lo = _split_bf16(g)
    vb = v.astype(BF16)
    rid = lax.broadcasted_iota(jnp.int32, (CHUNK, HGRN_K), 0)
    edge = 0 if rev else CHUNK - 1
    n_lev = len(HGRN_LEVELS)
    outs = [None] * n
    for c in (reversed(range(n)) if rev else range(n)):
        rs = slice(c * CHUNK, (c + 1) * CHUNK)
        e = jnp.exp2(_dot(sums_ref[...], jnp.concatenate([g_hi[rs], g_lo[rs]], axis=0)))
        q_c, k_c, v_c = q[rs], kk[rs], vb[rs]
        a = _dot_nt(q_c.astype(BF16), k_c.astype(BF16)) * masks_ref[n_lev]
        for i, h in enumerate(HGRN_LEVELS):
            later = ((rid & h) == 0) if rev else ((rid & h) != 0)
            z = (jnp.where(later, q_c, k_c) * e[i * CHUNK:(i + 1) * CHUNK]).astype(BF16)
            a = a + _dot_nt(z, z) * masks_ref[i]
        e_in = e[n_lev * CHUNK:(n_lev + 1) * CHUNK]
        e_out = e[(n_lev + 1) * CHUNK:(n_lev + 2) * CHUNK]
        st = st_ref[...]
        outs[c] = (_dot(a.astype(BF16), v_c)
                   + _dot_nt((q_c * e_in).astype(BF16), st.astype(BF16)))
        k_dec = (k_c * e_out).astype(BF16)
        st_ref[...] = st * e_in[edge:edge + 1] + _dot_tn(v_c, k_dec)
    return jnp.concatenate(outs, axis=0)


def _hgrn_kernel(l, seq_len, has_init, *refs):
    (hq_ref, ff_ref, fb_ref, hi_ref, hg_ref, lbf_ref, lbb_ref, hn_ref,
     sumf_ref, sumb_ref, mskf_ref, mskb_ref) = refs[:12]
    refs = refs[12:]
    if has_init:
        s0f_ref, s0b_ref, rec_ref, of_ref, ob_ref, stf_ref, stb_ref = refs
    else:
        rec_ref, sf_ref, sb_ref, of_ref, ob_ref, stf_ref, stb_ref = refs
    lb_f = _lower_bound(lbf_ref, l)
    lb_b = _lower_bound(lbb_ref, l)
    if has_init:
        stf_ref[...] = s0f_ref[...].T
        stb_ref[...] = s0b_ref[...].T
    else:
        stf_ref[...] = jnp.zeros((HGRN_V, HGRN_K), F32)
        stb_ref[...] = jnp.zeros((HGRN_V, HGRN_K), F32)
    rows = min(BLOCK_CHUNKS * CHUNK, seq_len)
    n_blocks = seq_len // rows

    def step(rf, rb):
        of_ref[rf, :] = _hgrn_block(hq_ref[rf, :], ff_ref[rf, :], hi_ref[rf, :], lb_f,
                                    sumf_ref, mskf_ref, stf_ref, False)
        ob_ref[rb, :] = _hgrn_block(hq_ref[rb, :], fb_ref[rb, :], hi_ref[rb, :], lb_b,
                                    sumb_ref, mskb_ref, stb_ref, True)

    if n_blocks == 1:
        step(pl.ds(0, rows), pl.ds(0, rows))
    else:
        def body(i, carry):
            step(pl.ds(pl.multiple_of(i * rows, rows), rows),
                 pl.ds(pl.multiple_of((n_blocks - 1 - i) * rows, rows), rows))
            return carry

        lax.fori_loop(0, n_blocks, body, 0)
    o = of_ref[...] + ob_ref[...]
    o = o * lax.rsqrt(jnp.mean(o * o, axis=-1, keepdims=True) + EPS) * hn_ref[...]
    rec_ref[...] = o * hg_ref[...]
    if not has_init:
        sf_ref[...] = stf_ref[...].T
        sb_ref[...] = stb_ref[...].T


def _hgrn(l, h5, row_off, n_seq, seq_len, lb_fwd, lb_bwd, hnorm_w, tables, init=None):
    blocks_off = row_off // seq_len
    part = lambda k: pl.BlockSpec((seq_len, HGRN_K),
                                  lambda b, h: (blocks_off + b, k * HGRN_HEADS + h))
    lb_spec = pl.BlockSpec((DEPTH, HGRN_K), lambda b, h: (0, h))
    sum_spec = pl.BlockSpec((8 * CHUNK, 2 * CHUNK), lambda b, h: (0, 0))
    msk_spec = pl.BlockSpec((len(HGRN_LEVELS) + 1, CHUNK, CHUNK), lambda b, h: (0, 0, 0))
    in_specs = [part(0), part(1), part(2), part(3), part(4), lb_spec, lb_spec,
                pl.BlockSpec((None, 1, HGRN_V), lambda b, h: (l, 0, 0)),
                sum_spec, sum_spec, msk_spec, msk_spec]
    args = [h5, h5, h5, h5, h5, lb_fwd, lb_bwd, hnorm_w, *tables]
    rec_spec = pl.BlockSpec((seq_len, HGRN_V), lambda b, h: (b, h))
    rec_shape = jax.ShapeDtypeStruct((n_seq * seq_len, HGRN_WIDTH), F32)
    if init is not None:
        st_spec = pl.BlockSpec((None, None, None, HGRN_K, HGRN_V), lambda b, h: (b, l, h, 0, 0))
        in_specs += [st_spec, st_spec]
        args += list(init)
        out_specs, out_shape = rec_spec, rec_shape
    else:
        st_spec = pl.BlockSpec((None, None, HGRN_K, HGRN_V), lambda b, h: (b, h, 0, 0))
        st_shape = jax.ShapeDtypeStruct((n_seq, HGRN_HEADS, HGRN_K, HGRN_V), F32)
        out_specs, out_shape = [rec_spec, st_spec, st_spec], [rec_shape, st_shape, st_shape]
    return pl.pallas_call(
        functools.partial(_hgrn_kernel, l, seq_len, init is not None),
        grid=(n_seq, HGRN_HEADS),
        in_specs=in_specs,
        out_specs=out_specs,
        out_shape=out_shape,
        scratch_shapes=[pltpu.VMEM((seq_len, HGRN_V), F32), pltpu.VMEM((seq_len, HGRN_V), F32),
                        pltpu.VMEM((HGRN_V, HGRN_K), F32), pltpu.VMEM((HGRN_V, HGRN_K), F32)],
        compiler_params=_params("arbitrary", "arbitrary"),
        name="hgrn_lat" if init is not None else "hgrn_ctx",
    )(*args)


def _route(scores, bias, sel_member, sel_group):
    sel = scores + bias
    members = [_dot_exact(sel, sel_member[k]) for k in range(EXPERTS_PER_GROUP)]
    pair = None
    for i in range(EXPERTS_PER_GROUP):
        for j in range(i + 1, EXPERTS_PER_GROUP):
            s = members[i] + members[j]
            pair = s if pair is None else jnp.maximum(pair, s)
    lane = lax.broadcasted_iota(jnp.int32, sel.shape, 1)
    best = _dot_exact(pair, sel_group[0])
    best_g = jnp.zeros(sel.shape, jnp.int32)
    for m in range(1, N_GROUPS):
        cand = _dot_exact(pair, sel_group[m])
        upd = cand > best
        best_g = jnp.where(upd, m, best_g)
        best = jnp.where(upd, cand, best)
    in_group = (lane >> 2) == best_g
    pos = lane & (EXPERTS_PER_GROUP - 1)
    rank = jnp.zeros(sel.shape, jnp.int32)
    for k in range(EXPERTS_PER_GROUP):
        ahead = (members[k] > sel) | ((members[k] == sel) & (k < pos))
        rank = rank + jnp.where(ahead, 1, 0)
    w = jnp.where(in_group & (rank < 2), scores, 0.0)
    return w / w.sum(axis=-1, keepdims=True)


def _merge_kernel(attn_c_ref, attn_s_ref, rec_c_ref, rec_s_ref, g_ref, x_ref, mod_ref, n2_ref,
                  wa_ref, wh_ref, wo_ref, wr_ref, br_ref, selm_ref, selg_ref, pad_ref,
                  x1_ref, slab_ref, route_ref):
    is_ctx = pl.program_id(0) < _CTX_TILES
    attn = jnp.where(is_ctx, attn_c_ref[...], attn_s_ref[...])
    rec = jnp.where(is_ctx, rec_c_ref[...], rec_s_ref[...])
    ya = _dot(attn, wa_ref[...])
    yh = _dot(rec.astype(BF16), wh_ref[...])
    merged = g_ref[:, 0:D_MODEL] * ya + g_ref[:, D_MODEL:2 * D_MODEL] * yh
    out = _dot(merged.astype(BF16), wo_ref[...])
    x1 = x_ref[...] + mod_ref[2:3, :] * out
    x1_ref[...] = x1
    xn = x1 * lax.rsqrt(jnp.mean(x1 * x1, axis=-1, keepdims=True) + EPS) * n2_ref[...]
    h2 = xn * (1.0 + mod_ref[4:5, :]) + mod_ref[3:4, :]
    h_hi, h_lo = _split_bf16(h2)
    w_hi, w_lo = _split_bf16(wr_ref[...])
    scores = _sigmoid(_dot(jnp.concatenate([h_hi, h_lo, h_hi], axis=1),
                           jnp.concatenate([w_hi, w_hi, w_lo], axis=0)))
    gates = _route(scores, br_ref[...], [selm_ref[k] for k in range(EXPERTS_PER_GROUP)],
                   [selg_ref[m] for m in range(N_GROUPS)])
    gates = _dot_exact(gates, pad_ref[...])
    route_ref[...] = gates
    for s in range(FEAT_ROWS):
        slab_ref[:, s, :] = h2[:, s * LANES:(s + 1) * LANES]
    slab_ref[:, FEAT_ROWS, :] = gates
    for s in range(FEAT_ROWS + 1, SLAB_ROWS):
        slab_ref[:, s, :] = jnp.zeros_like(gates)


def _merge(l, attn, rec, gates, x, mod, norm2_w, w_br_attn, w_br_hgrn, w_out, w_router, b_router,
           sel_member, sel_group, lane_pad):
    row = lambda w: pl.BlockSpec((TOK_TILE, w), lambda i: (i, 0))
    ctx_row = lambda w: pl.BlockSpec((TOK_TILE, w), lambda i: (jnp.minimum(i, _CTX_TILES - 1), 0))
    lat_row = lambda w: pl.BlockSpec((TOK_TILE, w), lambda i: (jnp.maximum(i - _CTX_TILES, 0), 0))
    layer = lambda r, c: pl.BlockSpec((None, r, c), lambda i: (l, 0, 0))
    full = lambda *s: pl.BlockSpec(s, lambda i: (0,) * len(s))
    return pl.pallas_call(
        _merge_kernel,
        grid=(N_TOK // TOK_TILE,),
        in_specs=[
            ctx_row(ATTN_WIDTH), lat_row(ATTN_WIDTH), ctx_row(HGRN_WIDTH), lat_row(HGRN_WIDTH),
            row(2 * D_MODEL), row(D_MODEL),
            pl.BlockSpec((None, None, N_MOD, D_MODEL), lambda i: (l, _tile_cond(i), 0, 0)),
            layer(1, D_MODEL),
            layer(ATTN_WIDTH, D_MODEL), layer(HGRN_WIDTH, D_MODEL), layer(D_MODEL, D_MODEL),
            full(D_MODEL, N_EXPERTS), full(1, N_EXPERTS),
            full(EXPERTS_PER_GROUP, N_EXPERTS, N_EXPERTS), full(N_GROUPS, N_EXPERTS, N_EXPERTS),
            full(N_EXPERTS, LANES),
        ],
        out_specs=[row(D_MODEL),
                   pl.BlockSpec((TOK_TILE, SLAB_ROWS, LANES), lambda i: (i, 0, 0)),
                   row(LANES)],
        out_shape=[
            jax.ShapeDtypeStruct((N_TOK, D_MODEL), F32),
            jax.ShapeDtypeStruct((N_TOK, SLAB_ROWS, LANES), F32),
            jax.ShapeDtypeStruct((N_TOK, LANES), F32),
        ],
        compiler_params=_params("arbitrary"),
        name="merge_router",
    )(*attn, *rec, gates, x, mod, norm2_w, w_br_attn, w_br_hgrn, w_out, w_router, b_router,
      sel_member, sel_group, lane_pad)


def _plan_kernel(route_ref, dst_ref, meta_ref, tot_ref, run_ref):
    p, i = pl.program_id(0), pl.program_id(1)

    @pl.when(jnp.logical_and(p == 0, i == 0))
    def _():
        tot_ref[...] = jnp.zeros_like(tot_ref)

    @pl.when(i == 0)
    def _():
        run_ref[...] = jnp.zeros_like(run_ref)

    grp = lax.broadcasted_iota(jnp.int32, (SUBLANES, LANES), 0)
    lane = lax.broadcasted_iota(jnp.int32, (SUBLANES, LANES), 1)
    member = jnp.where(((lane >> 2) == grp) & (lane < N_EXPERTS), 1.0, 0.0)
    one_hot = jnp.where(_dot_nt(member, route_ref[...], precision=_HI) > 0.0, 1.0, 0.0)
    src = lax.broadcasted_iota(jnp.int32, (PLAN_TILE, PLAN_TILE), 0)
    tgt = lax.broadcasted_iota(jnp.int32, (PLAN_TILE, PLAN_TILE), 1)
    before = jnp.where(src < tgt, 1.0, 0.0).astype(BF16)
    rank = _dot(one_hot.astype(BF16), before) + run_ref[:, 0:1]
    tile_tot = one_hot.sum(axis=1, keepdims=True)

    @pl.when(p == 0)
    def _():
        tot_ref[...] += tile_tot

    @pl.when(p == 1)
    def _():
        run_ref[...] += tile_tot
        tot = tot_ref[:, 0:1]
        n_tiles = jnp.floor((tot + (SORT_TILE - 1)) * (1.0 / SORT_TILE))
        offs, acc = [], jnp.zeros((1, 1), F32)
        for m in range(N_GROUPS):
            offs.append(acc * SORT_TILE)
            acc = acc + n_tiles[m:m + 1]
        dst = sum(one_hot[m:m + 1] * (offs[m] + rank[m:m + 1]) for m in range(N_GROUPS))
        for r in range(PLAN_TILE // LANES):
            dst_ref[r:r + 1, :] = dst[:, r * LANES:(r + 1) * LANES].astype(jnp.int32)
        tile_row = (lax.broadcasted_iota(jnp.int32, (1, LANES), 1) * SORT_TILE).astype(F32)
        tile_group = sum(jnp.where(tile_row >= offs[m], 1.0, 0.0) for m in range(1, N_GROUPS))
        rows = [tile_group, jnp.broadcast_to(acc, (1, LANES))]
        rows += [jnp.zeros((1, LANES), F32)] * (SUBLANES - len(rows))
        meta_ref[...] = jnp.concatenate(rows, axis=0).astype(jnp.int32)


def _plan(route):
    n_blocks = N_TOK // PLAN_TILE
    dst, meta = pl.pallas_call(
        _plan_kernel,
        grid=(2, n_blocks),
        in_specs=[pl.BlockSpec((PLAN_TILE, LANES), lambda p, i: (i, 0))],
        out_specs=[pl.BlockSpec((PLAN_TILE // LANES, LANES), lambda p, i: (i * p, 0)),
                   pl.BlockSpec((SUBLANES, LANES), lambda p, i: (0, 0))],
        out_shape=[jax.ShapeDtypeStruct((N_TOK // LANES, LANES), jnp.int32),
                   jax.ShapeDtypeStruct((SUBLANES, LANES), jnp.int32)],
        scratch_shapes=[pltpu.VMEM((SUBLANES, LANES), F32), pltpu.VMEM((SUBLANES, LANES), F32)],
        compiler_params=_params("arbitrary", "arbitrary"),
        name="route_plan",
    )(route)
    return dst, meta[0, :N_SORT_TILES], meta[1, 0:1]


def _start_rows(make_copy):
    def start(i, carry):
        make_copy(2 * i).start(priority=0)
        make_copy(2 * i + 1).start(priority=1)
        return carry

    lax.fori_loop(0, LANES // 2, start, 0, unroll=4)


def _invert_kernel(dst_ref, zero_ref, src_ref, sem):
    i = pl.program_id(0)

    @pl.when(i == 0)
    def _():
        clear = pltpu.make_async_copy(zero_ref, src_ref, sem)
        clear.start()
        clear.wait()

    first = i * PLAN_TILE

    def put(t, carry):
        src_ref[dst_ref[t]] = first + t
        return carry

    lax.fori_loop(0, PLAN_TILE, put, 0, unroll=8)


def _invert(dst):
    n_rows = N_SORT_TILES * SORT_TILE
    src = pl.pallas_call(
        _invert_kernel,
        grid=(N_TOK // PLAN_TILE,),
        in_specs=[pl.BlockSpec((PLAN_TILE,), lambda i: (i,), memory_space=pltpu.SMEM),
                  pl.BlockSpec(memory_space=pl.ANY)],
        out_specs=pl.BlockSpec((n_rows,), lambda i: (0,), memory_space=pltpu.SMEM),
        out_shape=jax.ShapeDtypeStruct((n_rows,), jnp.int32),
        scratch_shapes=[pltpu.SemaphoreType.DMA(())],
        compiler_params=_params("arbitrary"),
        name="invert_plan",
    )(dst.reshape(N_TOK), jnp.zeros((n_rows,), jnp.int32))
    return src.reshape(n_rows // LANES, LANES)


_SRC_BLOCK_TILES = SUBLANES * LANES // SORT_TILE


def _experts_kernel(tg_ref, nv_ref, src_ref, nxt_ref, slab_ref, wg_ref, wu_ref, wd_ref, y_ref,
                    x_ref, sem):
    j = pl.program_id(0)
    n_used = nv_ref[0]
    groups = SORT_TILE // LANES

    def gather(idx_ref, tile, slot, unrolled):
        first_row = (tile % _SRC_BLOCK_TILES) * groups
        for r in range(groups):
            make = lambda c, r=r: pltpu.make_async_copy(
                slab_ref.at[idx_ref[first_row + r, c]], x_ref.at[slot, r * LANES + c], sem.at[slot])
            if unrolled:
                for c in range(LANES):
                    make(c).start(priority=c % 2)
            else:
                _start_rows(make)

    def wait_rows(slot):
        for r in range(groups):
            pltpu.make_async_copy(slab_ref.at[pl.ds(0, LANES)],
                                  x_ref.at[slot, pl.ds(r * LANES, LANES)], sem.at[slot]).wait()

    @pl.when(j == 0)
    def _():
        gather(src_ref, j, 0, False)

    @pl.when(j < n_used)
    def _():
        slot = j % 2
        wait_rows(slot)
        first_expert = tg_ref[j] * EXPERTS_PER_GROUP
        h = jnp.concatenate([x_ref[slot, :, s, :] for s in range(FEAT_ROWS)], axis=1).astype(BF16)
        gates = x_ref[slot, :, FEAT_ROWS, :]
        gather(nxt_ref, jnp.minimum(j + 1, n_used - 1), 1 - slot, True)
        lane = lax.broadcasted_iota(jnp.int32, gates.shape, 1)
        acc = None
        for k in range(EXPERTS_PER_GROUP):
            a = _dot(h, wg_ref[k])
            u = _dot(h, wu_ref[k])
            ge = jnp.sum(jnp.where(lane == first_expert + k, gates, 0.0), axis=-1, keepdims=True)
            y = _dot((a * _sigmoid(a) * u * ge).astype(BF16), wd_ref[k])
            acc = y if acc is None else acc + y
        for s in range(FEAT_ROWS):
            y_ref[:, s, :] = acc[:, s * LANES:(s + 1) * LANES]

        @pl.when(j + 1 >= n_used)
        def _():
            wait_rows(1 - slot)

    @pl.when(j >= n_used)
    def _():
        y_ref[...] = jnp.zeros_like(y_ref)


def _experts(l, tile_group, n_used, src, slab, w_gate, w_up, w_down):
    tile = lambda j, tg, nv: jnp.minimum(j, nv[0] - 1)
    w_spec = lambda r, c: pl.BlockSpec((None, None, EXPERTS_PER_GROUP, r, c),
                                       lambda j, tg, nv: (l, tg[tile(j, tg, nv)], 0, 0, 0))
    return pl.pallas_call(
        _experts_kernel,
        grid_spec=pltpu.PrefetchScalarGridSpec(
            num_scalar_prefetch=2,
            grid=(N_SORT_TILES,),
            in_specs=[
                pl.BlockSpec((SUBLANES, LANES), lambda j, tg, nv: (j // _SRC_BLOCK_TILES, 0),
                             memory_space=pltpu.SMEM),
                pl.BlockSpec((SUBLANES, LANES),
                             lambda j, tg, nv: (tile(j + 1, tg, nv) // _SRC_BLOCK_TILES, 0),
                             memory_space=pltpu.SMEM),
                pl.BlockSpec(memory_space=pl.ANY),
                w_spec(D_MODEL, D_FF), w_spec(D_MODEL, D_FF), w_spec(D_FF, D_MODEL),
            ],
            out_specs=pl.BlockSpec((SORT_TILE, FEAT_ROWS, LANES), lambda j, tg, nv: (j, 0, 0)),
            scratch_shapes=[pltpu.VMEM((2, SORT_TILE, SLAB_ROWS, LANES), F32),
                            pltpu.SemaphoreType.DMA((2,))],
        ),
        out_shape=jax.ShapeDtypeStruct((N_SORT_TILES * SORT_TILE, FEAT_ROWS, LANES), F32),
        compiler_params=_params("arbitrary"),
        name="experts",
    )(tile_group, n_used, src, src, slab, w_gate, w_up, w_down)


def _combine_kernel(dst_ref, y_ref, x1_ref, mod_ref, o_ref, buf_ref, sem):
    groups = range(PLAN_TILE // LANES)
    for r in groups:
        _start_rows(lambda c, r=r: pltpu.make_async_copy(
            y_ref.at[dst_ref[r, c]], buf_ref.at[r * LANES + c], sem))
    for r in groups:
        pltpu.make_async_copy(y_ref.at[pl.ds(0, LANES)],
                              buf_ref.at[pl.ds(r * LANES, LANES)], sem).wait()
    for s in range(FEAT_ROWS):
        cols = slice(s * LANES, (s + 1) * LANES)
        o_ref[:, cols] = x1_ref[:, cols] + mod_ref[5:6, cols] * buf_ref[:, s, :]


_PLAN_CTX_TILES = N_CTX // PLAN_TILE
_PLAN_LAT_TILES_PER_SEQ = DEC_SEQ // PLAN_TILE


def _plan_cond(i):
    return jnp.where(i < _PLAN_CTX_TILES, 0, 1 + (i - _PLAN_CTX_TILES) // _PLAN_LAT_TILES_PER_SEQ)


def _combine(l, dst, expert_out, x1, mod):
    return pl.pallas_call(
        _combine_kernel,
        grid=(N_TOK // PLAN_TILE,),
        in_specs=[
            pl.BlockSpec((PLAN_TILE // LANES, LANES), lambda i: (i, 0), memory_space=pltpu.SMEM),
            pl.BlockSpec(memory_space=pl.ANY),
            pl.BlockSpec((PLAN_TILE, D_MODEL), lambda i: (i, 0)),
            pl.BlockSpec((None, None, N_MOD, D_MODEL), lambda i: (l, _plan_cond(i), 0, 0)),
        ],
        out_specs=pl.BlockSpec((PLAN_TILE, D_MODEL), lambda i: (i, 0)),
        out_shape=jax.ShapeDtypeStruct((N_TOK, D_MODEL), F32),
        scratch_shapes=[pltpu.VMEM((PLAN_TILE, FEAT_ROWS, LANES), F32), pltpu.SemaphoreType.DMA(())],
        compiler_params=_params("arbitrary"),
        name="combine",
    )(dst, expert_out, x1, mod)


def _rope_tables():
    pos = np.arange(DEC_SEQ)
    inv_freq = ROPE_THETA ** (-np.arange(0, AXIS_DIM, 2, dtype=np.float32) / AXIS_DIM)
    ang_r = (pos // GRID_W).astype(np.float32)[:, None] * inv_freq[None, :]
    ang_c = (pos % GRID_W).astype(np.float32)[:, None] * inv_freq[None, :]
    ang = jnp.asarray(np.concatenate([ang_r, ang_r, ang_c, ang_c], axis=-1).astype(np.float32))
    sign = np.where(np.arange(HEAD_DIM) % AXIS_DIM < AXIS_DIM // 2, -1.0, 1.0).astype(np.float32)
    cos = jnp.concatenate([jnp.cos(ang), jnp.ones((TOK_TILE, HEAD_DIM), F32)], axis=0)
    sin = jnp.concatenate([jnp.sin(ang) * sign, jnp.zeros((TOK_TILE, HEAD_DIM), F32)], axis=0)
    return jnp.tile(cos, (1, N_Q_HEADS)), jnp.tile(sin, (1, N_Q_HEADS))


def _selectors():
    lane = np.arange(N_EXPERTS)
    member = np.stack([(lane[:, None] == (lane[None, :] // EXPERTS_PER_GROUP) * EXPERTS_PER_GROUP + k)
                       for k in range(EXPERTS_PER_GROUP)]).astype(np.float32)
    group = np.stack([np.broadcast_to(lane[:, None] == m * EXPERTS_PER_GROUP, (N_EXPERTS, N_EXPERTS))
                      for m in range(N_GROUPS)]).astype(np.float32)
    head = (np.arange(ATTN_WIDTH)[:, None] // HEAD_DIM == np.arange(ATTN_WIDTH)[None, :] // HEAD_DIM)
    return jnp.asarray(member), jnp.asarray(group), jnp.asarray(head.astype(np.float32), dtype=BF16)


def kernel(x_prompt, x_sample, cache_k, cache_v, state_fwd, state_bwd, c, c_ctx, norm1_w, norm2_w, w_mod, b_mod, w_in, q_norm_w, k_norm_w, hgrn_lb_fwd, hgrn_lb_bwd, hgrn_norm_w, w_br_attn, w_br_hgrn, w_out, w_router, b_router, w_exp_gate, w_exp_up, w_exp_down):
    cos_t, sin_t = _rope_tables()
    sel_member, sel_group, head_ones = _selectors()
    sums_f, masks_f = _hgrn_tables(False)
    sums_b, masks_b = _hgrn_tables(True)
    hgrn_tables = (sums_f, sums_b, masks_f, masks_b)
    cond = jnp.concatenate([c_ctx[None, :], c, jnp.zeros((N_COND - 1 - DEC_BATCH, D_MODEL), F32)], axis=0)
    mod = _modulation(cond, w_mod, b_mod)

    w_in_b = w_in.astype(BF16)
    w_ba_b = w_br_attn.astype(BF16)
    w_bh_b = w_br_hgrn.astype(BF16)
    w_out_b = w_out.astype(BF16)
    by_group = lambda w: w.astype(BF16).reshape(DEPTH, N_GROUPS, EXPERTS_PER_GROUP, *w.shape[2:])
    w_eg_b, w_eu_b, w_ed_b = by_group(w_exp_gate), by_group(w_exp_up), by_group(w_exp_down)
    lane_pad = jnp.eye(N_EXPERTS, LANES, dtype=F32)
    b_router_r = b_router.reshape(1, N_EXPERTS)
    norm1_r = norm1_w.reshape(DEPTH, 1, D_MODEL)
    norm2_r = norm2_w.reshape(DEPTH, 1, D_MODEL)
    hnorm_r = hgrn_norm_w.reshape(DEPTH, 1, HGRN_V)
    qn_r = jnp.tile(q_norm_w, (1, N_Q_HEADS)).reshape(DEPTH, 1, ATTN_WIDTH)
    kn_r = jnp.tile(k_norm_w, (1, N_KV_HEADS)).reshape(DEPTH, 1, KV_WIDTH)

    x = jnp.concatenate([x_prompt.reshape(N_CTX, D_MODEL), x_sample.reshape(N_LAT, D_MODEL)], axis=0)
    ks_out, vs_out, sf_out, sb_out = [], [], [], []
    for l in range(DEPTH):
        q, k, v, h5, gates = _input_projection(l, x, mod, norm1_r, w_in_b, cos_t, sin_t, qn_r, kn_r,
                                               head_ones)
        attn = (_attention_ctx(q, k, v), _attention_lat(l, q, k, v, cache_k, cache_v))
        rec_c, sf, sb = _hgrn(l, h5, 0, BATCH, SEQ, hgrn_lb_fwd, hgrn_lb_bwd, hnorm_r, hgrn_tables)
        rec_s = _hgrn(l, h5, N_CTX, DEC_BATCH, DEC_SEQ, hgrn_lb_fwd, hgrn_lb_bwd, hnorm_r,
                      hgrn_tables, init=(state_fwd, state_bwd))
        rec = (rec_c, rec_s)
        x1, slab, route = _merge(l, attn, rec, gates, x, mod, norm2_r, w_ba_b, w_bh_b, w_out_b,
                                 w_router, b_router_r, sel_member, sel_group, lane_pad)
        dst, tile_group, n_used = _plan(route)
        expert_out = _experts(l, tile_group, n_used, _invert(dst), slab, w_eg_b, w_eu_b, w_ed_b)
        x = _combine(l, dst, expert_out, x1, mod)
        ks_out.append(k[:N_CTX].reshape(BATCH, SEQ, N_KV_HEADS, HEAD_DIM).transpose(0, 2, 1, 3))
        vs_out.append(v[:N_CTX].reshape(BATCH, SEQ, N_KV_HEADS, HEAD_DIM).transpose(0, 2, 1, 3))
        sf_out.append(sf)
        sb_out.append(sb)

    y_prompt = x[:N_CTX].reshape(BATCH, SEQ, D_MODEL)
    y_sample = x[N_CTX:].reshape(DEC_BATCH, DEC_SEQ, D_MODEL)
    return (y_prompt, y_sample, jnp.stack(ks_out, axis=1), jnp.stack(vs_out, axis=1),
            jnp.stack(sf_out, axis=1), jnp.stack(sb_out, axis=1))
```
